```python
import math
import jax, jax.numpy as jnp
from jax import lax
import numpy as np

D_MODEL = 1024
BATCH = 32
SEQ = 2048
DEPTH = 2
DEC_BATCH = 4
DEC_SEQ = 4096
PAST_LEN = 128

FN_WIDTH = D_MODEL // 4
FN_HEADS = 4
FN_HEAD_DIM = FN_WIDTH // FN_HEADS
SSM_WIDTH = D_MODEL // 4
SSM_GROUP = 16
SSM_GROUPS = SSM_WIDTH // SSM_GROUP
SSM_STATE = 64
DT_MIN = 1e-3
DT_MAX = 1e-1
ATT_WIDTH = D_MODEL // 2
ATT_HEAD_DIM = 64
ATT_HEADS = ATT_WIDTH // ATT_HEAD_DIM
DILATED_PATTERNS = ((128, 1), (512, 4), (2048, 16))
IN_PROJ_WIDTH = FN_WIDTH + SSM_WIDTH + 3 * ATT_WIDTH
N_EXPERTS = 16
EC_CAPACITY_FACTOR = 2
D_FF_EXPERT = 2 * D_MODEL
PLE_DIM = 256
RMS_EPS = 1e-6
NEG_INF = -1e30

kernel_name = 'hybrid_fnet_s5_dilated_ec_encoder'


def rms_norm(x, g):
    x32 = x.astype(jnp.float32)
    y = x32 * lax.rsqrt(jnp.mean(x32 * x32, axis=-1, keepdims=True) + RMS_EPS)
    return (y * g.astype(jnp.float32)).astype(x.dtype)


def fourier_mixer(z, w_fnet):
    b, s, _ = z.shape
    zf = z.astype(jnp.float32).reshape(b, s, FN_HEADS, FN_HEAD_DIM)
    f = jnp.real(jnp.fft.fft2(zf, axes=(1, 3)))
    out = jnp.einsum('bshc,hce->bshe', f, w_fnet.astype(jnp.float32))
    return out.reshape(b, s, FN_WIDTH)


def _s5_direction(u, a_re, a_im, log_dt, b_re, b_im, c_re, c_im, reverse):
    f32 = jnp.float32
    s = u.shape[1]
    lam = lax.complex(a_re.astype(f32), a_im.astype(f32))
    dt = jnp.exp(log_dt.astype(f32))[:, None]
    abar = jnp.exp(lam * dt)
    bmat = lax.complex(b_re.astype(f32), b_im.astype(f32))
    bbar = ((abar - 1.0) / lam)[..., None] * bmat
    bu = jnp.einsum('bsgc,gpc->bsgp', u.astype(jnp.complex64), bbar)
    a = jnp.broadcast_to(abar[None, None], (1, s) + abar.shape)

    def combine(e1, e2):
        a1, b1 = e1
        a2, b2 = e2
        return a1 * a2, a2 * b1 + b2

    _, states = lax.associative_scan(combine, (a, bu), axis=1, reverse=reverse)
    cmat = lax.complex(c_re.astype(f32), c_im.astype(f32))
    return jnp.real(jnp.einsum('bsgp,gcp->bsgc', states, cmat))


def s5_mixer(z, a_re, a_im, log_dt, b_re, b_im, c_re, c_im, d_skip, glu_w, glu_b):
    b, s, _ = z.shape
    f32 = jnp.float32
    u = z.astype(f32).reshape(b, s, SSM_GROUPS, SSM_GROUP)
    y_fw = _s5_direction(u, a_re[0], a_im[0], log_dt[0], b_re[0], b_im[0], c_re[0], c_im[0], False)
    y_bw = _s5_direction(u, a_re[1], a_im[1], log_dt[1], b_re[1], b_im[1], c_re[1], c_im[1], True)
    y = (y_fw + y_bw).reshape(b, s, SSM_WIDTH) + d_skip.astype(f32) * u.reshape(b, s, SSM_WIDTH)
    g = jax.nn.gelu(y)
    return g * jax.nn.sigmoid(g @ glu_w.astype(f32) + glu_b.astype(f32))


def alibi_slopes(n_heads):
    return 2.0 ** (-8.0 * jnp.arange(1, n_heads + 1, dtype=jnp.float32) / n_heads)


def _dilated_window_attention(q, k, v, slopes, window, dilation):
    b, s, h, dh = q.shape
    half = window // (2 * dilation)
    blk = half
    L = s // dilation
    nb = -(-L // blk)
    Lp = nb * blk

    def to_classes(t):
        t = t.reshape(b, L, dilation, h, dh).transpose(0, 2, 1, 3, 4).reshape(b * dilation, L, h, dh)
        return jnp.pad(t, ((0, 0), (0, Lp - L), (0, 0), (0, 0)))

    def band(t):
        tp = jnp.pad(t, ((0, 0), (blk, blk), (0, 0), (0, 0))).reshape(-1, nb + 2, blk, h, dh)
        return jnp.concatenate([tp[:, :-2], tp[:, 1:-1], tp[:, 2:]], axis=2)

    qb = to_classes(q).reshape(-1, nb, blk, h, dh)
    kb = band(to_classes(k))
    vb = band(to_classes(v))
    qi = jnp.arange(nb)[:, None] * blk + jnp.arange(blk)[None, :]
    kj = jnp.arange(nb)[:, None] * blk - blk + jnp.arange(3 * blk)[None, :]
    rel = jnp.abs(qi[:, :, None] - kj[:, None, :])
    valid = (rel <= half) & (kj[:, None, :] >= 0) & (kj[:, None, :] < L)
    dist = (dilation * rel).astype(jnp.float32)
    scores = jnp.einsum('znqhd,znkhd->zhnqk', qb, kb) - slopes[None, :, None, None, None] * dist[None, None]
    scores = jnp.where(valid[None, None], scores, NEG_INF)
    m = jnp.max(scores, axis=-1, keepdims=True)
    pr = jnp.exp(scores - m)
    den = jnp.sum(pr, axis=-1)
    o = jnp.einsum('zhnqk,znkhd->znqhd', pr, vb) / den.transpose(0, 2, 3, 1)[..., None]
    lse = (m[..., 0] + jnp.log(den)).transpose(0, 2, 3, 1)
    o = o.reshape(-1, Lp, h, dh)[:, :L].reshape(b, dilation, L, h, dh).transpose(0, 2, 1, 3, 4).reshape(b, s, h, dh)
    lse = lse.reshape(-1, Lp, h)[:, :L].reshape(b, dilation, L, h).transpose(0, 2, 1, 3).reshape(b, s, h)
    return o, lse


def dilated_attention_mixer(zq, zk, zv):
    b, s, _ = zq.shape
    shp = (b, s, ATT_HEADS, ATT_HEAD_DIM)
    q = zq.astype(jnp.float32).reshape(shp) * (ATT_HEAD_DIM ** -0.5)
    k = zk.astype(jnp.float32).reshape(shp)
    v = zv.astype(jnp.float32).reshape(shp)
    slopes = alibi_slopes(ATT_HEADS)
    outs, lses = [], []
    for window, dilation in DILATED_PATTERNS:
        o, lse = _dilated_window_attention(q, k, v, slopes, window, dilation)
        outs.append(o)
        lses.append(lse)
    w = jax.nn.softmax(jnp.stack(lses), axis=0)
    o = jnp.sum(w[..., None] * jnp.stack(outs), axis=0)
    return o.reshape(b, s, ATT_WIDTH)


def expert_choice_ffn(m, w_router, w_gate, w_up, w_down):
    b, s, d = m.shape
    n = b * s
    xt = m.reshape(n, d)
    cap = EC_CAPACITY_FACTOR * n // N_EXPERTS
    aff = jax.nn.softmax((xt @ w_router).astype(jnp.float32), axis=-1)
    gates, idx = lax.top_k(aff.T, cap)
    xe = jnp.take(xt, idx, axis=0)
    hdn = jax.nn.silu(jnp.einsum('ecd,edf->ecf', xe, w_gate)) * jnp.einsum('ecd,edf->ecf', xe, w_up)
    ye = jnp.einsum('ecf,efd->ecd', hdn, w_down) * gates[..., None].astype(xe.dtype)
    out = jnp.zeros_like(xt).at[idx.reshape(-1)].add(ye.reshape(-1, d))
    return out.reshape(b, s, d)


def _trunk(x, p, norm_mix, w_in, w_fnet, ssm_a_re, ssm_a_im, ssm_log_dt, ssm_b_re, ssm_b_im,
           ssm_c_re, ssm_c_im, ssm_d, ssm_glu_w, ssm_glu_b, norm_branch, w_out, norm_ffn, w_router,
           w_exp_gate, w_exp_up, w_exp_down, norm_ple, w_ple_gate, w_ple_proj, norm_final):
    o1 = FN_WIDTH
    o2 = o1 + SSM_WIDTH
    o3 = o2 + ATT_WIDTH
    o4 = o3 + ATT_WIDTH
    h = x
    for l in range(DEPTH):
        a = rms_norm(h, norm_mix[l])
        z = a @ w_in[l]
        out_f = fourier_mixer(z[..., :o1], w_fnet[l])
        out_s = s5_mixer(z[..., o1:o2], ssm_a_re[l], ssm_a_im[l], ssm_log_dt[l], ssm_b_re[l], ssm_b_im[l],
                         ssm_c_re[l], ssm_c_im[l], ssm_d[l], ssm_glu_w[l], ssm_glu_b[l])
        out_a = dilated_attention_mixer(z[..., o2:o3], z[..., o3:o4], z[..., o4:])
        gb = norm_branch[l]
        mixed = jnp.concatenate([rms_norm(out_f, gb[:o1]), rms_norm(out_s, gb[o1:o2]),
                                 rms_norm(out_a, gb[o2:])], axis=-1).astype(h.dtype)
        h = h + mixed @ w_out[l]
        h = h + expert_choice_ffn(rms_norm(h, norm_ffn[l]), w_router[l], w_exp_gate[l], w_exp_up[l], w_exp_down[l])
        gate = jax.nn.sigmoid(rms_norm(h, norm_ple[l]) @ w_ple_gate[l])
        h = h + (p[l] @ w_ple_proj[l]) * gate
    return rms_norm(h, norm_final)


def setup_inputs(seed: int = 0) -> dict:
    key = jax.random.key(seed)
    ks = iter(jax.random.split(key, 40))
    f32 = jnp.float32

    def nrm(shape, scale):
        return scale * jax.random.normal(next(ks), shape, f32)

    def gain(shape):
        return 1.0 + 0.01 * jax.random.normal(next(ks), shape, f32)

    gp = (DEPTH, 2, SSM_GROUPS, SSM_STATE)
    a_im_base = math.pi * jnp.arange(SSM_STATE, dtype=f32)
    return {
        'x_prompt': nrm((BATCH, SEQ, D_MODEL), 1.0),
        'x_sample': nrm((DEC_BATCH, DEC_SEQ, D_MODEL), 1.0),
        'p_prompt': nrm((DEPTH, BATCH, SEQ, PLE_DIM), 1.0),
        'p_sample': nrm((DEPTH, DEC_BATCH, DEC_SEQ, PLE_DIM), 1.0),
        'norm_mix': gain((DEPTH, D_MODEL)),
        'w_in': nrm((DEPTH, D_MODEL, IN_PROJ_WIDTH), D_MODEL ** -0.5),
        'w_fnet': nrm((DEPTH, FN_HEADS, FN_HEAD_DIM, FN_HEAD_DIM), FN_HEAD_DIM ** -0.5),
        'ssm_a_re': -0.5 * jnp.exp(nrm(gp, 0.05)),
        'ssm_a_im': a_im_base + nrm(gp, 0.05),
        'ssm_log_dt': jax.random.uniform(next(ks), (DEPTH, 2, SSM_GROUPS), f32, math.log(DT_MIN), math.log(DT_MAX)),
        'ssm_b_re': nrm(gp + (SSM_GROUP,), (2 * SSM_GROUP) ** -0.5),
        'ssm_b_im': nrm(gp + (SSM_GROUP,), (2 * SSM_GROUP) ** -0.5),
        'ssm_c_re': nrm((DEPTH, 2, SSM_GROUPS, SSM_GROUP, SSM_STATE), (2 * SSM_STATE) ** -0.5),
        'ssm_c_im': nrm((DEPTH, 2, SSM_GROUPS, SSM_GROUP, SSM_STATE), (2 * SSM_STATE) ** -0.5),
        'ssm_d': nrm((DEPTH, SSM_WIDTH), 1.0),
        'ssm_glu_w': nrm((DEPTH, SSM_WIDTH, SSM_WIDTH), SSM_WIDTH ** -0.5),
        'ssm_glu_b': nrm((DEPTH, SSM_WIDTH), 0.01),
        'norm_branch': gain((DEPTH, D_MODEL)),
        'w_out': nrm((DEPTH, D_MODEL, D_MODEL), D_MODEL ** -0.5),
        'norm_ffn': gain((DEPTH, D_MODEL)),
        'w_router': nrm((DEPTH, D_MODEL, N_EXPERTS), D_MODEL ** -0.5),
        'w_exp_gate': nrm((DEPTH, N_EXPERTS, D_MODEL, D_FF_EXPERT), D_MODEL ** -0.5),
        'w_exp_up': nrm((DEPTH, N_EXPERTS, D_MODEL, D_FF_EXPERT), D_MODEL ** -0.5),
        'w_exp_down': nrm((DEPTH, N_EXPERTS, D_FF_EXPERT, D_MODEL), D_FF_EXPERT ** -0.5),
        'norm_ple': gain((DEPTH, D_MODEL)),
        'w_ple_gate': nrm((DEPTH, D_MODEL, D_MODEL), D_MODEL ** -0.5),
        'w_ple_proj': nrm((DEPTH, PLE_DIM, D_MODEL), PLE_DIM ** -0.5),
        'norm_final': gain((D_MODEL,)),
    }


def reference(x_prompt, x_sample, p_prompt, p_sample, norm_mix, w_in, w_fnet, ssm_a_re, ssm_a_im,
              ssm_log_dt, ssm_b_re, ssm_b_im, ssm_c_re, ssm_c_im, ssm_d, ssm_glu_w, ssm_glu_b,
              norm_branch, w_out, norm_ffn, w_router, w_exp_gate, w_exp_up, w_exp_down,
              norm_ple, w_ple_gate, w_ple_proj, norm_final):
    params = (norm_mix, w_in, w_fnet, ssm_a_re, ssm_a_im, ssm_log_dt, ssm_b_re, ssm_b_im, ssm_c_re,
              ssm_c_im, ssm_d, ssm_glu_w, ssm_glu_b, norm_branch, w_out, norm_ffn, w_router,
              w_exp_gate, w_exp_up, w_exp_down, norm_ple, w_ple_gate, w_ple_proj, norm_final)
    y_prompt = _trunk(x_prompt, p_prompt, *params)
    y_sample = _trunk(x_sample, p_sample, *params)
    return (y_prompt, y_sample)
```

```python
import functools
import math

import jax
import jax.numpy as jnp
from jax import lax
from jax.experimental import pallas as pl
from jax.experimental.pallas import tpu as pltpu

D_MODEL = 1024
FN_WIDTH = 256
FN_HEADS = 4
FN_HEAD_DIM = 64
SSM_WIDTH = 256
SSM_GROUP = 16
SSM_GROUPS = 16
SSM_STATE = 64
ATT_WIDTH = 512
ATT_HEAD_DIM = 64
ATT_HEADS = 8
DILATED_PATTERNS = ((128, 1), (512, 4), (2048, 16))
IN_PROJ_WIDTH = 2048
N_EXPERTS = 16
EC_CAPACITY_FACTOR = 2
D_FF_EXPERT = 2048
PLE_DIM = 256
RMS_EPS = 1e-6
NEG_INF = -1e30

LANES = 128
SSM_CHUNK = 8
SSM_ROW = SSM_CHUNK * SSM_WIDTH
SSM_NSTATE = SSM_GROUPS * SSM_STATE
ATT_HALF = 64
COMBINE_TILE = 256
COMBINE_WIN = 64
VMEM_LIMIT = 56 * 1024 * 1024

F32 = jnp.float32
BF16 = jnp.bfloat16
I32 = jnp.int32


def _cparams(sem):
    return pltpu.CompilerParams(dimension_semantics=sem, vmem_limit_bytes=VMEM_LIMIT)


def _rms(x, g):
    return x * lax.rsqrt(jnp.mean(x * x, axis=-1, keepdims=True) + RMS_EPS) * g


def _dot(a, b):
    return jnp.dot(a, b, preferred_element_type=F32)


def _dot_nt(a, b):
    return lax.dot_general(a, b, (((1,), (1,)), ((), ())), preferred_element_type=F32)


def _inproj_body(h_ref, g_ref, w_ref, zf_ref, zs_ref, zqkv_ref):
    a = _rms(h_ref[...], g_ref[...]).astype(BF16)
    z = _dot(a, w_ref[...])
    zf_ref[...] = z[:, :FN_WIDTH]
    zs_ref[...] = z[:, FN_WIDTH:FN_WIDTH + SSM_WIDTH]
    zqkv_ref[...] = z[:, FN_WIDTH + SSM_WIDTH:].astype(BF16)


def _inproj(h, g, w):
    n = h.shape[0]
    tm = 512
    return pl.pallas_call(
        _inproj_body,
        grid=(n // tm,),
        in_specs=[pl.BlockSpec((tm, D_MODEL), lambda i: (i, 0)),
                  pl.BlockSpec((1, D_MODEL), lambda i: (0, 0)),
                  pl.BlockSpec((D_MODEL, IN_PROJ_WIDTH), lambda i: (0, 0))],
        out_specs=[pl.BlockSpec((tm, FN_WIDTH), lambda i: (i, 0)),
                   pl.BlockSpec((tm, SSM_WIDTH), lambda i: (i, 0)),
                   pl.BlockSpec((tm, 3 * ATT_WIDTH), lambda i: (i, 0))],
        out_shape=[jax.ShapeDtypeStruct((n, FN_WIDTH), F32),
                   jax.ShapeDtypeStruct((n, SSM_WIDTH), F32),
                   jax.ShapeDtypeStruct((n, 3 * ATT_WIDTH), BF16)],
        compiler_params=_cparams(("arbitrary",)),
        name="inproj",
    )(h, g, w)


def _fourier_body(x_ref, cs_ref, ss_ref, cc_ref, sc_ref, wb_ref, g_ref, o_ref):
    x = x_ref[0].astype(BF16)
    y = _dot(cs_ref[...], x).astype(BF16)
    z = _dot(ss_ref[...], x).astype(BF16)
    f = _dot(y, cc_ref[...]) - _dot(z, sc_ref[...])
    o = _dot(f.astype(BF16), wb_ref[...])
    o_ref[0] = _rms(o, g_ref[...]).astype(BF16)


def _dft_tables(n):
    k = (jnp.arange(n, dtype=I32)[:, None] * jnp.arange(n, dtype=I32)[None, :]) % n
    ang = k.astype(F32) * (2.0 * math.pi / n)
    return jnp.cos(ang), jnp.sin(ang)


def _fourier(zf, cs, ss, ccb, scb, wb, g, b, s):
    tr = 512
    x = zf.reshape(b, s, FN_WIDTH)
    out = pl.pallas_call(
        _fourier_body,
        grid=(s // tr, b),
        in_specs=[pl.BlockSpec((1, s, FN_WIDTH), lambda i, j: (j, 0, 0)),
                  pl.BlockSpec((tr, s), lambda i, j: (i, 0)),
                  pl.BlockSpec((tr, s), lambda i, j: (i, 0)),
                  pl.BlockSpec((FN_WIDTH, FN_WIDTH), lambda i, j: (0, 0)),
                  pl.BlockSpec((FN_WIDTH, FN_WIDTH), lambda i, j: (0, 0)),
                  pl.BlockSpec((FN_WIDTH, FN_WIDTH), lambda i, j: (0, 0)),
                  pl.BlockSpec((1, FN_WIDTH), lambda i, j: (0, 0))],
        out_specs=pl.BlockSpec((1, tr, FN_WIDTH), lambda i, j: (j, i, 0)),
        out_shape=jax.ShapeDtypeStruct((b, s, FN_WIDTH), BF16),
        compiler_params=_cparams(("arbitrary", "arbitrary")),
        name="fourier",
    )(x, cs, ss, ccb, scb, wb, g)
    return out.reshape(b * s, FN_WIDTH)


def _block_diag(blocks):
    h, a, bb = blocks.shape
    eye = jnp.eye(h, dtype=blocks.dtype)
    return jnp.einsum('hab,hg->hagb', blocks, eye).reshape(h * a, h * bb)


def _s5_matrices(a_re, a_im, log_dt, b_re, b_im, c_re, c_im):
    t = SSM_CHUNK
    g, p, c = SSM_GROUPS, SSM_STATE, SSM_GROUP
    lam = lax.complex(a_re.astype(F32), a_im.astype(F32))
    dt = jnp.exp(log_dt.astype(F32))[..., None]
    abar = jnp.exp(lam * dt)
    bbar = ((abar - 1.0) / lam)[..., None] * lax.complex(b_re.astype(F32), b_im.astype(F32))
    cmat = lax.complex(c_re.astype(F32), c_im.astype(F32))
    ks = jnp.arange(t + 1, dtype=F32)
    apow = jnp.exp((lam * dt)[:, None] * ks[None, :, None, None])
    eye = jnp.eye(g, dtype=F32)

    kern = jnp.real(jnp.einsum('dgcp,dkgp,dgpe->dkgce', cmat, apow[:, :t], bbar))
    lag = jnp.arange(t)[None, :] - jnp.arange(t)[:, None]
    kf = jnp.where((lag >= 0)[:, :, None, None, None], kern[0][jnp.clip(lag, 0, t - 1)], 0.0)
    kb = jnp.where((lag <= 0)[:, :, None, None, None], kern[1][jnp.clip(-lag, 0, t - 1)], 0.0)
    ktot = kf + kb
    m_intra = jnp.einsum('abgce,gh->agebhc', ktot, eye).reshape(SSM_ROW, SSM_ROW)

    wf = apow[0, t - 1 - jnp.arange(t)][:, :, :, None] * bbar[0][None]
    wb = apow[1, jnp.arange(t)][:, :, :, None] * bbar[1][None]
    def _w_in(w):
        w6 = jnp.einsum('rgpe,gh->rgehp', w, eye.astype(w.dtype))
        return w6.reshape(SSM_ROW, SSM_NSTATE)
    wf, wb = _w_in(wf), _w_in(wb)
    w_in = jnp.concatenate([jnp.real(wf), jnp.imag(wf), jnp.real(wb), jnp.imag(wb)], axis=1)

    qf = cmat[0][None] * apow[0, 1 + jnp.arange(t)][:, :, None, :]
    qb = cmat[1][None] * apow[1, t - jnp.arange(t)][:, :, None, :]
    def _w_out(q):
        q6 = jnp.einsum('rgcp,gh->gprhc', q, eye.astype(q.dtype))
        return q6.reshape(SSM_NSTATE, SSM_ROW)
    qf, qb = _w_out(qf), _w_out(qb)
    w_out = jnp.concatenate([jnp.real(qf), -jnp.imag(qf), jnp.real(qb), -jnp.imag(qb)], axis=0)

    a_chunk = lam * dt * t
    return m_intra.astype(BF16), w_in.astype(BF16), w_out.astype(BF16), a_chunk


def _scan_powers(a_chunk, nsteps):
    e = jnp.exp(a_chunk[None] * (2.0 ** jnp.arange(nsteps, dtype=F32))[:, None, None, None])
    e = e.reshape(nsteps, 2 * SSM_NSTATE)
    return jnp.real(e), jnp.imag(e)


def _s5_in_body(u_ref, w_ref, o_ref):
    o_ref[...] = _dot(u_ref[...].astype(BF16), w_ref[...])


def _s5_in(u, w_in):
    rows = u.shape[0]
    tr, tc = min(512, rows), 1024
    width = 4 * SSM_NSTATE
    return pl.pallas_call(
        _s5_in_body,
        grid=(width // tc, rows // tr),
        in_specs=[pl.BlockSpec((tr, SSM_ROW), lambda j, i: (i, 0)),
                  pl.BlockSpec((SSM_ROW, tc), lambda j, i: (0, j))],
        out_specs=pl.BlockSpec((tr, tc), lambda j, i: (i, j)),
        out_shape=jax.ShapeDtypeStruct((rows, width), F32),
        compiler_params=_cparams(("arbitrary", "arbitrary")),
        name="s5_in",
    )(u, w_in)


def _s5_scan_body(x_ref, pr_ref, pi_ref, o_ref, *, nc, nsteps):
    ns = SSM_NSTATE
    x = x_ref[0]
    fre, fim = x[:, 0:ns], x[:, ns:2 * ns]
    bre, bim = x[:, 2 * ns:3 * ns], x[:, 3 * ns:4 * ns]
    row = lax.broadcasted_iota(I32, (nc, 1), 0)
    for k in range(nsteps):
        sh = 2 ** k
        far, fai = pr_ref[k:k + 1, 0:ns], pi_ref[k:k + 1, 0:ns]
        bar, bai = pr_ref[k:k + 1, ns:2 * ns], pi_ref[k:k + 1, ns:2 * ns]
        fmask = row >= sh
        sre = jnp.where(fmask, pltpu.roll(fre, sh, 0), 0.0)
        sim = jnp.where(fmask, pltpu.roll(fim, sh, 0), 0.0)
        fre, fim = fre + far * sre - fai * sim, fim + far * sim + fai * sre
        bmask = row < nc - sh
        sre = jnp.where(bmask, pltpu.roll(bre, nc - sh, 0), 0.0)
        sim = jnp.where(bmask, pltpu.roll(bim, nc - sh, 0), 0.0)
        bre, bim = bre + bar * sre - bai * sim, bim + bar * sim + bai * sre
    fmask = row >= 1
    bmask = row < nc - 1
    o_ref[0, :, 0:ns] = jnp.where(fmask, pltpu.roll(fre, 1, 0), 0.0).astype(BF16)
    o_ref[0, :, ns:2 * ns] = jnp.where(fmask, pltpu.roll(fim, 1, 0), 0.0).astype(BF16)
    o_ref[0, :, 2 * ns:3 * ns] = jnp.where(bmask, pltpu.roll(bre, nc - 1, 0), 0.0).astype(BF16)
    o_ref[0, :, 3 * ns:4 * ns] = jnp.where(bmask, pltpu.roll(bim, nc - 1, 0), 0.0).astype(BF16)


def _s5_scan(xl, pr, pi, b, nc):
    nsteps = int(math.log2(nc))
    width = 4 * SSM_NSTATE
    x = xl.reshape(b, nc, width)
    out = pl.pallas_call(
        functools.partial(_s5_scan_body, nc=nc, nsteps=nsteps),
        grid=(b,),
        in_specs=[pl.BlockSpec((1, nc, width), lambda i: (i, 0, 0)),
                  pl.BlockSpec((nsteps, 2 * SSM_NSTATE), lambda i: (0, 0)),
                  pl.BlockSpec((nsteps, 2 * SSM_NSTATE), lambda i: (0, 0))],
        out_specs=pl.BlockSpec((1, nc, width), lambda i: (i, 0, 0)),
        out_shape=jax.ShapeDtypeStruct((b, nc, width), BF16),
        compiler_params=_cparams(("arbitrary",)),
        name="s5_scan",
    )(x, pr, pi)
    return out.reshape(b * nc, width)


def _s5_out_body(u_ref, s_ref, m_ref, w_ref, o_ref):
    o_ref[...] = _dot(u_ref[...].astype(BF16), m_ref[...]) + _dot(s_ref[...], w_ref[...])


def _s5_out(u, states, m_intra, w_out):
    rows = u.shape[0]
    tr, tc = min(512, rows), 512
    width = 4 * SSM_NSTATE
    return pl.pallas_call(
        _s5_out_body,
        grid=(SSM_ROW // tc, rows // tr),
        in_specs=[pl.BlockSpec((tr, SSM_ROW), lambda j, i: (i, 0)),
                  pl.BlockSpec((tr, width), lambda j, i: (i, 0)),
                  pl.BlockSpec((SSM_ROW, tc), lambda j, i: (0, j)),
                  pl.BlockSpec((width, tc), lambda j, i: (0, j))],
        out_specs=pl.BlockSpec((tr, tc), lambda j, i: (i, j)),
        out_shape=jax.ShapeDtypeStruct((rows, SSM_ROW), F32),
        compiler_params=_cparams(("arbitrary", "arbitrary")),
        name="s5_out",
    )(u, states, m_intra, w_out)


def _s5_post_body(y_ref, u_ref, d_ref, w_ref, b_ref, g_ref, o_ref):
    v = y_ref[...] + d_ref[...] * u_ref[...]
    c0 = math.sqrt(2.0 / math.pi)
    gl = 0.5 * v * (1.0 + jnp.tanh(c0 * (v + 0.044715 * (v * v * v))))
    gate = jax.nn.sigmoid(_dot(gl.astype(BF16), w_ref[...]) + b_ref[...])
    o_ref[...] = _rms(gl * gate, g_ref[...]).astype(BF16)


def _s5_post(y, u, d, w, bias, g):
    n = y.shape[0]
    tm = 2048
    row = lambda i: (i, 0)
    fix = lambda i: (0, 0)
    return pl.pallas_call(
        _s5_post_body,
        grid=(n // tm,),
        in_specs=[pl.BlockSpec((tm, SSM_WIDTH), row), pl.BlockSpec((tm, SSM_WIDTH), row),
                  pl.BlockSpec((1, SSM_WIDTH), fix), pl.BlockSpec((SSM_WIDTH, SSM_WIDTH), fix),
                  pl.BlockSpec((1, SSM_WIDTH), fix), pl.BlockSpec((1, SSM_WIDTH), fix)],
        out_specs=pl.BlockSpec((tm, SSM_WIDTH), row),
        out_shape=jax.ShapeDtypeStruct((n, SSM_WIDTH), BF16),
        compiler_params=_cparams(("arbitrary",)),
        name="s5_post",
    )(y, u, d, w, bias, g)


def _attn_body(slope_ref, q_ref, k_ref, v_ref, o_ref, qf, kf, vf, *acc, s):
    hp = pl.program_id(1)
    qf[...] = q_ref[...].astype(F32)
    kf[...] = k_ref[...].astype(F32)
    vf[...] = v_ref[...].astype(F32)
    lane = lax.broadcasted_iota(I32, (1, LANES), 1)
    first = lane < ATT_HEAD_DIM
    slopes = (slope_ref[2 * hp], slope_ref[2 * hp + 1])
    scale = ATT_HEAD_DIM ** -0.5

    for p, (_, d) in enumerate(DILATED_PATTERNS):
        acc_o, acc_m, acc_l = acc[3 * p], acc[3 * p + 1], acc[3 * p + 2]
        ln = s // d
        bq = min(128, ln)
        bk = min(bq + 2 * ATT_HALF, ln)
        nqb = ln // bq

        def rows(start, size, d=d):
            return pl.ds(start, size) if d == 1 else pl.ds(start, size, stride=d)

        def block(i, carry, d=d, ln=ln, bq=bq, bk=bk, nqb=nqb, rows=rows,
                  acc_o=acc_o, acc_m=acc_m, acc_l=acc_l):
            c = i // nqb
            j0 = (i % nqb) * bq
            ks = jnp.clip(j0 - ATT_HALF, 0, ln - bk)
            q = qf[rows(c + d * j0, bq), :]
            k = kf[rows(c + d * ks, bk), :].astype(BF16)
            v = vf[rows(c + d * ks, bk), :].astype(BF16)
            jq = j0 + lax.broadcasted_iota(I32, (bq, 1), 0)
            jk = ks + lax.broadcasted_iota(I32, (1, bk), 1)
            rel = jnp.abs(jq - jk)
            valid = rel <= ATT_HALF
            dist = (d * rel).astype(F32)
            res = []
            for hh in range(2):
                qm = jnp.where(first if hh == 0 else jnp.logical_not(first), q, 0.0).astype(BF16)
                sc = _dot_nt(qm, k) * scale - slopes[hh] * dist
                sc = jnp.where(valid, sc, NEG_INF)
                m = jnp.max(sc, axis=-1, keepdims=True)
                pe = jnp.exp(sc - m)
                l = jnp.sum(pe, axis=-1, keepdims=True)
                res.append((_dot(pe.astype(BF16), v), m, l))
            dst = rows(c + d * j0, bq)
            acc_o[dst, :] = jnp.where(first, res[0][0], res[1][0])
            acc_m[dst, :] = jnp.where(first, res[0][1], res[1][1])
            acc_l[dst, :] = jnp.where(first, res[0][2], res[1][2])
            return carry

        lax.fori_loop(0, d * nqb, block, 0)

    m = jnp.maximum(jnp.maximum(acc[1][...], acc[4][...]), acc[7][...])
    num = jnp.zeros((s, LANES), F32)
    den = jnp.zeros((s, LANES), F32)
    for p in range(3):
        w = jnp.exp(acc[3 * p + 1][...] - m)
        num = num + w * acc[3 * p][...]
        den = den + w * acc[3 * p + 2][...]
    o_ref[...] = (num / den).astype(BF16)


def _attn(zqkv, slopes, b, s):
    n = b * s
    nhp = ATT_HEADS // 2
    col = lambda off: (lambda i, j, sl: (i, off + j))
    return pl.pallas_call(
        functools.partial(_attn_body, s=s),
        grid_spec=pltpu.PrefetchScalarGridSpec(
            num_scalar_prefetch=1,
            grid=(b, nhp),
            in_specs=[pl.BlockSpec((s, LANES), col(0)),
                      pl.BlockSpec((s, LANES), col(nhp)),
                      pl.BlockSpec((s, LANES), col(2 * nhp))],
            out_specs=pl.BlockSpec((s, LANES), lambda i, j, sl: (i, j)),
            scratch_shapes=[pltpu.VMEM((s, LANES), F32) for _ in range(12)],
        ),
        out_shape=jax.ShapeDtypeStruct((n, ATT_WIDTH), BF16),
        compiler_params=_cparams(("arbitrary", "arbitrary")),
        name="attn",
    )(slopes, zqkv, zqkv, zqkv)


def _outproj_body(h_ref, mf_ref, ms_ref, oa_ref, ga_ref, w_ref, gf_ref, wr_ref,
                  h1_ref, m_ref, aff_ref):
    oa = _rms(oa_ref[...].astype(F32), ga_ref[...]).astype(BF16)
    o1, o2 = FN_WIDTH, FN_WIDTH + SSM_WIDTH
    acc = h_ref[...] + _dot(mf_ref[...], w_ref[0:o1, :]) + _dot(ms_ref[...], w_ref[o1:o2, :])
    acc = acc + _dot(oa, w_ref[o2:, :])
    h1_ref[...] = acc
    m = _rms(acc, gf_ref[...]).astype(BF16)
    m_ref[...] = m
    lg = _dot_nt(wr_ref[...], m)
    e = jnp.exp(lg - jnp.max(lg, axis=0, keepdims=True))
    aff_ref[...] = e / jnp.sum(e, axis=0, keepdims=True)


def _outproj(h, mf, ms, oa, ga, w, gf, wr_t):
    n = h.shape[0]
    tm = 512
    row = lambda i: (i, 0)
    fix = lambda i: (0, 0)
    return pl.pallas_call(
        _outproj_body,
        grid=(n // tm,),
        in_specs=[pl.BlockSpec((tm, D_MODEL), row), pl.BlockSpec((tm, FN_WIDTH), row),
                  pl.BlockSpec((tm, SSM_WIDTH), row), pl.BlockSpec((tm, ATT_WIDTH), row),
                  pl.BlockSpec((1, ATT_WIDTH), fix), pl.BlockSpec((D_MODEL, D_MODEL), fix),
                  pl.BlockSpec((1, D_MODEL), fix), pl.BlockSpec((N_EXPERTS, D_MODEL), fix)],
        out_specs=[pl.BlockSpec((tm, D_MODEL), row), pl.BlockSpec((tm, D_MODEL), row),
                   pl.BlockSpec((N_EXPERTS, tm), lambda i: (0, i))],
        out_shape=[jax.ShapeDtypeStruct((n, D_MODEL), F32),
                   jax.ShapeDtypeStruct((n, D_MODEL), BF16),
                   jax.ShapeDtypeStruct((N_EXPERTS, n), F32)],
        compiler_params=_cparams(("arbitrary",)),
        name="outproj",
    )(h, mf, ms, oa, ga, w, gf, wr_t)


def _select_body(aff_ref, rank_ref, off_ref, *, cap, nb):
    r128 = lax.broadcasted_iota(I32, (LANES, LANES), 0)
    c128 = lax.broadcasted_iota(I32, (LANES, LANES), 1)
    upper_incl = jnp.where(r128 <= c128, 1.0, 0.0).astype(BF16)
    ones = jnp.ones((LANES, LANES), BF16)
    rb = lax.broadcasted_iota(I32, (nb, nb), 0)
    cb = lax.broadcasted_iota(I32, (nb, nb), 1)
    lower_strict = jnp.where(cb < rb, 1.0, 0.0).astype(BF16)
    upper_strict = jnp.where(rb < cb, 1.0, 0.0).astype(BF16)
    ones8 = jnp.ones((8, LANES), BF16)

    def count(mask):
        c = jnp.sum(jnp.where(mask, 1.0, 0.0), axis=1, keepdims=True)
        return jnp.sum(c, axis=0, keepdims=True)

    def prefix(mask):
        mb = jnp.where(mask, 1.0, 0.0).astype(BF16)
        incl = _dot(mb, upper_incl)
        tot = _dot(mb, ones)
        offs = _dot(lower_strict, tot.astype(BF16))
        return offs + incl - 1.0, mb

    def per_expert(e, carry):
        bits = pltpu.bitcast(aff_ref[e], I32)

        def bitstep(i, t):
            cand = t | jnp.left_shift(jnp.int32(1), 30 - i)
            return jnp.where(count(bits >= cand) >= cap, cand, t)

        t = lax.fori_loop(0, 31, bitstep, jnp.zeros((1, 1), I32))
        gt = bits > t
        eq = bits == t
        need = cap - count(gt)
        eq_rank, _ = prefix(eq)
        sel = jnp.logical_or(gt, jnp.logical_and(eq, eq_rank < need))
        rank, mb = prefix(sel)
        rank_ref[e] = jnp.where(sel, rank.astype(I32), -1)
        tot_row = _dot_nt(ones8, mb)
        off_row = _dot(tot_row.astype(BF16), upper_strict)
        off_ref[pl.ds(e, 1), :] = off_row[0:1, :].astype(I32)
        return carry

    lax.fori_loop(0, N_EXPERTS, per_expert, 0)


def _select(aff_t, cap):
    n = aff_t.shape[1]
    nb = n // LANES
    a3 = aff_t.reshape(N_EXPERTS, nb, LANES)
    rank, off = pl.pallas_call(
        functools.partial(_select_body, cap=cap, nb=nb),
        grid=(1,),
        in_specs=[pl.BlockSpec((N_EXPERTS, nb, LANES), lambda i: (0, 0, 0))],
        out_specs=[pl.BlockSpec((N_EXPERTS, nb, LANES), lambda i: (0, 0, 0)),
                   pl.BlockSpec((N_EXPERTS, nb), lambda i: (0, 0))],
        out_shape=[jax.ShapeDtypeStruct((N_EXPERTS, nb, LANES), I32),
                   jax.ShapeDtypeStruct((N_EXPERTS, nb), I32)],
        compiler_params=_cparams(("arbitrary",)),
        name="select",
    )(a3)
    return rank.reshape(N_EXPERTS, n), off


def _ffn_body(x_ref, wg_ref, wu_ref, wd_ref, o_ref):
    x = x_ref[0]
    tf = 512
    acc = jnp.zeros(o_ref.shape[1:], F32)
    for j in range(D_FF_EXPERT // tf):
        g = _dot(x, wg_ref[0, :, j * tf:(j + 1) * tf])
        u = _dot(x, wu_ref[0, :, j * tf:(j + 1) * tf])
        hdn = (g * jax.nn.sigmoid(g) * u).astype(BF16)
        acc = acc + _dot(hdn, wd_ref[0, j * tf:(j + 1) * tf, :])
    o_ref[0] = acc.astype(BF16)


def _ffn(xe, wg, wu, wd):
    e, cap, _ = xe.shape
    tm = min(512, cap)
    return pl.pallas_call(
        _ffn_body,
        grid=(e, cap // tm),
        in_specs=[pl.BlockSpec((1, tm, D_MODEL), lambda i, j: (i, j, 0)),
                  pl.BlockSpec((1, D_MODEL, D_FF_EXPERT), lambda i, j: (i, 0, 0)),
                  pl.BlockSpec((1, D_MODEL, D_FF_EXPERT), lambda i, j: (i, 0, 0)),
                  pl.BlockSpec((1, D_FF_EXPERT, D_MODEL), lambda i, j: (i, 0, 0))],
        out_specs=pl.BlockSpec((1, tm, D_MODEL), lambda i, j: (i, j, 0)),
        out_shape=jax.ShapeDtypeStruct((e, cap, D_MODEL), BF16),
        compiler_params=_cparams(("arbitrary", "arbitrary")),
        name="ffn",
    )(xe, wg, wu, wd)


def _combine_body(off_ref, h_ref, rank_ref, aff_ref, p_ref, gp_ref, wg_ref, wp_ref, gfin_ref, ye_ref,
                  o_ref, stack, sems, *, cap, last):
    t = pl.program_id(0)
    nt = pl.num_programs(0)
    tile, win = COMBINE_TILE, COMBINE_WIN
    bpt = tile // LANES
    kdim = N_EXPERTS * win

    def base(tt, e):
        return (off_ref[e, tt * bpt] // 16) * 16

    def copy(e, start, slot):
        return pltpu.make_async_copy(ye_ref.at[e, pl.ds(pl.multiple_of(start, 16), win), :],
                                     stack.at[slot, pl.ds(e * win, win), :], sems.at[slot, e])

    def starts(tt, r):
        want = [base(tt, e) + r * win for e in range(N_EXPERTS)]
        return want, [jnp.minimum(w, cap - win) for w in want]

    def issue(tt, r, slot):
        _, got = starts(tt, r)
        for e in range(N_EXPERTS):
            copy(e, got[e], slot).start()

    def wait(tt, r, slot):
        _, got = starts(tt, r)
        for e in range(N_EXPERTS):
            copy(e, got[e], slot).wait()

    slot = t % 2

    @pl.when(t == 0)
    def _():
        issue(0, 0, 0)

    @pl.when(t + 1 < nt)
    def _():
        issue(t + 1, 0, 1 - slot)

    lane16 = lax.broadcasted_iota(I32, (1, N_EXPERTS), 1)
    er = lax.broadcasted_iota(I32, (N_EXPERTS, kdim), 0)
    ec = lax.broadcasted_iota(I32, (N_EXPERTS, kdim), 1)
    expand = jnp.where(ec // win == er, 1.0, 0.0).astype(BF16)
    lane_in = (lax.broadcasted_iota(I32, (1, kdim), 1) % win).astype(F32)

    aff = aff_ref[...]
    a_hi = aff.astype(BF16)
    a_lo = (aff - a_hi.astype(F32)).astype(BF16)
    gates = _dot(a_hi, expand) + _dot(a_lo, expand)
    rk = rank_ref[...]

    def row_of(vals):
        r = jnp.zeros((1, N_EXPERTS), I32)
        for e in range(N_EXPERTS):
            r = jnp.where(lane16 == e, vals[e], r)
        return r

    def accumulate(r, slot_):
        want, got = starts(t, r)
        wrow, grow = row_of(want), row_of(got)
        ok = jnp.logical_and(rk >= wrow, rk < grow + win)
        rel = jnp.where(ok, rk - grow, -1).astype(F32).astype(BF16)
        hit = _dot(rel, expand) == lane_in
        w = jnp.where(hit, gates, 0.0).astype(BF16)
        o_ref[...] += _dot(w, stack[slot_])

    o_ref[...] = h_ref[...]
    wait(t, 0, slot)
    accumulate(0, slot)

    nr = jnp.int32(1)
    for e in range(N_EXPERTS):
        span = off_ref[e, (t + 1) * bpt] - base(t, e)
        nr = jnp.maximum(nr, (span + win - 1) // win)

    def extra(r, carry):
        issue(t, r, slot)
        wait(t, r, slot)
        accumulate(r, slot)
        return carry

    lax.fori_loop(1, nr, extra, 0)

    h2 = o_ref[...]
    gate = jax.nn.sigmoid(_dot(_rms(h2, gp_ref[...]).astype(BF16), wg_ref[...]))
    h3 = h2 + _dot(p_ref[...].astype(BF16), wp_ref[...]) * gate
    o_ref[...] = _rms(h3, gfin_ref[...]) if last else h3


def _combine(off, h1, rank_tok, aff_tok, p, gp, wg, wp, gfin, ye, cap, last):
    n = h1.shape[0]
    tile = COMBINE_TILE
    row = lambda i, o: (i, 0)
    fix = lambda i, o: (0, 0)
    return pl.pallas_call(
        functools.partial(_combine_body, cap=cap, last=last),
        grid_spec=pltpu.PrefetchScalarGridSpec(
            num_scalar_prefetch=1,
            grid=(n // tile,),
            in_specs=[pl.BlockSpec((tile, D_MODEL), row),
                      pl.BlockSpec((tile, N_EXPERTS), row),
                      pl.BlockSpec((tile, N_EXPERTS), row),
                      pl.BlockSpec((tile, PLE_DIM), row),
                      pl.BlockSpec((1, D_MODEL), fix),
                      pl.BlockSpec((D_MODEL, D_MODEL), fix),
                      pl.BlockSpec((PLE_DIM, D_MODEL), fix),
                      pl.BlockSpec((1, D_MODEL), fix),
                      pl.BlockSpec(memory_space=pl.ANY)],
            out_specs=pl.BlockSpec((tile, D_MODEL), row),
            scratch_shapes=[pltpu.VMEM((2, N_EXPERTS * COMBINE_WIN, D_MODEL), BF16),
                            pltpu.SemaphoreType.DMA((2, N_EXPERTS))],
        ),
        out_shape=jax.ShapeDtypeStruct((n, D_MODEL), F32),
        compiler_params=_cparams(("arbitrary",)),
        name="combine",
    )(off, h1, rank_tok, aff_tok, p, gp, wg, wp, gfin, ye)


def _gather_rows(m, rank, cap):
    n = m.shape[0]
    tok = jnp.broadcast_to(jnp.arange(n, dtype=I32)[None], rank.shape)
    eidx = jnp.broadcast_to(jnp.arange(N_EXPERTS, dtype=I32)[:, None], rank.shape)
    pos = jnp.where(rank < 0, cap, rank)
    idx = jnp.zeros((N_EXPERTS, cap), I32).at[eidx, pos].set(tok, mode='drop')
    return jnp.take(m, idx.reshape(-1), axis=0).reshape(N_EXPERTS, cap, D_MODEL)


def _trunk(x, p, prm, b, s):
    n = b * s
    nc = s // SSM_CHUNK
    cap = EC_CAPACITY_FACTOR * n // N_EXPERTS
    cs, ss = _dft_tables(s)
    cs, ss = cs.astype(BF16), ss.astype(BF16)
    h = x.reshape(n, D_MODEL)
    depth = prm['w_in'].shape[0]
    for l in range(depth):
        lp = prm['layers'][l]
        zf, zs, zqkv = _inproj(h, lp['g_mix'], lp['w_in'])
        mf = _fourier(zf, cs, ss, prm['ccb'], prm['scb'], lp['w_fnet'], lp['g_f'], b, s)
        u = zs.reshape(n // SSM_CHUNK, SSM_ROW)
        xl = _s5_in(u, lp['s5_w_in'])
        pr, pi = _scan_powers(lp['s5_a_chunk'], int(math.log2(nc)))
        st = _s5_scan(xl, pr, pi, b, nc)
        y = _s5_out(u, st, lp['s5_m_intra'], lp['s5_w_out']).reshape(n, SSM_WIDTH)
        ms = _s5_post(y, zs, lp['s5_d'], lp['s5_glu_w'], lp['s5_glu_b'], lp['g_s'])
        oa = _attn(zqkv, prm['slopes'], b, s)
        h1, m, aff_t = _outproj(h, mf, ms, oa, lp['g_a'], lp['w_out'], lp['g_ffn'], lp['w_router_t'])
        rank, off = _select(aff_t, cap)
        off = jnp.concatenate([off, jnp.full((N_EXPERTS, 1), cap, I32)], axis=1)
        xe = _gather_rows(m, rank, cap)
        ye = _ffn(xe, lp['w_gate'], lp['w_up'], lp['w_down'])
        h = _combine(off, h1, rank.T, aff_t.T, p[l].reshape(n, PLE_DIM), lp['g_ple'], lp['w_ple_gate'],
                     lp['w_ple_proj'], prm['g_final'], ye, cap, l == depth - 1)
    return h.reshape(b, s, D_MODEL)


def kernel(x_prompt, x_sample, p_prompt, p_sample, norm_mix, w_in, w_fnet, ssm_a_re, ssm_a_im, ssm_log_dt, ssm_b_re, ssm_b_im, ssm_c_re, ssm_c_im, ssm_d, ssm_glu_w, ssm_glu_b, norm_branch, w_out, norm_ffn, w_router, w_exp_gate, w_exp_up, w_exp_down, norm_ple, w_ple_gate, w_ple_proj, norm_final):
    depth = w_in.shape[0]
    o1, o2 = FN_WIDTH, FN_WIDTH + SSM_WIDTH
    row = lambda v: v.reshape(1, -1).astype(F32)
    cc, sc = _dft_tables(FN_HEAD_DIM)
    tile4 = lambda mtx: _block_diag(jnp.broadcast_to(mtx[None], (FN_HEADS,) + mtx.shape)).astype(BF16)
    layers = []
    for l in range(depth):
        m_intra, s5_w_in, s5_w_out, a_chunk = _s5_matrices(
            ssm_a_re[l], ssm_a_im[l], ssm_log_dt[l], ssm_b_re[l], ssm_b_im[l], ssm_c_re[l], ssm_c_im[l])
        layers.append(dict(
            g_mix=row(norm_mix[l]), w_in=w_in[l].astype(BF16),
            w_fnet=_block_diag(w_fnet[l]).astype(BF16),
            g_f=row(norm_branch[l][:o1]), g_s=row(norm_branch[l][o1:o2]), g_a=row(norm_branch[l][o2:]),
            s5_m_intra=m_intra, s5_w_in=s5_w_in, s5_w_out=s5_w_out, s5_a_chunk=a_chunk,
            s5_d=row(ssm_d[l]), s5_glu_w=ssm_glu_w[l].astype(BF16), s5_glu_b=row(ssm_glu_b[l]),
            w_out=w_out[l].astype(BF16), g_ffn=row(norm_ffn[l]),
            w_router_t=w_router[l].T.astype(BF16),
            w_gate=w_exp_gate[l].astype(BF16), w_up=w_exp_up[l].astype(BF16),
            w_down=w_exp_down[l].astype(BF16),
            g_ple=row(norm_ple[l]), w_ple_gate=w_ple_gate[l].astype(BF16),
            w_ple_proj=w_ple_proj[l].astype(BF16)))
    slopes = jnp.asarray([2.0 ** (-8.0 * (i + 1) / ATT_HEADS) for i in range(ATT_HEADS)], F32)
    prm = dict(w_in=w_in, layers=layers, ccb=tile4(cc), scb=tile4(sc), slopes=slopes,
               g_final=row(norm_final))
    bp, sp = x_prompt.shape[0], x_prompt.shape[1]
    bs, ssq = x_sample.shape[0], x_sample.shape[1]
    y_prompt = _trunk(x_prompt, p_prompt, prm, bp, sp)
    y_sample = _trunk(x_sample, p_sample, prm, bs, ssq)
    return (y_prompt, y_sample)
```

```python
import dataclasses
import functools
import math

import jax
import jax.numpy as jnp
from jax import lax
from jax.experimental import pallas as pl
from jax.experimental.pallas import tpu as pltpu
from jax.experimental.pallas import tpu_sc as plsc

D_MODEL = 1024
FN_WIDTH = 256
FN_HEADS = 4
FN_HEAD_DIM = 64
SSM_WIDTH = 256
SSM_GROUP = 16
SSM_GROUPS = 16
SSM_STATE = 64
ATT_WIDTH = 512
ATT_HEAD_DIM = 64
ATT_HEADS = 8
DILATED_PATTERNS = ((128, 1), (512, 4), (2048, 16))
IN_PROJ_WIDTH = 2048
N_EXPERTS = 16
EC_CAPACITY_FACTOR = 2
D_FF_EXPERT = 2048
PLE_DIM = 256
RMS_EPS = 1e-6
NEG_INF = -1e30

LANES = 128
SSM_CHUNK = 8
SSM_ROW = SSM_CHUNK * SSM_WIDTH
SSM_NSTATE = SSM_GROUPS * SSM_STATE
ATT_HALF = 64
COMBINE_TILE = 256
COMBINE_WIN = 64
VMEM_LIMIT = 56 * 1024 * 1024
SC_CORES = 2
SC_SUBCORES = 16
SC_LANES = 16
GATHER_ROWS = 64
RANK_CHUNK = 2048
PACKED = D_MODEL // 2

F32 = jnp.float32
BF16 = jnp.bfloat16
I32 = jnp.int32


def _cparams(sem):
    return pltpu.CompilerParams(dimension_semantics=sem, vmem_limit_bytes=VMEM_LIMIT)


def _rms(x, g):
    return x * lax.rsqrt(jnp.mean(x * x, axis=-1, keepdims=True) + RMS_EPS) * g


def _dot(a, b):
    return jnp.dot(a, b, preferred_element_type=F32)


def _dot_nt(a, b):
    return lax.dot_general(a, b, (((1,), (1,)), ((), ())), preferred_element_type=F32)


def _inproj_body(h_ref, g_ref, w_ref, zf_ref, zs_ref, zqkv_ref):
    a = _rms(h_ref[...], g_ref[...]).astype(BF16)
    z = _dot(a, w_ref[...])
    zf_ref[...] = z[:, :FN_WIDTH]
    zs_ref[...] = z[:, FN_WIDTH:FN_WIDTH + SSM_WIDTH]
    zqkv_ref[...] = z[:, FN_WIDTH + SSM_WIDTH:].astype(BF16)


def _inproj(h, g, w):
    n = h.shape[0]
    tm = 512
    return pl.pallas_call(
        _inproj_body,
        grid=(n // tm,),
        in_specs=[pl.BlockSpec((tm, D_MODEL), lambda i: (i, 0)),
                  pl.BlockSpec((1, D_MODEL), lambda i: (0, 0)),
                  pl.BlockSpec((D_MODEL, IN_PROJ_WIDTH), lambda i: (0, 0))],
        out_specs=[pl.BlockSpec((tm, FN_WIDTH), lambda i: (i, 0)),
                   pl.BlockSpec((tm, SSM_WIDTH), lambda i: (i, 0)),
                   pl.BlockSpec((tm, 3 * ATT_WIDTH), lambda i: (i, 0))],
        out_shape=[jax.ShapeDtypeStruct((n, FN_WIDTH), F32),
                   jax.ShapeDtypeStruct((n, SSM_WIDTH), F32),
                   jax.ShapeDtypeStruct((n, 3 * ATT_WIDTH), BF16)],
        compiler_params=_cparams(("arbitrary",)),
        name="inproj",
    )(h, g, w)


def _fourier_body(x_ref, cs_ref, ss_ref, cc_ref, sc_ref, wb_ref, g_ref, o_ref):
    x = x_ref[0].astype(BF16)
    y = _dot(cs_ref[...], x).astype(BF16)
    z = _dot(ss_ref[...], x).astype(BF16)
    f = _dot(y, cc_ref[...]) - _dot(z, sc_ref[...])
    o = _dot(f.astype(BF16), wb_ref[...])
    o_ref[0] = _rms(o, g_ref[...]).astype(BF16)


def _dft_tables(n):
    k = (jnp.arange(n, dtype=I32)[:, None] * jnp.arange(n, dtype=I32)[None, :]) % n
    ang = k.astype(F32) * (2.0 * math.pi / n)
    return jnp.cos(ang), jnp.sin(ang)


def _fourier(zf, cs, ss, ccb, scb, wb, g, b, s):
    tr = 512
    x = zf.reshape(b, s, FN_WIDTH)
    out = pl.pallas_call(
        _fourier_body,
        grid=(s // tr, b),
        in_specs=[pl.BlockSpec((1, s, FN_WIDTH), lambda i, j: (j, 0, 0)),
                  pl.BlockSpec((tr, s), lambda i, j: (i, 0)),
                  pl.BlockSpec((tr, s), lambda i, j: (i, 0)),
                  pl.BlockSpec((FN_WIDTH, FN_WIDTH), lambda i, j: (0, 0)),
                  pl.BlockSpec((FN_WIDTH, FN_WIDTH), lambda i, j: (0, 0)),
                  pl.BlockSpec((FN_WIDTH, FN_WIDTH), lambda i, j: (0, 0)),
                  pl.BlockSpec((1, FN_WIDTH), lambda i, j: (0, 0))],
        out_specs=pl.BlockSpec((1, tr, FN_WIDTH), lambda i, j: (j, i, 0)),
        out_shape=jax.ShapeDtypeStruct((b, s, FN_WIDTH), BF16),
        compiler_params=_cparams(("arbitrary", "arbitrary")),
        name="fourier",
    )(x, cs, ss, ccb, scb, wb, g)
    return out.reshape(b * s, FN_WIDTH)


def _block_diag(blocks):
    h, a, bb = blocks.shape
    eye = jnp.eye(h, dtype=blocks.dtype)
    return jnp.einsum('hab,hg->hagb', blocks, eye).reshape(h * a, h * bb)


def _s5_matrices(a_re, a_im, log_dt, b_re, b_im, c_re, c_im):
    t = SSM_CHUNK
    g, p, c = SSM_GROUPS, SSM_STATE, SSM_GROUP
    lam = lax.complex(a_re.astype(F32), a_im.astype(F32))
    dt = jnp.exp(log_dt.astype(F32))[..., None]
    abar = jnp.exp(lam * dt)
    bbar = ((abar - 1.0) / lam)[..., None] * lax.complex(b_re.astype(F32), b_im.astype(F32))
    cmat = lax.complex(c_re.astype(F32), c_im.astype(F32))
    ks = jnp.arange(t + 1, dtype=F32)
    apow = jnp.exp((lam * dt)[:, None] * ks[None, :, None, None])
    eye = jnp.eye(g, dtype=F32)

    kern = jnp.real(jnp.einsum('dgcp,dkgp,dgpe->dkgce', cmat, apow[:, :t], bbar))
    lag = jnp.arange(t)[None, :] - jnp.arange(t)[:, None]
    kf = jnp.where((lag >= 0)[:, :, None, None, None], kern[0][jnp.clip(lag, 0, t - 1)], 0.0)
    kb = jnp.where((lag <= 0)[:, :, None, None, None], kern[1][jnp.clip(-lag, 0, t - 1)], 0.0)
    ktot = kf + kb
    m_intra = jnp.einsum('abgce,gh->agebhc', ktot, eye).reshape(SSM_ROW, SSM_ROW)

    wf = apow[0, t - 1 - jnp.arange(t)][:, :, :, None] * bbar[0][None]
    wb = apow[1, jnp.arange(t)][:, :, :, None] * bbar[1][None]
    def _w_in(w):
        w6 = jnp.einsum('rgpe,gh->rgehp', w, eye.astype(w.dtype))
        return w6.reshape(SSM_ROW, SSM_NSTATE)
    wf, wb = _w_in(wf), _w_in(wb)
    w_in = jnp.concatenate([jnp.real(wf), jnp.imag(wf), jnp.real(wb), jnp.imag(wb)], axis=1)

    qf = cmat[0][None] * apow[0, 1 + jnp.arange(t)][:, :, None, :]
    qb = cmat[1][None] * apow[1, t - jnp.arange(t)][:, :, None, :]
    def _w_out(q):
        q6 = jnp.einsum('rgcp,gh->gprhc', q, eye.astype(q.dtype))
        return q6.reshape(SSM_NSTATE, SSM_ROW)
    qf, qb = _w_out(qf), _w_out(qb)
    w_out = jnp.concatenate([jnp.real(qf), -jnp.imag(qf), jnp.real(qb), -jnp.imag(qb)], axis=0)

    a_chunk = lam * dt * t
    return m_intra.astype(BF16), w_in.astype(BF16), w_out.astype(BF16), a_chunk


def _scan_powers(a_chunk, nsteps):
    e = jnp.exp(a_chunk[None] * (2.0 ** jnp.arange(nsteps, dtype=F32))[:, None, None, None])
    e = e.reshape(nsteps, 2 * SSM_NSTATE)
    return jnp.real(e), jnp.imag(e)


def _s5_in_body(u_ref, w_ref, o_ref):
    o_ref[...] = _dot(u_ref[...].astype(BF16), w_ref[...])


def _s5_in(u, w_in):
    rows = u.shape[0]
    tr, tc = min(512, rows), 1024
    width = 4 * SSM_NSTATE
    return pl.pallas_call(
        _s5_in_body,
        grid=(width // tc, rows // tr),
        in_specs=[pl.BlockSpec((tr, SSM_ROW), lambda j, i: (i, 0)),
                  pl.BlockSpec((SSM_ROW, tc), lambda j, i: (0, j))],
        out_specs=pl.BlockSpec((tr, tc), lambda j, i: (i, j)),
        out_shape=jax.ShapeDtypeStruct((rows, width), F32),
        compiler_params=_cparams(("arbitrary", "arbitrary")),
        name="s5_in",
    )(u, w_in)


def _s5_scan_body(x_ref, pr_ref, pi_ref, o_ref, *, nc, nsteps):
    ns = SSM_NSTATE
    x = x_ref[0]
    fre, fim = x[:, 0:ns], x[:, ns:2 * ns]
    bre, bim = x[:, 2 * ns:3 * ns], x[:, 3 * ns:4 * ns]
    row = lax.broadcasted_iota(I32, (nc, 1), 0)
    for k in range(nsteps):
        sh = 2 ** k
        far, fai = pr_ref[k:k + 1, 0:ns], pi_ref[k:k + 1, 0:ns]
        bar, bai = pr_ref[k:k + 1, ns:2 * ns], pi_ref[k:k + 1, ns:2 * ns]
        fmask = row >= sh
        sre = jnp.where(fmask, pltpu.roll(fre, sh, 0), 0.0)
        sim = jnp.where(fmask, pltpu.roll(fim, sh, 0), 0.0)
        fre, fim = fre + far * sre - fai * sim, fim + far * sim + fai * sre
        bmask = row < nc - sh
        sre = jnp.where(bmask, pltpu.roll(bre, nc - sh, 0), 0.0)
        sim = jnp.where(bmask, pltpu.roll(bim, nc - sh, 0), 0.0)
        bre, bim = bre + bar * sre - bai * sim, bim + bar * sim + bai * sre
    fmask = row >= 1
    bmask = row < nc - 1
    o_ref[0, :, 0:ns] = jnp.where(fmask, pltpu.roll(fre, 1, 0), 0.0).astype(BF16)
    o_ref[0, :, ns:2 * ns] = jnp.where(fmask, pltpu.roll(fim, 1, 0), 0.0).astype(BF16)
    o_ref[0, :, 2 * ns:3 * ns] = jnp.where(bmask, pltpu.roll(bre, nc - 1, 0), 0.0).astype(BF16)
    o_ref[0, :, 3 * ns:4 * ns] = jnp.where(bmask, pltpu.roll(bim, nc - 1, 0), 0.0).astype(BF16)


def _s5_scan(xl, pr, pi, b, nc):
    nsteps = int(math.log2(nc))
    width = 4 * SSM_NSTATE
    x = xl.reshape(b, nc, width)
    out = pl.pallas_call(
        functools.partial(_s5_scan_body, nc=nc, nsteps=nsteps),
        grid=(b,),
        in_specs=[pl.BlockSpec((1, nc, width), lambda i: (i, 0, 0)),
                  pl.BlockSpec((nsteps, 2 * SSM_NSTATE), lambda i: (0, 0)),
                  pl.BlockSpec((nsteps, 2 * SSM_NSTATE), lambda i: (0, 0))],
        out_specs=pl.BlockSpec((1, nc, width), lambda i: (i, 0, 0)),
        out_shape=jax.ShapeDtypeStruct((b, nc, width), BF16),
        compiler_params=_cparams(("arbitrary",)),
        name="s5_scan",
    )(x, pr, pi)
    return out.reshape(b * nc, width)


def _s5_out_body(u_ref, s_ref, m_ref, w_ref, o_ref):
    o_ref[...] = _dot(u_ref[...].astype(BF16), m_ref[...]) + _dot(s_ref[...], w_ref[...])


def _s5_out(u, states, m_intra, w_out):
    rows = u.shape[0]
    tr, tc = min(512, rows), 512
    width = 4 * SSM_NSTATE
    return pl.pallas_call(
        _s5_out_body,
        grid=(SSM_ROW // tc, rows // tr),
        in_specs=[pl.BlockSpec((tr, SSM_ROW), lambda j, i: (i, 0)),
                  pl.BlockSpec((tr, width), lambda j, i: (i, 0)),
                  pl.BlockSpec((SSM_ROW, tc), lambda j, i: (0, j)),
                  pl.BlockSpec((width, tc), lambda j, i: (0, j))],
        out_specs=pl.BlockSpec((tr, tc), lambda j, i: (i, j)),
        out_shape=jax.ShapeDtypeStruct((rows, SSM_ROW), F32),
        compiler_params=_cparams(("arbitrary", "arbitrary")),
        name="s5_out",
    )(u, states, m_intra, w_out)


def _s5_post_body(y_ref, u_ref, d_ref, w_ref, b_ref, g_ref, o_ref):
    v = y_ref[...] + d_ref[...] * u_ref[...]
    c0 = math.sqrt(2.0 / math.pi)
    gl = 0.5 * v * (1.0 + jnp.tanh(c0 * (v + 0.044715 * (v * v * v))))
    gate = jax.nn.sigmoid(_dot(gl.astype(BF16), w_ref[...]) + b_ref[...])
    o_ref[...] = _rms(gl * gate, g_ref[...]).astype(BF16)


def _s5_post(y, u, d, w, bias, g):
    n = y.shape[0]
    tm = 2048
    row = lambda i: (i, 0)
    fix = lambda i: (0, 0)
    return pl.pallas_call(
        _s5_post_body,
        grid=(n // tm,),
        in_specs=[pl.BlockSpec((tm, SSM_WIDTH), row), pl.BlockSpec((tm, SSM_WIDTH), row),
                  pl.BlockSpec((1, SSM_WIDTH), fix), pl.BlockSpec((SSM_WIDTH, SSM_WIDTH), fix),
                  pl.BlockSpec((1, SSM_WIDTH), fix), pl.BlockSpec((1, SSM_WIDTH), fix)],
        out_specs=pl.BlockSpec((tm, SSM_WIDTH), row),
        out_shape=jax.ShapeDtypeStruct((n, SSM_WIDTH), BF16),
        compiler_params=_cparams(("arbitrary",)),
        name="s5_post",
    )(y, u, d, w, bias, g)


def _attn_body(slope_ref, q_ref, k_ref, v_ref, o_ref, qf, kf, vf, *acc, s):
    hp = pl.program_id(1)
    qf[...] = q_ref[...].astype(F32)
    kf[...] = k_ref[...].astype(F32)
    vf[...] = v_ref[...].astype(F32)
    lane = lax.broadcasted_iota(I32, (1, LANES), 1)
    first = lane < ATT_HEAD_DIM
    slopes = (slope_ref[2 * hp], slope_ref[2 * hp + 1])
    scale = ATT_HEAD_DIM ** -0.5

    for p, (_, d) in enumerate(DILATED_PATTERNS):
        acc_o, acc_m, acc_l = acc[3 * p], acc[3 * p + 1], acc[3 * p + 2]
        ln = s // d
        bq = min(128, ln)
        bk = min(bq + 2 * ATT_HALF, ln)
        nqb = ln // bq

        def rows(start, size, d=d):
            return pl.ds(start, size) if d == 1 else pl.ds(start, size, stride=d)

        def block(i, carry, d=d, ln=ln, bq=bq, bk=bk, nqb=nqb, rows=rows,
                  acc_o=acc_o, acc_m=acc_m, acc_l=acc_l):
            c = i // nqb
            j0 = (i % nqb) * bq
            ks = jnp.clip(j0 - ATT_HALF, 0, ln - bk)
            q = qf[rows(c + d * j0, bq), :]
            k = kf[rows(c + d * ks, bk), :].astype(BF16)
            v = vf[rows(c + d * ks, bk), :].astype(BF16)
            jq = j0 + lax.broadcasted_iota(I32, (bq, 1), 0)
            jk = ks + lax.broadcasted_iota(I32, (1, bk), 1)
            rel = jnp.abs(jq - jk)
            valid = rel <= ATT_HALF
            dist = (d * rel).astype(F32)
            res = []
            for hh in range(2):
                qm = jnp.where(first if hh == 0 else jnp.logical_not(first), q, 0.0).astype(BF16)
                sc = _dot_nt(qm, k) * scale - slopes[hh] * dist
                sc = jnp.where(valid, sc, NEG_INF)
                m = jnp.max(sc, axis=-1, keepdims=True)
                pe = jnp.exp(sc - m)
                l = jnp.sum(pe, axis=-1, keepdims=True)
                res.append((_dot(pe.astype(BF16), v), m, l))
            dst = rows(c + d * j0, bq)
            acc_o[dst, :] = jnp.where(first, res[0][0], res[1][0])
            acc_m[dst, :] = jnp.where(first, res[0][1], res[1][1])
            acc_l[dst, :] = jnp.where(first, res[0][2], res[1][2])
            return carry

        lax.fori_loop(0, d * nqb, block, 0)

    m = jnp.maximum(jnp.maximum(acc[1][...], acc[4][...]), acc[7][...])
    num = jnp.zeros((s, LANES), F32)
    den = jnp.zeros((s, LANES), F32)
    for p in range(3):
        w = jnp.exp(acc[3 * p + 1][...] - m)
        num = num + w * acc[3 * p][...]
        den = den + w * acc[3 * p + 2][...]
    o_ref[...] = (num / den).astype(BF16)


def _attn(zqkv, slopes, b, s):
    n = b * s
    nhp = ATT_HEADS // 2
    col = lambda off: (lambda i, j, sl: (i, off + j))
    return pl.pallas_call(
        functools.partial(_attn_body, s=s),
        grid_spec=pltpu.PrefetchScalarGridSpec(
            num_scalar_prefetch=1,
            grid=(b, nhp),
            in_specs=[pl.BlockSpec((s, LANES), col(0)),
                      pl.BlockSpec((s, LANES), col(nhp)),
                      pl.BlockSpec((s, LANES), col(2 * nhp))],
            out_specs=pl.BlockSpec((s, LANES), lambda i, j, sl: (i, j)),
            scratch_shapes=[pltpu.VMEM((s, LANES), F32) for _ in range(12)],
        ),
        out_shape=jax.ShapeDtypeStruct((n, ATT_WIDTH), BF16),
        compiler_params=_cparams(("arbitrary", "arbitrary")),
        name="attn",
    )(slopes, zqkv, zqkv, zqkv)


def _outproj_body(h_ref, mf_ref, ms_ref, oa_ref, ga_ref, w_ref, gf_ref, wr_ref,
                  h1_ref, m_ref, aff_ref):
    oa = _rms(oa_ref[...].astype(F32), ga_ref[...]).astype(BF16)
    o1, o2 = FN_WIDTH, FN_WIDTH + SSM_WIDTH
    acc = h_ref[...] + _dot(mf_ref[...], w_ref[0:o1, :]) + _dot(ms_ref[...], w_ref[o1:o2, :])
    acc = acc + _dot(oa, w_ref[o2:, :])
    h1_ref[...] = acc
    m = _rms(acc, gf_ref[...]).astype(BF16)
    bits = pltpu.bitcast(m.astype(F32), I32)
    m_ref[...] = lax.shift_right_logical(bits[:, :PACKED], 16) | (bits[:, PACKED:] & jnp.int32(-65536))
    lg = _dot_nt(wr_ref[...], m)
    e = jnp.exp(lg - jnp.max(lg, axis=0, keepdims=True))
    aff_ref[...] = e / jnp.sum(e, axis=0, keepdims=True)


def _outproj(h, mf, ms, oa, ga, w, gf, wr_t):
    n = h.shape[0]
    tm = 512
    row = lambda i: (i, 0)
    fix = lambda i: (0, 0)
    return pl.pallas_call(
        _outproj_body,
        grid=(n // tm,),
        in_specs=[pl.BlockSpec((tm, D_MODEL), row), pl.BlockSpec((tm, FN_WIDTH), row),
                  pl.BlockSpec((tm, SSM_WIDTH), row), pl.BlockSpec((tm, ATT_WIDTH), row),
                  pl.BlockSpec((1, ATT_WIDTH), fix), pl.BlockSpec((D_MODEL, D_MODEL), fix),
                  pl.BlockSpec((1, D_MODEL), fix), pl.BlockSpec((N_EXPERTS, D_MODEL), fix)],
        out_specs=[pl.BlockSpec((tm, D_MODEL), row), pl.BlockSpec((tm, PACKED), row),
                   pl.BlockSpec((N_EXPERTS, tm), lambda i: (0, i))],
        out_shape=[jax.ShapeDtypeStruct((n, D_MODEL), F32),
                   jax.ShapeDtypeStruct((n, PACKED), I32),
                   jax.ShapeDtypeStruct((N_EXPERTS, n), F32)],
        compiler_params=_cparams(("arbitrary",)),
        name="outproj",
    )(h, mf, ms, oa, ga, w, gf, wr_t)


def _select_body(aff_ref, rank_ref, off_ref, *, cap, nb):
    r128 = lax.broadcasted_iota(I32, (LANES, LANES), 0)
    c128 = lax.broadcasted_iota(I32, (LANES, LANES), 1)
    upper_incl = jnp.where(r128 <= c128, 1.0, 0.0).astype(BF16)
    ones = jnp.ones((LANES, LANES), BF16)
    rb = lax.broadcasted_iota(I32, (nb, nb), 0)
    cb = lax.broadcasted_iota(I32, (nb, nb), 1)
    lower_strict = jnp.where(cb < rb, 1.0, 0.0).astype(BF16)
    upper_strict = jnp.where(rb < cb, 1.0, 0.0).astype(BF16)
    ones8 = jnp.ones((8, LANES), BF16)

    def count(mask):
        c = jnp.sum(jnp.where(mask, 1.0, 0.0), axis=1, keepdims=True)
        return jnp.sum(c, axis=0, keepdims=True)

    def prefix(mask):
        mb = jnp.where(mask, 1.0, 0.0).astype(BF16)
        incl = _dot(mb, upper_incl)
        tot = _dot(mb, ones)
        offs = _dot(lower_strict, tot.astype(BF16))
        return offs + incl - 1.0, mb

    def per_expert(e, carry):
        bits = pltpu.bitcast(aff_ref[e], I32)

        def bitstep(i, t):
            cand = t | jnp.left_shift(jnp.int32(1), 30 - i)
            return jnp.where(count(bits >= cand) >= cap, cand, t)

        t = lax.fori_loop(0, 31, bitstep, jnp.zeros((1, 1), I32))
        gt = bits > t
        eq = bits == t
        need = cap - count(gt)
        eq_rank, _ = prefix(eq)
        sel = jnp.logical_or(gt, jnp.logical_and(eq, eq_rank < need))
        rank, mb = prefix(sel)
        rank_ref[e] = jnp.where(sel, rank.astype(I32), -1)
        tot_row = _dot_nt(ones8, mb)
        off_row = _dot(tot_row.astype(BF16), upper_strict)
        off_ref[pl.ds(e, 1), :] = off_row[0:1, :].astype(I32)
        return carry

    lax.fori_loop(0, N_EXPERTS, per_expert, 0)


def _select(aff_t, cap):
    n = aff_t.shape[1]
    nb = n // LANES
    a3 = aff_t.reshape(N_EXPERTS, nb, LANES)
    rank, off = pl.pallas_call(
        functools.partial(_select_body, cap=cap, nb=nb),
        grid=(1,),
        in_specs=[pl.BlockSpec((N_EXPERTS, nb, LANES), lambda i: (0, 0, 0))],
        out_specs=[pl.BlockSpec((N_EXPERTS, nb, LANES), lambda i: (0, 0, 0)),
                   pl.BlockSpec((N_EXPERTS, nb), lambda i: (0, 0))],
        out_shape=[jax.ShapeDtypeStruct((N_EXPERTS, nb, LANES), I32),
                   jax.ShapeDtypeStruct((N_EXPERTS, nb), I32)],
        compiler_params=_cparams(("arbitrary",)),
        name="select",
    )(a3)
    return rank.reshape(N_EXPERTS, n), off


def _ffn_body(x_ref, wg_ref, wu_ref, wd_ref, o_ref):
    w = x_ref[0]
    x = jnp.concatenate([pltpu.bitcast(lax.shift_left(w, 16), F32),
                         pltpu.bitcast(w & jnp.int32(-65536), F32)], axis=1).astype(BF16)
    tf = 512
    acc = jnp.zeros(o_ref.shape[1:], F32)
    for j in range(D_FF_EXPERT // tf):
        g = _dot(x, wg_ref[0, :, j * tf:(j + 1) * tf])
        u = _dot(x, wu_ref[0, :, j * tf:(j + 1) * tf])
        hdn = (g * jax.nn.sigmoid(g) * u).astype(BF16)
        acc = acc + _dot(hdn, wd_ref[0, j * tf:(j + 1) * tf, :])
    o_ref[0] = acc.astype(BF16)


def _ffn(xe, wg, wu, wd):
    e, cap, _ = xe.shape
    tm = min(512, cap)
    return pl.pallas_call(
        _ffn_body,
        grid=(e, cap // tm),
        in_specs=[pl.BlockSpec((1, tm, PACKED), lambda i, j: (i, j, 0)),
                  pl.BlockSpec((1, D_MODEL, D_FF_EXPERT), lambda i, j: (i, 0, 0)),
                  pl.BlockSpec((1, D_MODEL, D_FF_EXPERT), lambda i, j: (i, 0, 0)),
                  pl.BlockSpec((1, D_FF_EXPERT, D_MODEL), lambda i, j: (i, 0, 0))],
        out_specs=pl.BlockSpec((1, tm, D_MODEL), lambda i, j: (i, j, 0)),
        out_shape=jax.ShapeDtypeStruct((e, cap, D_MODEL), BF16),
        compiler_params=_cparams(("arbitrary", "arbitrary")),
        name="ffn",
    )(xe, wg, wu, wd)


def _combine_body(off_ref, h_ref, rank_ref, aff_ref, p_ref, gp_ref, wg_ref, wp_ref, gfin_ref, ye_ref,
                  o_ref, stack, sems, *, cap, last):
    t = pl.program_id(0)
    nt = pl.num_programs(0)
    tile, win = COMBINE_TILE, COMBINE_WIN
    bpt = tile // LANES
    kdim = N_EXPERTS * win

    def base(tt, e):
        return (off_ref[e, tt * bpt] // 16) * 16

    def copy(e, start, slot):
        return pltpu.make_async_copy(ye_ref.at[e, pl.ds(pl.multiple_of(start, 16), win), :],
                                     stack.at[slot, pl.ds(e * win, win), :], sems.at[slot, e])

    def starts(tt, r):
        want = [base(tt, e) + r * win for e in range(N_EXPERTS)]
        return want, [jnp.minimum(w, cap - win) for w in want]

    def issue(tt, r, slot):
        _, got = starts(tt, r)
        for e in range(N_EXPERTS):
            copy(e, got[e], slot).start()

    def wait(tt, r, slot):
        _, got = starts(tt, r)
        for e in range(N_EXPERTS):
            copy(e, got[e], slot).wait()

    slot = t % 2

    @pl.when(t == 0)
    def _():
        issue(0, 0, 0)

    @pl.when(t + 1 < nt)
    def _():
        issue(t + 1, 0, 1 - slot)

    lane16 = lax.broadcasted_iota(I32, (1, N_EXPERTS), 1)
    er = lax.broadcasted_iota(I32, (N_EXPERTS, kdim), 0)
    ec = lax.broadcasted_iota(I32, (N_EXPERTS, kdim), 1)
    expand = jnp.where(ec // win == er, 1.0, 0.0).astype(BF16)
    lane_in = (lax.broadcasted_iota(I32, (1, kdim), 1) % win).astype(F32)

    aff = aff_ref[...]
    a_hi = aff.astype(BF16)
    a_lo = (aff - a_hi.astype(F32)).astype(BF16)
    gates = _dot(a_hi, expand) + _dot(a_lo, expand)
    rk = rank_ref[...]

    def row_of(vals):
        r = jnp.zeros((1, N_EXPERTS), I32)
        for e in range(N_EXPERTS):
            r = jnp.where(lane16 == e, vals[e], r)
        return r

    def accumulate(r, slot_):
        want, got = starts(t, r)
        wrow, grow = row_of(want), row_of(got)
        ok = jnp.logical_and(rk >= wrow, rk < grow + win)
        rel = jnp.where(ok, rk - grow, -1).astype(F32).astype(BF16)
        hit = _dot(rel, expand) == lane_in
        w = jnp.where(hit, gates, 0.0).astype(BF16)
        o_ref[...] += _dot(w, stack[slot_])

    o_ref[...] = h_ref[...]
    wait(t, 0, slot)
    accumulate(0, slot)

    nr = jnp.int32(1)
    for e in range(N_EXPERTS):
        span = off_ref[e, (t + 1) * bpt] - base(t, e)
        nr = jnp.maximum(nr, (span + win - 1) // win)

    def extra(r, carry):
        issue(t, r, slot)
        wait(t, r, slot)
        accumulate(r, slot)
        return carry

    lax.fori_loop(1, nr, extra, 0)

    h2 = o_ref[...]
    gate = jax.nn.sigmoid(_dot(_rms(h2, gp_ref[...]).astype(BF16), wg_ref[...]))
    h3 = h2 + _dot(p_ref[...].astype(BF16), wp_ref[...]) * gate
    o_ref[...] = _rms(h3, gfin_ref[...]) if last else h3


def _combine(off, h1, rank_tok, aff_tok, p, gp, wg, wp, gfin, ye, cap, last):
    n = h1.shape[0]
    tile = COMBINE_TILE
    row = lambda i, o: (i, 0)
    fix = lambda i, o: (0, 0)
    return pl.pallas_call(
        functools.partial(_combine_body, cap=cap, last=last),
        grid_spec=pltpu.PrefetchScalarGridSpec(
            num_scalar_prefetch=1,
            grid=(n // tile,),
            in_specs=[pl.BlockSpec((tile, D_MODEL), row),
                      pl.BlockSpec((tile, N_EXPERTS), row),
                      pl.BlockSpec((tile, N_EXPERTS), row),
                      pl.BlockSpec((tile, PLE_DIM), row),
                      pl.BlockSpec((1, D_MODEL), fix),
                      pl.BlockSpec((D_MODEL, D_MODEL), fix),
                      pl.BlockSpec((PLE_DIM, D_MODEL), fix),
                      pl.BlockSpec((1, D_MODEL), fix),
                      pl.BlockSpec(memory_space=pl.ANY)],
            out_specs=pl.BlockSpec((tile, D_MODEL), row),
            scratch_shapes=[pltpu.VMEM((2, N_EXPERTS * COMBINE_WIN, D_MODEL), BF16),
                            pltpu.SemaphoreType.DMA((2, N_EXPERTS))],
        ),
        out_shape=jax.ShapeDtypeStruct((n, D_MODEL), F32),
        compiler_params=_cparams(("arbitrary",)),
        name="combine",
    )(off, h1, rank_tok, aff_tok, p, gp, wg, wp, gfin, ye)


def _gather_rows(table, rank, cap):
    n, w = table.shape
    workers_per_expert = SC_CORES * SC_SUBCORES // N_EXPERTS
    per = cap // workers_per_expert
    nch = per // GATHER_ROWS
    mesh = plsc.VectorSubcoreMesh(core_axis_name="c", subcore_axis_name="s",
                                  num_cores=SC_CORES, num_subcores=SC_SUBCORES)
    cp = dataclasses.replace(pltpu.CompilerParams(), needs_layout_passes=False)

    @functools.partial(
        pl.kernel, mesh=mesh, compiler_params=cp,
        out_type=jax.ShapeDtypeStruct((N_EXPERTS * cap, w), I32),
        scratch_types=[pltpu.VMEM((RANK_CHUNK,), I32),
                       pltpu.VMEM((nch, GATHER_ROWS), I32),
                       pltpu.VMEM((GATHER_ROWS, w), I32),
                       pltpu.SemaphoreType.DMA],
        name="sc_gather")
    def gather(table_hbm, rank_hbm, out_hbm, rbuf, idx, rows, sem):
        wid = lax.axis_index("s") * SC_CORES + lax.axis_index("c")
        e = wid // workers_per_expert
        lo = (wid % workers_per_expert) * per
        lane = lax.iota(I32, SC_LANES)

        @pl.loop(0, n // RANK_CHUNK)
        def _(c):
            pltpu.sync_copy(rank_hbm.at[e, pl.ds(c * RANK_CHUNK, RANK_CHUNK)], rbuf)

            @pl.loop(0, RANK_CHUNK // SC_LANES)
            def _(i):
                rel = rbuf[pl.ds(i * SC_LANES, SC_LANES)] - lo
                mask = jnp.logical_and(rel >= 0, rel < per)
                rel = jnp.where(mask, rel, 0)
                tok = c * RANK_CHUNK + i * SC_LANES + lane
                plsc.store_scatter(idx, [rel // GATHER_ROWS, rel % GATHER_ROWS], tok, mask=mask)

        @pl.loop(0, nch)
        def _(c):
            pltpu.async_copy(table_hbm.at[idx.at[c]], rows, sem).wait()
            pltpu.sync_copy(rows, out_hbm.at[pl.ds(e * cap + lo + c * GATHER_ROWS, GATHER_ROWS)])

    return gather(table, rank).reshape(N_EXPERTS, cap, w)


def _trunk(x, p, prm, b, s):
    n = b * s
    nc = s // SSM_CHUNK
    cap = EC_CAPACITY_FACTOR * n // N_EXPERTS
    cs, ss = _dft_tables(s)
    cs, ss = cs.astype(BF16), ss.astype(BF16)
    h = x.reshape(n, D_MODEL)
    depth = prm['w_in'].shape[0]
    for l in range(depth):
        lp = prm['layers'][l]
        zf, zs, zqkv = _inproj(h, lp['g_mix'], lp['w_in'])
        mf = _fourier(zf, cs, ss, prm['ccb'], prm['scb'], lp['w_fnet'], lp['g_f'], b, s)
        u = zs.reshape(n // SSM_CHUNK, SSM_ROW)
        xl = _s5_in(u, lp['s5_w_in'])
        pr, pi = _scan_powers(lp['s5_a_chunk'], int(math.log2(nc)))
        st = _s5_scan(xl, pr, pi, b, nc)
        y = _s5_out(u, st, lp['s5_m_intra'], lp['s5_w_out']).reshape(n, SSM_WIDTH)
        ms = _s5_post(y, zs, lp['s5_d'], lp['s5_glu_w'], lp['s5_glu_b'], lp['g_s'])
        oa = _attn(zqkv, prm['slopes'], b, s)
        h1, m, aff_t = _outproj(h, mf, ms, oa, lp['g_a'], lp['w_out'], lp['g_ffn'], lp['w_router_t'])
        rank, off = _select(aff_t, cap)
        off = jnp.concatenate([off, jnp.full((N_EXPERTS, 1), cap, I32)], axis=1)
        xe = _gather_rows(m, rank, cap)
        ye = _ffn(xe, lp['w_gate'], lp['w_up'], lp['w_down'])
        h = _combine(off, h1, rank.T, aff_t.T, p[l].reshape(n, PLE_DIM), lp['g_ple'], lp['w_ple_gate'],
                     lp['w_ple_proj'], prm['g_final'], ye, cap, l == depth - 1)
    return h.reshape(b, s, D_MODEL)


def kernel(x_prompt, x_sample, p_prompt, p_sample, norm_mix, w_in, w_fnet, ssm_a_re, ssm_a_im, ssm_log_dt, ssm_b_re, ssm_b_im, ssm_c_re, ssm_c_im, ssm_d, ssm_glu_w, ssm_glu_b, norm_branch, w_out, norm_ffn, w_router, w_exp_gate, w_exp_up, w_exp_down, norm_ple, w_ple_gate, w_ple_proj, norm_final):
    depth = w_in.shape[0]
    o1, o2 = FN_WIDTH, FN_WIDTH + SSM_WIDTH
    row = lambda v: v.reshape(1, -1).astype(F32)
    cc, sc = _dft_tables(FN_HEAD_DIM)
    tile4 = lambda mtx: _block_diag(jnp.broadcast_to(mtx[None], (FN_HEADS,) + mtx.shape)).astype(BF16)
    layers = []
    for l in range(depth):
        m_intra, s5_w_in, s5_w_out, a_chunk = _s5_matrices(
            ssm_a_re[l], ssm_a_im[l], ssm_log_dt[l], ssm_b_re[l], ssm_b_im[l], ssm_c_re[l], ssm_c_im[l])
        layers.append(dict(
            g_mix=row(norm_mix[l]), w_in=w_in[l].astype(BF16),
            w_fnet=_block_diag(w_fnet[l]).astype(BF16),
            g_f=row(norm_branch[l][:o1]), g_s=row(norm_branch[l][o1:o2]), g_a=row(norm_branch[l][o2:]),
            s5_m_intra=m_intra, s5_w_in=s5_w_in, s5_w_out=s5_w_out, s5_a_chunk=a_chunk,
            s5_d=row(ssm_d[l]), s5_glu_w=ssm_glu_w[l].astype(BF16), s5_glu_b=row(ssm_glu_b[l]),
            w_out=w_out[l].astype(BF16), g_ffn=row(norm_ffn[l]),
            w_router_t=w_router[l].T.astype(BF16),
            w_gate=w_exp_gate[l].astype(BF16), w_up=w_exp_up[l].astype(BF16),
            w_down=w_exp_down[l].astype(BF16),
            g_ple=row(norm_ple[l]), w_ple_gate=w_ple_gate[l].astype(BF16),
            w_ple_proj=w_ple_proj[l].astype(BF16)))
    slopes = jnp.asarray([2.0 ** (-8.0 * (i + 1) / ATT_HEADS) for i in range(ATT_HEADS)], F32)
    prm = dict(w_in=w_in, layers=layers, ccb=tile4(cc), scb=tile4(sc), slopes=slopes,
               g_final=row(norm_final))
    bp, sp = x_prompt.shape[0], x_prompt.shape[1]
    bs, ssq = x_sample.shape[0], x_sample.shape[1]
    y_prompt = _trunk(x_prompt, p_prompt, prm, bp, sp)
    y_sample = _trunk(x_sample, p_sample, prm, bs, ssq)
    return (y_prompt, y_sample)
```

```python
import dataclasses
import functools
import math

import jax
import jax.numpy as jnp
from jax import lax
from jax.experimental import pallas as pl
from jax.experimental.pallas import tpu as pltpu
from jax.experimental.pallas import tpu_sc as plsc

D_MODEL = 1024
FN_WIDTH = 256
FN_HEADS = 4
FN_HEAD_DIM = 64
SSM_WIDTH = 256
SSM_GROUP = 16
SSM_GROUPS = 16
SSM_STATE = 64
ATT_WIDTH = 512
ATT_HEAD_DIM = 64
ATT_HEADS = 8
DILATED_PATTERNS = ((128, 1), (512, 4), (2048, 16))
IN_PROJ_WIDTH = 2048
N_EXPERTS = 16
EC_CAPACITY_FACTOR = 2
D_FF_EXPERT = 2048
PLE_DIM = 256
RMS_EPS = 1e-6
NEG_INF = -1e30

LANES = 128
SSM_CHUNK = 8
SSM_ROW = SSM_CHUNK * SSM_WIDTH
SSM_NSTATE = SSM_GROUPS * SSM_STATE
ATT_HALF = 64
ATT_UNROLL = 4
ATT_SPLIT = 4
ATT_OFFSETS = 3
COMBINE_TILE = 256
COMBINE_WIN = 64
VMEM_LIMIT = 56 * 1024 * 1024
SC_CORES = 2
SC_SUBCORES = 16
SC_LANES = 16
GATHER_ROWS = 64
RANK_CHUNK = 2048
PACKED = D_MODEL // 2

F32 = jnp.float32
BF16 = jnp.bfloat16
I32 = jnp.int32


def _cparams(sem):
    return pltpu.CompilerParams(dimension_semantics=sem, vmem_limit_bytes=VMEM_LIMIT)


def _rms(x, g):
    return x * lax.rsqrt(jnp.mean(x * x, axis=-1, keepdims=True) + RMS_EPS) * g


def _dot(a, b):
    return jnp.dot(a, b, preferred_element_type=F32)


def _dot_nt(a, b):
    return lax.dot_general(a, b, (((1,), (1,)), ((), ())), preferred_element_type=F32)


def _inproj_body(h_ref, g_ref, w_ref, zf_ref, zs_ref, zqkv_ref):
    a = _rms(h_ref[...], g_ref[...]).astype(BF16)
    z = _dot(a, w_ref[...])
    zf_ref[...] = z[:, :FN_WIDTH]
    zs_ref[...] = z[:, FN_WIDTH:FN_WIDTH + SSM_WIDTH]
    zqkv_ref[...] = z[:, FN_WIDTH + SSM_WIDTH:].astype(BF16)


def _inproj(h, g, w):
    n = h.shape[0]
    tm = 512
    return pl.pallas_call(
        _inproj_body,
        grid=(n // tm,),
        in_specs=[pl.BlockSpec((tm, D_MODEL), lambda i: (i, 0)),
                  pl.BlockSpec((1, D_MODEL), lambda i: (0, 0)),
                  pl.BlockSpec((D_MODEL, IN_PROJ_WIDTH), lambda i: (0, 0))],
        out_specs=[pl.BlockSpec((tm, FN_WIDTH), lambda i: (i, 0)),
                   pl.BlockSpec((tm, SSM_WIDTH), lambda i: (i, 0)),
                   pl.BlockSpec((tm, 3 * ATT_WIDTH), lambda i: (i, 0))],
        out_shape=[jax.ShapeDtypeStruct((n, FN_WIDTH), F32),
                   jax.ShapeDtypeStruct((n, SSM_WIDTH), F32),
                   jax.ShapeDtypeStruct((n, 3 * ATT_WIDTH), BF16)],
        compiler_params=_cparams(("arbitrary",)),
        name="inproj",
    )(h, g, w)


def _fourier_body(x_ref, cs_ref, ss_ref, cc_ref, sc_ref, wb_ref, g_ref, o_ref):
    x = x_ref[0].astype(BF16)
    y = _dot(cs_ref[...], x).astype(BF16)
    z = _dot(ss_ref[...], x).astype(BF16)
    f = _dot(y, cc_ref[...]) - _dot(z, sc_ref[...])
    o = _dot(f.astype(BF16), wb_ref[...])
    o_ref[0] = _rms(o, g_ref[...]).astype(BF16)


def _dft_tables(n):
    k = (jnp.arange(n, dtype=I32)[:, None] * jnp.arange(n, dtype=I32)[None, :]) % n
    ang = k.astype(F32) * (2.0 * math.pi / n)
    return jnp.cos(ang), jnp.sin(ang)


def _fourier(zf, cs, ss, ccb, scb, wb, g, b, s):
    tr = 512
    x = zf.reshape(b, s, FN_WIDTH)
    out = pl.pallas_call(
        _fourier_body,
        grid=(s // tr, b),
        in_specs=[pl.BlockSpec((1, s, FN_WIDTH), lambda i, j: (j, 0, 0)),
                  pl.BlockSpec((tr, s), lambda i, j: (i, 0)),
                  pl.BlockSpec((tr, s), lambda i, j: (i, 0)),
                  pl.BlockSpec((FN_WIDTH, FN_WIDTH), lambda i, j: (0, 0)),
                  pl.BlockSpec((FN_WIDTH, FN_WIDTH), lambda i, j: (0, 0)),
                  pl.BlockSpec((FN_WIDTH, FN_WIDTH), lambda i, j: (0, 0)),
                  pl.BlockSpec((1, FN_WIDTH), lambda i, j: (0, 0))],
        out_specs=pl.BlockSpec((1, tr, FN_WIDTH), lambda i, j: (j, i, 0)),
        out_shape=jax.ShapeDtypeStruct((b, s, FN_WIDTH), BF16),
        compiler_params=_cparams(("arbitrary", "arbitrary")),
        name="fourier",
    )(x, cs, ss, ccb, scb, wb, g)
    return out.reshape(b * s, FN_WIDTH)


def _block_diag(blocks):
    h, a, bb = blocks.shape
    eye = jnp.eye(h, dtype=blocks.dtype)
    return jnp.einsum('hab,hg->hagb', blocks, eye).reshape(h * a, h * bb)


def _s5_matrices(a_re, a_im, log_dt, b_re, b_im, c_re, c_im):
    t = SSM_CHUNK
    g, p, c = SSM_GROUPS, SSM_STATE, SSM_GROUP
    lam = lax.complex(a_re.astype(F32), a_im.astype(F32))
    dt = jnp.exp(log_dt.astype(F32))[..., None]
    abar = jnp.exp(lam * dt)
    bbar = ((abar - 1.0) / lam)[..., None] * lax.complex(b_re.astype(F32), b_im.astype(F32))
    cmat = lax.complex(c_re.astype(F32), c_im.astype(F32))
    ks = jnp.arange(t + 1, dtype=F32)
    apow = jnp.exp((lam * dt)[:, None] * ks[None, :, None, None])
    eye = jnp.eye(g, dtype=F32)

    kern = jnp.real(jnp.einsum('dgcp,dkgp,dgpe->dkgce', cmat, apow[:, :t], bbar))
    lag = jnp.arange(t)[None, :] - jnp.arange(t)[:, None]
    kf = jnp.where((lag >= 0)[:, :, None, None, None], kern[0][jnp.clip(lag, 0, t - 1)], 0.0)
    kb = jnp.where((lag <= 0)[:, :, None, None, None], kern[1][jnp.clip(-lag, 0, t - 1)], 0.0)
    ktot = kf + kb
    m_intra = jnp.einsum('abgce,gh->agebhc', ktot, eye).reshape(SSM_ROW, SSM_ROW)

    wf = apow[0, t - 1 - jnp.arange(t)][:, :, :, None] * bbar[0][None]
    wb = apow[1, jnp.arange(t)][:, :, :, None] * bbar[1][None]
    def _w_in(w):
        w6 = jnp.einsum('rgpe,gh->rgehp', w, eye.astype(w.dtype))
        return w6.reshape(SSM_ROW, SSM_NSTATE)
    wf, wb = _w_in(wf), _w_in(wb)
    w_in = jnp.concatenate([jnp.real(wf), jnp.imag(wf), jnp.real(wb), jnp.imag(wb)], axis=1)

    qf = cmat[0][None] * apow[0, 1 + jnp.arange(t)][:, :, None, :]
    qb = cmat[1][None] * apow[1, t - jnp.arange(t)][:, :, None, :]
    def _w_out(q):
        q6 = jnp.einsum('rgcp,gh->gprhc', q, eye.astype(q.dtype))
        return q6.reshape(SSM_NSTATE, SSM_ROW)
    qf, qb = _w_out(qf), _w_out(qb)
    w_out = jnp.concatenate([jnp.real(qf), -jnp.imag(qf), jnp.real(qb), -jnp.imag(qb)], axis=0)

    a_chunk = lam * dt * t
    return m_intra.astype(BF16), w_in.astype(BF16), w_out.astype(BF16), a_chunk


def _scan_powers(a_chunk, nsteps):
    e = jnp.exp(a_chunk[None] * (2.0 ** jnp.arange(nsteps, dtype=F32))[:, None, None, None])
    e = e.reshape(nsteps, 2 * SSM_NSTATE)
    return jnp.real(e), jnp.imag(e)


def _s5_in_body(u_ref, w_ref, o_ref):
    o_ref[...] = _dot(u_ref[...].astype(BF16), w_ref[...])


def _s5_in(u, w_in):
    rows = u.shape[0]
    tr, tc = min(512, rows), 1024
    width = 4 * SSM_NSTATE
    return pl.pallas_call(
        _s5_in_body,
        grid=(width // tc, rows // tr),
        in_specs=[pl.BlockSpec((tr, SSM_ROW), lambda j, i: (i, 0)),
                  pl.BlockSpec((SSM_ROW, tc), lambda j, i: (0, j))],
        out_specs=pl.BlockSpec((tr, tc), lambda j, i: (i, j)),
        out_shape=jax.ShapeDtypeStruct((rows, width), F32),
        compiler_params=_cparams(("arbitrary", "arbitrary")),
        name="s5_in",
    )(u, w_in)


def _s5_scan_body(x_ref, pr_ref, pi_ref, o_ref, *, nc, nsteps):
    ns = SSM_NSTATE
    x = x_ref[0]
    fre, fim = x[:, 0:ns], x[:, ns:2 * ns]
    bre, bim = x[:, 2 * ns:3 * ns], x[:, 3 * ns:4 * ns]
    row = lax.broadcasted_iota(I32, (nc, 1), 0)
    for k in range(nsteps):
        sh = 2 ** k
        far, fai = pr_ref[k:k + 1, 0:ns], pi_ref[k:k + 1, 0:ns]
        bar, bai = pr_ref[k:k + 1, ns:2 * ns], pi_ref[k:k + 1, ns:2 * ns]
        fmask = row >= sh
        sre = jnp.where(fmask, pltpu.roll(fre, sh, 0), 0.0)
        sim = jnp.where(fmask, pltpu.roll(fim, sh, 0), 0.0)
        fre, fim = fre + far * sre - fai * sim, fim + far * sim + fai * sre
        bmask = row < nc - sh
        sre = jnp.where(bmask, pltpu.roll(bre, nc - sh, 0), 0.0)
        sim = jnp.where(bmask, pltpu.roll(bim, nc - sh, 0), 0.0)
        bre, bim = bre + bar * sre - bai * sim, bim + bar * sim + bai * sre
    fmask = row >= 1
    bmask = row < nc - 1
    o_ref[0, :, 0:ns] = jnp.where(fmask, pltpu.roll(fre, 1, 0), 0.0).astype(BF16)
    o_ref[0, :, ns:2 * ns] = jnp.where(fmask, pltpu.roll(fim, 1, 0), 0.0).astype(BF16)
    o_ref[0, :, 2 * ns:3 * ns] = jnp.where(bmask, pltpu.roll(bre, nc - 1, 0), 0.0).astype(BF16)
    o_ref[0, :, 3 * ns:4 * ns] = jnp.where(bmask, pltpu.roll(bim, nc - 1, 0), 0.0).astype(BF16)


def _s5_scan(xl, pr, pi, b, nc):
    nsteps = int(math.log2(nc))
    width = 4 * SSM_NSTATE
    x = xl.reshape(b, nc, width)
    out = pl.pallas_call(
        functools.partial(_s5_scan_body, nc=nc, nsteps=nsteps),
        grid=(b,),
        in_specs=[pl.BlockSpec((1, nc, width), lambda i: (i, 0, 0)),
                  pl.BlockSpec((nsteps, 2 * SSM_NSTATE), lambda i: (0, 0)),
                  pl.BlockSpec((nsteps, 2 * SSM_NSTATE), lambda i: (0, 0))],
        out_specs=pl.BlockSpec((1, nc, width), lambda i: (i, 0, 0)),
        out_shape=jax.ShapeDtypeStruct((b, nc, width), BF16),
        compiler_params=_cparams(("arbitrary",)),
        name="s5_scan",
    )(x, pr, pi)
    return out.reshape(b * nc, width)


def _s5_out_body(u_ref, s_ref, m_ref, w_ref, o_ref):
    o_ref[...] = _dot(u_ref[...].astype(BF16), m_ref[...]) + _dot(s_ref[...], w_ref[...])


def _s5_out(u, states, m_intra, w_out):
    rows = u.shape[0]
    tr, tc = min(512, rows), 512
    width = 4 * SSM_NSTATE
    return pl.pallas_call(
        _s5_out_body,
        grid=(SSM_ROW // tc, rows // tr),
        in_specs=[pl.BlockSpec((tr, SSM_ROW), lambda j, i: (i, 0)),
                  pl.BlockSpec((tr, width), lambda j, i: (i, 0)),
                  pl.BlockSpec((SSM_ROW, tc), lambda j, i: (0, j)),
                  pl.BlockSpec((width, tc), lambda j, i: (0, j))],
        out_specs=pl.BlockSpec((tr, tc), lambda j, i: (i, j)),
        out_shape=jax.ShapeDtypeStruct((rows, SSM_ROW), F32),
        compiler_params=_cparams(("arbitrary", "arbitrary")),
        name="s5_out",
    )(u, states, m_intra, w_out)


def _s5_post_body(y_ref, u_ref, d_ref, w_ref, b_ref, g_ref, o_ref):
    v = y_ref[...] + d_ref[...] * u_ref[...]
    c0 = math.sqrt(2.0 / math.pi)
    gl = 0.5 * v * (1.0 + jnp.tanh(c0 * (v + 0.044715 * (v * v * v))))
    gate = jax.nn.sigmoid(_dot(gl.astype(BF16), w_ref[...]) + b_ref[...])
    o_ref[...] = _rms(gl * gate, g_ref[...]).astype(BF16)


def _s5_post(y, u, d, w, bias, g):
    n = y.shape[0]
    tm = 2048
    row = lambda i: (i, 0)
    fix = lambda i: (0, 0)
    return pl.pallas_call(
        _s5_post_body,
        grid=(n // tm,),
        in_specs=[pl.BlockSpec((tm, SSM_WIDTH), row), pl.BlockSpec((tm, SSM_WIDTH), row),
                  pl.BlockSpec((1, SSM_WIDTH), fix), pl.BlockSpec((SSM_WIDTH, SSM_WIDTH), fix),
                  pl.BlockSpec((1, SSM_WIDTH), fix), pl.BlockSpec((1, SSM_WIDTH), fix)],
        out_specs=pl.BlockSpec((tm, SSM_WIDTH), row),
        out_shape=jax.ShapeDtypeStruct((n, SSM_WIDTH), BF16),
        compiler_params=_cparams(("arbitrary",)),
        name="s5_post",
    )(y, u, d, w, bias, g)


def _attn_geometry(s, d):
    ln = s // d
    bq = min(128, ln)
    bk = min(bq + 2 * ATT_HALF, ln)
    return ln, bq, bk, ln // bq


def _attn_body(slope_ref, q_ref, k_ref, v_ref, o_ref, nat, qd, kd, vd, qdb, kdb, vdb, b0, b1, b2, *acc, s):
    hp = pl.program_id(1)
    s4 = s // ATT_SPLIT
    lane = lax.broadcasted_iota(I32, (1, LANES), 1)
    first = lane < ATT_HEAD_DIM
    second = jnp.logical_not(first)
    slopes = (slope_ref[2 * hp], slope_ref[2 * hp + 1])
    log2e = math.log2(math.e)
    scale = ATT_HEAD_DIM ** -0.5 * log2e

    for (_, d), bias in zip(DILATED_PATTERNS, (b0, b1, b2)):
        _, bq, bk, _ = _attn_geometry(s, d)
        jk = lax.broadcasted_iota(I32, (1, bk), 1)
        for o in range(ATT_OFFSETS):
            rel = jnp.abs(lax.broadcasted_iota(I32, (bq, 1), 0) + o * ATT_HALF - jk)
            dist = (d * rel).astype(F32) * log2e
            for hh in range(2):
                bias[2 * o + hh] = jnp.where(rel <= ATT_HALF, -slopes[hh] * dist, NEG_INF)

    for src, dst_f, dst_b in ((q_ref, qd, qdb), (k_ref, kd, kdb), (v_ref, vd, vdb)):
        nat[...] = src[...].astype(F32)
        for c in range(ATT_SPLIT):
            x = nat[pl.ds(c, s4, stride=ATT_SPLIT), :]
            dst_f[c * s4:(c + 1) * s4, :] = x
            dst_b[c * s4:(c + 1) * s4, :] = x.astype(BF16)

    def pattern(p, d, refs, bias, locate, stride):
        acc_o, acc_m, acc_l = acc[3 * p], acc[3 * p + 1], acc[3 * p + 2]
        ln, bq, bk, nqb = _attn_geometry(s, d)

        def rows(c, j, size):
            start = locate(c, j)
            if stride == 1:
                return pl.ds(pl.multiple_of(start, ATT_HALF), size)
            return pl.ds(start, size, stride=stride)

        def scores(i):
            c = i // nqb
            j0 = (i % nqb) * bq
            ks = jnp.clip(j0 - ATT_HALF, 0, ln - bk)
            q = refs[0][rows(c, j0, bq), :].astype(BF16)
            k = refs[1][rows(c, ks, bk), :].astype(BF16)
            v = refs[2][rows(c, ks, bk), :].astype(BF16)
            off = (j0 - ks) // ATT_HALF
            scs = []
            for hh in range(2):
                qm = jnp.where(first if hh == 0 else second, q, jnp.zeros_like(q))
                scs.append(_dot_nt(qm, k) * scale + bias[2 * off + hh])
            return rows(c, j0, bq), v, scs

        def softmax(sc):
            m = jnp.max(sc, axis=-1, keepdims=True)
            pe = jnp.exp2(sc - m)
            return pe.astype(BF16), m, jnp.sum(pe, axis=-1, keepdims=True)

        def group(g, carry):
            staged = [scores(g * ATT_UNROLL + u) for u in range(ATT_UNROLL)]
            soft = [[softmax(sc) for sc in scs] for _, _, scs in staged]
            for (dst, v, _), ((p0, m0, l0), (p1, m1, l1)) in zip(staged, soft):
                acc_o[dst, :] = jnp.where(first, _dot(p0, v), _dot(p1, v))
                acc_m[dst, :] = jnp.where(first, m0, m1)
                acc_l[dst, :] = jnp.where(first, l0, l1)
            return carry

        lax.fori_loop(0, d * nqb // ATT_UNROLL, group, 0)

    (_, d1), (_, d2), (_, d3) = DILATED_PATTERNS
    assert d1 == 1 and d2 == ATT_SPLIT and d3 == ATT_SPLIT * ATT_SPLIT
    pattern(0, d1, (q_ref, k_ref, v_ref), b0, lambda c, j: j, 1)
    pattern(1, d2, (qdb, kdb, vdb), b1, lambda c, j: c * s4 + j, 1)
    pattern(2, d3, (qd, kd, vd), b2,
            lambda c, j: (c % ATT_SPLIT) * s4 + c // ATT_SPLIT + ATT_SPLIT * j, ATT_SPLIT)

    for c in range(ATT_SPLIT):
        part = pl.ds(c, s4, stride=ATT_SPLIT)
        blk = slice(c * s4, (c + 1) * s4)
        ms = (acc[1][part, :], acc[4][blk, :], acc[7][blk, :])
        os_ = (acc[0][part, :], acc[3][blk, :], acc[6][blk, :])
        ls = (acc[2][part, :], acc[5][blk, :], acc[8][blk, :])
        m = jnp.maximum(jnp.maximum(ms[0], ms[1]), ms[2])
        num = jnp.zeros((s4, LANES), F32)
        den = jnp.zeros((s4, LANES), F32)
        for p in range(3):
            w = jnp.exp2(ms[p] - m)
            num = num + w * os_[p]
            den = den + w * ls[p]
        nat[part, :] = num / den
    o_ref[...] = nat[...].astype(BF16)


def _attn(zqkv, slopes, b, s):
    n = b * s
    nhp = ATT_HEADS // 2
    col = lambda off: (lambda i, j, sl: (i, off + j))
    return pl.pallas_call(
        functools.partial(_attn_body, s=s),
        grid_spec=pltpu.PrefetchScalarGridSpec(
            num_scalar_prefetch=1,
            grid=(b, nhp),
            in_specs=[pl.BlockSpec((s, LANES), col(0)),
                      pl.BlockSpec((s, LANES), col(nhp)),
                      pl.BlockSpec((s, LANES), col(2 * nhp))],
            out_specs=pl.BlockSpec((s, LANES), lambda i, j, sl: (i, j)),
            scratch_shapes=([pltpu.VMEM((s, LANES), F32) for _ in range(4)]
                            + [pltpu.VMEM((s, LANES), BF16) for _ in range(3)]
                            + [pltpu.VMEM((2 * ATT_OFFSETS,) + _attn_geometry(s, d)[1:3], F32)
                               for _, d in DILATED_PATTERNS]
                            + [pltpu.VMEM((s, LANES), F32) for _ in range(9)]),
        ),
        out_shape=jax.ShapeDtypeStruct((n, ATT_WIDTH), BF16),
        compiler_params=_cparams(("arbitrary", "arbitrary")),
        name="attn",
    )(slopes, zqkv, zqkv, zqkv)


def _outproj_body(h_ref, mf_ref, ms_ref, oa_ref, ga_ref, w_ref, gf_ref, wr_ref,
                  h1_ref, m_ref, aff_ref):
    oa = _rms(oa_ref[...].astype(F32), ga_ref[...]).astype(BF16)
    o1, o2 = FN_WIDTH, FN_WIDTH + SSM_WIDTH
    acc = h_ref[...] + _dot(mf_ref[...], w_ref[0:o1, :]) + _dot(ms_ref[...], w_ref[o1:o2, :])
    acc = acc + _dot(oa, w_ref[o2:, :])
    h1_ref[...] = acc
    m = _rms(acc, gf_ref[...]).astype(BF16)
    bits = pltpu.bitcast(m.astype(F32), I32)
    m_ref[...] = lax.shift_right_logical(bits[:, :PACKED], 16) | (bits[:, PACKED:] & jnp.int32(-65536))
    lg = _dot_nt(wr_ref[...], m)
    e = jnp.exp(lg - jnp.max(lg, axis=0, keepdims=True))
    aff_ref[...] = e / jnp.sum(e, axis=0, keepdims=True)


def _outproj(h, mf, ms, oa, ga, w, gf, wr_t):
    n = h.shape[0]
    tm = 512
    row = lambda i: (i, 0)
    fix = lambda i: (0, 0)
    return pl.pallas_call(
        _outproj_body,
        grid=(n // tm,),
        in_specs=[pl.BlockSpec((tm, D_MODEL), row), pl.BlockSpec((tm, FN_WIDTH), row),
                  pl.BlockSpec((tm, SSM_WIDTH), row), pl.BlockSpec((tm, ATT_WIDTH), row),
                  pl.BlockSpec((1, ATT_WIDTH), fix), pl.BlockSpec((D_MODEL, D_MODEL), fix),
                  pl.BlockSpec((1, D_MODEL), fix), pl.BlockSpec((N_EXPERTS, D_MODEL), fix)],
        out_specs=[pl.BlockSpec((tm, D_MODEL), row), pl.BlockSpec((tm, PACKED), row),
                   pl.BlockSpec((N_EXPERTS, tm), lambda i: (0, i))],
        out_shape=[jax.ShapeDtypeStruct((n, D_MODEL), F32),
                   jax.ShapeDtypeStruct((n, PACKED), I32),
                   jax.ShapeDtypeStruct((N_EXPERTS, n), F32)],
        compiler_params=_cparams(("arbitrary",)),
        name="outproj",
    )(h, mf, ms, oa, ga, w, gf, wr_t)


def _select_body(aff_ref, rank_ref, off_ref, *, cap, nb):
    r128 = lax.broadcasted_iota(I32, (LANES, LANES), 0)
    c128 = lax.broadcasted_iota(I32, (LANES, LANES), 1)
    upper_incl = jnp.where(r128 <= c128, 1.0, 0.0).astype(BF16)
    ones = jnp.ones((LANES, LANES), BF16)
    rb = lax.broadcasted_iota(I32, (nb, nb), 0)
    cb = lax.broadcasted_iota(I32, (nb, nb), 1)
    lower_strict = jnp.where(cb < rb, 1.0, 0.0).astype(BF16)
    upper_strict = jnp.where(rb < cb, 1.0, 0.0).astype(BF16)
    ones8 = jnp.ones((8, LANES), BF16)

    def count(mask):
        c = jnp.sum(jnp.where(mask, 1.0, 0.0), axis=1, keepdims=True)
        return jnp.sum(c, axis=0, keepdims=True)

    def prefix(mask):
        mb = jnp.where(mask, 1.0, 0.0).astype(BF16)
        incl = _dot(mb, upper_incl)
        tot = _dot(mb, ones)
        offs = _dot(lower_strict, tot.astype(BF16))
        return offs + incl - 1.0, mb

    def per_expert(e, carry):
        bits = pltpu.bitcast(aff_ref[e], I32)

        def bitstep(i, t):
            cand = t | jnp.left_shift(jnp.int32(1), 30 - i)
            return jnp.where(count(bits >= cand) >= cap, cand, t)

        t = lax.fori_loop(0, 31, bitstep, jnp.zeros((1, 1), I32))
        gt = bits > t
        eq = bits == t
        need = cap - count(gt)
        eq_rank, _ = prefix(eq)
        sel = jnp.logical_or(gt, jnp.logical_and(eq, eq_rank < need))
        rank, mb = prefix(sel)
        rank_ref[e] = jnp.where(sel, rank.astype(I32), -1)
        tot_row = _dot_nt(ones8, mb)
        off_row = _dot(tot_row.astype(BF16), upper_strict)
        off_ref[pl.ds(e, 1), :] = off_row[0:1, :].astype(I32)
        return carry

    lax.fori_loop(0, N_EXPERTS, per_expert, 0)


def _select(aff_t, cap):
    n = aff_t.shape[1]
    nb = n // LANES
    a3 = aff_t.reshape(N_EXPERTS, nb, LANES)
    rank, off = pl.pallas_call(
        functools.partial(_select_body, cap=cap, nb=nb),
        grid=(1,),
        in_specs=[pl.BlockSpec((N_EXPERTS, nb, LANES), lambda i: (0, 0, 0))],
        out_specs=[pl.BlockSpec((N_EXPERTS, nb, LANES), lambda i: (0, 0, 0)),
                   pl.BlockSpec((N_EXPERTS, nb), lambda i: (0, 0))],
        out_shape=[jax.ShapeDtypeStruct((N_EXPERTS, nb, LANES), I32),
                   jax.ShapeDtypeStruct((N_EXPERTS, nb), I32)],
        compiler_params=_cparams(("arbitrary",)),
        name="select",
    )(a3)
    return rank.reshape(N_EXPERTS, n), off


def _ffn_body(x_ref, wg_ref, wu_ref, wd_ref, o_ref):
    w = x_ref[0]
    x = jnp.concatenate([pltpu.bitcast(lax.shift_left(w, 16), F32),
                         pltpu.bitcast(w & jnp.int32(-65536), F32)], axis=1).astype(BF16)
    tf = 512
    acc = jnp.zeros(o_ref.shape[1:], F32)
    for j in range(D_FF_EXPERT // tf):
        g = _dot(x, wg_ref[0, :, j * tf:(j + 1) * tf])
        u = _dot(x, wu_ref[0, :, j * tf:(j + 1) * tf])
        hdn = (g * jax.nn.sigmoid(g) * u).astype(BF16)
        acc = acc + _dot(hdn, wd_ref[0, j * tf:(j + 1) * tf, :])
    o_ref[0] = acc.astype(BF16)


def _ffn(xe, wg, wu, wd):
    e, cap, _ = xe.shape
    tm = min(512, cap)
    return pl.pallas_call(
        _ffn_body,
        grid=(e, cap // tm),
        in_specs=[pl.BlockSpec((1, tm, PACKED), lambda i, j: (i, j, 0)),
                  pl.BlockSpec((1, D_MODEL, D_FF_EXPERT), lambda i, j: (i, 0, 0)),
                  pl.BlockSpec((1, D_MODEL, D_FF_EXPERT), lambda i, j: (i, 0, 0)),
                  pl.BlockSpec((1, D_FF_EXPERT, D_MODEL), lambda i, j: (i, 0, 0))],
        out_specs=pl.BlockSpec((1, tm, D_MODEL), lambda i, j: (i, j, 0)),
        out_shape=jax.ShapeDtypeStruct((e, cap, D_MODEL), BF16),
        compiler_params=_cparams(("arbitrary", "arbitrary")),
        name="ffn",
    )(xe, wg, wu, wd)


def _combine_body(off_ref, h_ref, rank_ref, aff_ref, p_ref, gp_ref, wg_ref, wp_ref, gfin_ref, ye_ref,
                  o_ref, stack, sems, *, cap, last):
    t = pl.program_id(0)
    nt = pl.num_programs(0)
    tile, win = COMBINE_TILE, COMBINE_WIN
    bpt = tile // LANES
    kdim = N_EXPERTS * win

    def base(tt, e):
        return (off_ref[e, tt * bpt] // 16) * 16

    def copy(e, start, slot):
        return pltpu.make_async_copy(ye_ref.at[e, pl.ds(pl.multiple_of(start, 16), win), :],
                                     stack.at[slot, pl.ds(e * win, win), :], sems.at[slot, e])

    def starts(tt, r):
        want = [base(tt, e) + r * win for e in range(N_EXPERTS)]
        return want, [jnp.minimum(w, cap - win) for w in want]

    def issue(tt, r, slot):
        _, got = starts(tt, r)
        for e in range(N_EXPERTS):
            copy(e, got[e], slot).start()

    def wait(tt, r, slot):
        _, got = starts(tt, r)
        for e in range(N_EXPERTS):
            copy(e, got[e], slot).wait()

    slot = t % 2

    @pl.when(t == 0)
    def _():
        issue(0, 0, 0)

    @pl.when(t + 1 < nt)
    def _():
        issue(t + 1, 0, 1 - slot)

    lane16 = lax.broadcasted_iota(I32, (1, N_EXPERTS), 1)
    er = lax.broadcasted_iota(I32, (N_EXPERTS, kdim), 0)
    ec = lax.broadcasted_iota(I32, (N_EXPERTS, kdim), 1)
    expand = jnp.where(ec // win == er, 1.0, 0.0).astype(BF16)
    lane_in = (lax.broadcasted_iota(I32, (1, kdim), 1) % win).astype(F32)

    aff = aff_ref[...]
    a_hi = aff.astype(BF16)
    a_lo = (aff - a_hi.astype(F32)).astype(BF16)
    gates = _dot(a_hi, expand) + _dot(a_lo, expand)
    rk = rank_ref[...]

    def row_of(vals):
        r = jnp.zeros((1, N_EXPERTS), I32)
        for e in range(N_EXPERTS):
            r = jnp.where(lane16 == e, vals[e], r)
        return r

    def accumulate(r, slot_):
        want, got = starts(t, r)
        wrow, grow = row_of(want), row_of(got)
        ok = jnp.logical_and(rk >= wrow, rk < grow + win)
        rel = jnp.where(ok, rk - grow, -1).astype(F32).astype(BF16)
        hit = _dot(rel, expand) == lane_in
        w = jnp.where(hit, gates, 0.0).astype(BF16)
        o_ref[...] += _dot(w, stack[slot_])

    o_ref[...] = h_ref[...]
    wait(t, 0, slot)
    accumulate(0, slot)

    nr = jnp.int32(1)
    for e in range(N_EXPERTS):
        span = off_ref[e, (t + 1) * bpt] - base(t, e)
        nr = jnp.maximum(nr, (span + win - 1) // win)

    def extra(r, carry):
        issue(t, r, slot)
        wait(t, r, slot)
        accumulate(r, slot)
        return carry

    lax.fori_loop(1, nr, extra, 0)

    h2 = o_ref[...]
    gate = jax.nn.sigmoid(_dot(_rms(h2, gp_ref[...]).astype(BF16), wg_ref[...]))
    h3 = h2 + _dot(p_ref[...].astype(BF16), wp_ref[...]) * gate
    o_ref[...] = _rms(h3, gfin_ref[...]) if last else h3


def _combine(off, h1, rank_tok, aff_tok, p, gp, wg, wp, gfin, ye, cap, last):
    n = h1.shape[0]
    tile = COMBINE_TILE
    row = lambda i, o: (i, 0)
    fix = lambda i, o: (0, 0)
    return pl.pallas_call(
        functools.partial(_combine_body, cap=cap, last=last),
        grid_spec=pltpu.PrefetchScalarGridSpec(
            num_scalar_prefetch=1,
            grid=(n // tile,),
            in_specs=[pl.BlockSpec((tile, D_MODEL), row),
                      pl.BlockSpec((tile, N_EXPERTS), row),
                      pl.BlockSpec((tile, N_EXPERTS), row),
                      pl.BlockSpec((tile, PLE_DIM), row),
                      pl.BlockSpec((1, D_MODEL), fix),
                      pl.BlockSpec((D_MODEL, D_MODEL), fix),
                      pl.BlockSpec((PLE_DIM, D_MODEL), fix),
                      pl.BlockSpec((1, D_MODEL), fix),
                      pl.BlockSpec(memory_space=pl.ANY)],
            out_specs=pl.BlockSpec((tile, D_MODEL), row),
            scratch_shapes=[pltpu.VMEM((2, N_EXPERTS * COMBINE_WIN, D_MODEL), BF16),
                            pltpu.SemaphoreType.DMA((2, N_EXPERTS))],
        ),
        out_shape=jax.ShapeDtypeStruct((n, D_MODEL), F32),
        compiler_params=_cparams(("arbitrary",)),
        name="combine",
    )(off, h1, rank_tok, aff_tok, p, gp, wg, wp, gfin, ye)


def _gather_rows(table, rank, cap):
    n, w = table.shape
    workers_per_expert = SC_CORES * SC_SUBCORES // N_EXPERTS
    per = cap // workers_per_expert
    nch = per // GATHER_ROWS
    mesh = plsc.VectorSubcoreMesh(core_axis_name="c", subcore_axis_name="s",
                                  num_cores=SC_CORES, num_subcores=SC_SUBCORES)
    cp = dataclasses.replace(pltpu.CompilerParams(), needs_layout_passes=False)

    @functools.partial(
        pl.kernel, mesh=mesh, compiler_params=cp,
        out_type=jax.ShapeDtypeStruct((N_EXPERTS * cap, w), I32),
        scratch_types=[pltpu.VMEM((RANK_CHUNK,), I32),
                       pltpu.VMEM((nch, GATHER_ROWS), I32),
                       pltpu.VMEM((GATHER_ROWS, w), I32),
                       pltpu.SemaphoreType.DMA],
        name="sc_gather")
    def gather(table_hbm, rank_hbm, out_hbm, rbuf, idx, rows, sem):
        wid = lax.axis_index("s") * SC_CORES + lax.axis_index("c")
        e = wid // workers_per_expert
        lo = (wid % workers_per_expert) * per
        lane = lax.iota(I32, SC_LANES)

        @pl.loop(0, n // RANK_CHUNK)
        def _(c):
            pltpu.sync_copy(rank_hbm.at[e, pl.ds(c * RANK_CHUNK, RANK_CHUNK)], rbuf)

            @pl.loop(0, RANK_CHUNK // SC_LANES)
            def _(i):
                rel = rbuf[pl.ds(i * SC_LANES, SC_LANES)] - lo
                mask = jnp.logical_and(rel >= 0, rel < per)
                rel = jnp.where(mask, rel, 0)
                tok = c * RANK_CHUNK + i * SC_LANES + lane
                plsc.store_scatter(idx, [rel // GATHER_ROWS, rel % GATHER_ROWS], tok, mask=mask)

        @pl.loop(0, nch)
        def _(c):
            pltpu.async_copy(table_hbm.at[idx.at[c]], rows, sem).wait()
            pltpu.sync_copy(rows, out_hbm.at[pl.ds(e * cap + lo + c * GATHER_ROWS, GATHER_ROWS)])

    return gather(table, rank).reshape(N_EXPERTS, cap, w)


def _trunk(x, p, prm, b, s):
    n = b * s
    nc = s // SSM_CHUNK
    cap = EC_CAPACITY_FACTOR * n // N_EXPERTS
    cs, ss = _dft_tables(s)
    cs, ss = cs.astype(BF16), ss.astype(BF16)
    h = x.reshape(n, D_MODEL)
    depth = prm['w_in'].shape[0]
    for l in range(depth):
        lp = prm['layers'][l]
        zf, zs, zqkv = _inproj(h, lp['g_mix'], lp['w_in'])
        mf = _fourier(zf, cs, ss, prm['ccb'], prm['scb'], lp['w_fnet'], lp['g_f'], b, s)
        u = zs.reshape(n // SSM_CHUNK, SSM_ROW)
        xl = _s5_in(u, lp['s5_w_in'])
        pr, pi = _scan_powers(lp['s5_a_chunk'], int(math.log2(nc)))
        st = _s5_scan(xl, pr, pi, b, nc)
        y = _s5_out(u, st, lp['s5_m_intra'], lp['s5_w_out']).reshape(n, SSM_WIDTH)
        ms = _s5_post(y, zs, lp['s5_d'], lp['s5_glu_w'], lp['s5_glu_b'], lp['g_s'])
        oa = _attn(zqkv, prm['slopes'], b, s)
        h1, m, aff_t = _outproj(h, mf, ms, oa, lp['g_a'], lp['w_out'], lp['g_ffn'], lp['w_router_t'])
        rank, off = _select(aff_t, cap)
        off = jnp.concatenate([off, jnp.full((N_EXPERTS, 1), cap, I32)], axis=1)
        xe = _gather_rows(m, rank, cap)
        ye = _ffn(xe, lp['w_gate'], lp['w_up'], lp['w_down'])
        h = _combine(off, h1, rank.T, aff_t.T, p[l].reshape(n, PLE_DIM), lp['g_ple'], lp['w_ple_gate'],
                     lp['w_ple_proj'], prm['g_final'], ye, cap, l == depth - 1)
    return h.reshape(b, s, D_MODEL)


def kernel(x_prompt, x_sample, p_prompt, p_sample, norm_mix, w_in, w_fnet, ssm_a_re, ssm_a_im, ssm_log_dt, ssm_b_re, ssm_b_im, ssm_c_re, ssm_c_im, ssm_d, ssm_glu_w, ssm_glu_b, norm_branch, w_out, norm_ffn, w_router, w_exp_gate, w_exp_up, w_exp_down, norm_ple, w_ple_gate, w_ple_proj, norm_final):
    depth = w_in.shape[0]
    o1, o2 = FN_WIDTH, FN_WIDTH + SSM_WIDTH
    row = lambda v: v.reshape(1, -1).astype(F32)
    cc, sc = _dft_tables(FN_HEAD_DIM)
    tile4 = lambda mtx: _block_diag(jnp.broadcast_to(mtx[None], (FN_HEADS,) + mtx.shape)).astype(BF16)
    layers = []
    for l in range(depth):
        m_intra, s5_w_in, s5_w_out, a_chunk = _s5_matrices(
            ssm_a_re[l], ssm_a_im[l], ssm_log_dt[l], ssm_b_re[l], ssm_b_im[l], ssm_c_re[l], ssm_c_im[l])
        layers.append(dict(
            g_mix=row(norm_mix[l]), w_in=w_in[l].astype(BF16),
            w_fnet=_block_diag(w_fnet[l]).astype(BF16),
            g_f=row(norm_branch[l][:o1]), g_s=row(norm_branch[l][o1:o2]), g_a=row(norm_branch[l][o2:]),
            s5_m_intra=m_intra, s5_w_in=s5_w_in, s5_w_out=s5_w_out, s5_a_chunk=a_chunk,
            s5_d=row(ssm_d[l]), s5_glu_w=ssm_glu_w[l].astype(BF16), s5_glu_b=row(ssm_glu_b[l]),
            w_out=w_out[l].astype(BF16), g_ffn=row(norm_ffn[l]),
            w_router_t=w_router[l].T.astype(BF16),
            w_gate=w_exp_gate[l].astype(BF16), w_up=w_exp_up[l].astype(BF16),
            w_down=w_exp_down[l].astype(BF16),
            g_ple=row(norm_ple[l]), w_ple_gate=w_ple_gate[l].astype(BF16),
            w_ple_proj=w_ple_proj[l].astype(BF16)))
    slopes = jnp.asarray([2.0 ** (-8.0 * (i + 1) / ATT_HEADS) for i in range(ATT_HEADS)], F32)
    prm = dict(w_in=w_in, layers=layers, ccb=tile4(cc), scb=tile4(sc), slopes=slopes,
               g_final=row(norm_final))
    bp, sp = x_prompt.shape[0], x_prompt.shape[1]
    bs, ssq = x_sample.shape[0], x_sample.shape[1]
    y_prompt = _trunk(x_prompt, p_prompt, prm, bp, sp)
    y_sample = _trunk(x_sample, p_sample, prm, bs, ssq)
    return (y_prompt, y_sample)
```

```python
import dataclasses
import functools
import math

import jax
import jax.numpy as jnp
from jax import lax
from jax.experimental import pallas as pl
from jax.experimental.pallas import tpu as pltpu
from jax.experimental.pallas import tpu_sc as plsc

D_MODEL = 1024
FN_WIDTH = 256
FN_HEADS = 4
FN_HEAD_DIM = 64
SSM_WIDTH = 256
SSM_GROUP = 16
SSM_GROUPS = 16
SSM_STATE = 64
ATT_WIDTH = 512
ATT_HEAD_DIM = 64
ATT_HEADS = 8
DILATED_PATTERNS = ((128, 1), (512, 4), (2048, 16))
IN_PROJ_WIDTH = 2048
N_EXPERTS = 16
EC_CAPACITY_FACTOR = 2
D_FF_EXPERT = 2048
PLE_DIM = 256
RMS_EPS = 1e-6
NEG_INF = -1e30

LANES = 128
SSM_CHUNK = 8
SSM_ROW = SSM_CHUNK * SSM_WIDTH
SSM_NSTATE = SSM_GROUPS * SSM_STATE
S5_ROWS = 256
ATT_HALF = 64
ATT_UNROLL = 4
ATT_SPLIT = 4
ATT_OFFSETS = 3
COMBINE_TILE = 256
COMBINE_WIN = 64
VMEM_LIMIT = 56 * 1024 * 1024
SC_CORES = 2
SC_SUBCORES = 16
SC_LANES = 16
GATHER_ROWS = 64
RANK_CHUNK = 2048
PACKED = D_MODEL // 2

F32 = jnp.float32
BF16 = jnp.bfloat16
I32 = jnp.int32


def _cparams(sem):
    return pltpu.CompilerParams(dimension_semantics=sem, vmem_limit_bytes=VMEM_LIMIT)


def _rms(x, g):
    return x * lax.rsqrt(jnp.mean(x * x, axis=-1, keepdims=True) + RMS_EPS) * g


def _dot(a, b):
    return jnp.dot(a, b, preferred_element_type=F32)


def _dot_nt(a, b):
    return lax.dot_general(a, b, (((1,), (1,)), ((), ())), preferred_element_type=F32)


def _inproj_body(h_ref, g_ref, w_ref, zf_ref, zs_ref, zqkv_ref):
    a = _rms(h_ref[...], g_ref[...]).astype(BF16)
    z = _dot(a, w_ref[...])
    zf_ref[...] = z[:, :FN_WIDTH]
    zs_ref[...] = z[:, FN_WIDTH:FN_WIDTH + SSM_WIDTH]
    zqkv_ref[...] = z[:, FN_WIDTH + SSM_WIDTH:].astype(BF16)


def _inproj(h, g, w):
    n = h.shape[0]
    tm = 512
    return pl.pallas_call(
        _inproj_body,
        grid=(n // tm,),
        in_specs=[pl.BlockSpec((tm, D_MODEL), lambda i: (i, 0)),
                  pl.BlockSpec((1, D_MODEL), lambda i: (0, 0)),
                  pl.BlockSpec((D_MODEL, IN_PROJ_WIDTH), lambda i: (0, 0))],
        out_specs=[pl.BlockSpec((tm, FN_WIDTH), lambda i: (i, 0)),
                   pl.BlockSpec((tm, SSM_WIDTH), lambda i: (i, 0)),
                   pl.BlockSpec((tm, 3 * ATT_WIDTH), lambda i: (i, 0))],
        out_shape=[jax.ShapeDtypeStruct((n, FN_WIDTH), F32),
                   jax.ShapeDtypeStruct((n, SSM_WIDTH), F32),
                   jax.ShapeDtypeStruct((n, 3 * ATT_WIDTH), BF16)],
        compiler_params=_cparams(("arbitrary",)),
        name="inproj",
    )(h, g, w)


def _fourier_body(x_ref, cs_ref, ss_ref, cc_ref, sc_ref, wb_ref, g_ref, o_ref):
    x = x_ref[0].astype(BF16)
    y = _dot(cs_ref[...], x).astype(BF16)
    z = _dot(ss_ref[...], x).astype(BF16)
    f = _dot(y, cc_ref[...]) - _dot(z, sc_ref[...])
    o = _dot(f.astype(BF16), wb_ref[...])
    o_ref[0] = _rms(o, g_ref[...]).astype(BF16)


def _dft_tables(n):
    k = (jnp.arange(n, dtype=I32)[:, None] * jnp.arange(n, dtype=I32)[None, :]) % n
    ang = k.astype(F32) * (2.0 * math.pi / n)
    return jnp.cos(ang), jnp.sin(ang)


def _fourier(zf, cs, ss, ccb, scb, wb, g, b, s):
    tr = 512
    x = zf.reshape(b, s, FN_WIDTH)
    out = pl.pallas_call(
        _fourier_body,
        grid=(s // tr, b),
        in_specs=[pl.BlockSpec((1, s, FN_WIDTH), lambda i, j: (j, 0, 0)),
                  pl.BlockSpec((tr, s), lambda i, j: (i, 0)),
                  pl.BlockSpec((tr, s), lambda i, j: (i, 0)),
                  pl.BlockSpec((FN_WIDTH, FN_WIDTH), lambda i, j: (0, 0)),
                  pl.BlockSpec((FN_WIDTH, FN_WIDTH), lambda i, j: (0, 0)),
                  pl.BlockSpec((FN_WIDTH, FN_WIDTH), lambda i, j: (0, 0)),
                  pl.BlockSpec((1, FN_WIDTH), lambda i, j: (0, 0))],
        out_specs=pl.BlockSpec((1, tr, FN_WIDTH), lambda i, j: (j, i, 0)),
        out_shape=jax.ShapeDtypeStruct((b, s, FN_WIDTH), BF16),
        compiler_params=_cparams(("arbitrary", "arbitrary")),
        name="fourier",
    )(x, cs, ss, ccb, scb, wb, g)
    return out.reshape(b * s, FN_WIDTH)


def _block_diag(blocks):
    h, a, bb = blocks.shape
    eye = jnp.eye(h, dtype=blocks.dtype)
    return jnp.einsum('hab,hg->hagb', blocks, eye).reshape(h * a, h * bb)


def _s5_matrices(a_re, a_im, log_dt, b_re, b_im, c_re, c_im):
    t = SSM_CHUNK
    g, p, c = SSM_GROUPS, SSM_STATE, SSM_GROUP
    lam = lax.complex(a_re.astype(F32), a_im.astype(F32))
    dt = jnp.exp(log_dt.astype(F32))[..., None]
    abar = jnp.exp(lam * dt)
    bbar = ((abar - 1.0) / lam)[..., None] * lax.complex(b_re.astype(F32), b_im.astype(F32))
    cmat = lax.complex(c_re.astype(F32), c_im.astype(F32))
    ks = jnp.arange(t + 1, dtype=F32)
    apow = jnp.exp((lam * dt)[:, None] * ks[None, :, None, None])
    eye = jnp.eye(g, dtype=F32)

    kern = jnp.real(jnp.einsum('dgcp,dkgp,dgpe->dkgce', cmat, apow[:, :t], bbar))
    lags = jnp.arange(-(t - 1), t)
    pick = lambda m: m[:, None, None, None]
    klag = (jnp.where(pick(lags >= 0), kern[0][jnp.clip(lags, 0, t - 1)], 0.0)
            + jnp.where(pick(lags <= 0), kern[1][jnp.clip(-lags, 0, t - 1)], 0.0))
    blocks = jnp.einsum('lgce,gh->lgehc', klag, eye).reshape(2 * t - 1, SSM_WIDTH, SSM_WIDTH)
    lag = jnp.arange(t)[None, :] - jnp.arange(t)[:, None]
    m_intra = blocks[lag + t - 1].transpose(0, 2, 1, 3).reshape(SSM_ROW, SSM_ROW)

    wf = apow[0, t - 1 - jnp.arange(t)][:, :, :, None] * bbar[0][None]
    wb = apow[1, jnp.arange(t)][:, :, :, None] * bbar[1][None]
    def _w_in(w):
        w6 = jnp.einsum('rgpe,gh->rgehp', w, eye.astype(w.dtype))
        return w6.reshape(SSM_ROW, SSM_NSTATE)
    wf, wb = _w_in(wf), _w_in(wb)
    w_in = jnp.concatenate([jnp.real(wf), jnp.imag(wf), jnp.real(wb), jnp.imag(wb)], axis=1)

    qf = cmat[0][None] * apow[0, 1 + jnp.arange(t)][:, :, None, :]
    qb = cmat[1][None] * apow[1, t - jnp.arange(t)][:, :, None, :]
    def _w_out(q):
        q6 = jnp.einsum('rgcp,gh->gprhc', q, eye.astype(q.dtype))
        return q6.reshape(SSM_NSTATE, SSM_ROW)
    qf, qb = _w_out(qf), _w_out(qb)
    w_out = jnp.concatenate([jnp.real(qf), -jnp.imag(qf), jnp.real(qb), -jnp.imag(qb)], axis=0)

    a_chunk = lam * dt * t
    return m_intra.astype(BF16), w_in.astype(BF16), w_out.astype(BF16), a_chunk


def _scan_powers(a_chunk, nsteps):
    e = jnp.exp(a_chunk[None] * (2.0 ** jnp.arange(nsteps, dtype=F32))[:, None, None, None])
    e = e.reshape(nsteps, 2 * SSM_NSTATE)
    return jnp.real(e), jnp.imag(e)


def _chunk_steps(za_ref, zb_ref, tr):
    return [jnp.concatenate([za_ref[pl.ds(r, tr, stride=SSM_CHUNK), :],
                             zb_ref[pl.ds(r, tr, stride=SSM_CHUNK), :]], axis=1) for r in range(SSM_CHUNK)]


def _halves(rows):
    return [pl.BlockSpec((rows, LANES), lambda i: (i, 0)), pl.BlockSpec((rows, LANES), lambda i: (i, 1))]


def _s5_in_body(za_ref, zb_ref, w_ref, o_ref):
    tr = o_ref.shape[0]
    acc = jnp.zeros(o_ref.shape, F32)
    for r, u in enumerate(_chunk_steps(za_ref, zb_ref, tr)):
        acc = acc + _dot(u.astype(BF16), w_ref[r * SSM_WIDTH:(r + 1) * SSM_WIDTH, :])
    o_ref[...] = acc


def _s5_in(zs, w_in):
    rows = zs.shape[0] // SSM_CHUNK
    tr = min(S5_ROWS, rows)
    width = 4 * SSM_NSTATE
    return pl.pallas_call(
        _s5_in_body,
        grid=(rows // tr,),
        in_specs=_halves(tr * SSM_CHUNK) + [
            pl.BlockSpec((SSM_ROW, width), lambda i: (0, 0), pipeline_mode=pl.Buffered(1))],
        out_specs=pl.BlockSpec((tr, width), lambda i: (i, 0)),
        out_shape=jax.ShapeDtypeStruct((rows, width), F32),
        compiler_params=_cparams(("arbitrary",)),
        name="s5_in",
    )(zs, zs, w_in)


def _s5_scan_body(x_ref, pr_ref, pi_ref, o_ref, *, nc, nsteps):
    ns = SSM_NSTATE
    x = x_ref[0]
    fre, fim = x[:, 0:ns], x[:, ns:2 * ns]
    bre, bim = x[:, 2 * ns:3 * ns], x[:, 3 * ns:4 * ns]
    row = lax.broadcasted_iota(I32, (nc, 1), 0)
    for k in range(nsteps):
        sh = 2 ** k
        far, fai = pr_ref[k:k + 1, 0:ns], pi_ref[k:k + 1, 0:ns]
        bar, bai = pr_ref[k:k + 1, ns:2 * ns], pi_ref[k:k + 1, ns:2 * ns]
        fmask = row >= sh
        sre = jnp.where(fmask, pltpu.roll(fre, sh, 0), 0.0)
        sim = jnp.where(fmask, pltpu.roll(fim, sh, 0), 0.0)
        fre, fim = fre + far * sre - fai * sim, fim + far * sim + fai * sre
        bmask = row < nc - sh
        sre = jnp.where(bmask, pltpu.roll(bre, nc - sh, 0), 0.0)
        sim = jnp.where(bmask, pltpu.roll(bim, nc - sh, 0), 0.0)
        bre, bim = bre + bar * sre - bai * sim, bim + bar * sim + bai * sre
    fmask = row >= 1
    bmask = row < nc - 1
    o_ref[0, :, 0:ns] = jnp.where(fmask, pltpu.roll(fre, 1, 0), 0.0).astype(BF16)
    o_ref[0, :, ns:2 * ns] = jnp.where(fmask, pltpu.roll(fim, 1, 0), 0.0).astype(BF16)
    o_ref[0, :, 2 * ns:3 * ns] = jnp.where(bmask, pltpu.roll(bre, nc - 1, 0), 0.0).astype(BF16)
    o_ref[0, :, 3 * ns:4 * ns] = jnp.where(bmask, pltpu.roll(bim, nc - 1, 0), 0.0).astype(BF16)


def _s5_scan(xl, pr, pi, b, nc):
    nsteps = int(math.log2(nc))
    width = 4 * SSM_NSTATE
    x = xl.reshape(b, nc, width)
    out = pl.pallas_call(
        functools.partial(_s5_scan_body, nc=nc, nsteps=nsteps),
        grid=(b,),
        in_specs=[pl.BlockSpec((1, nc, width), lambda i: (i, 0, 0)),
                  pl.BlockSpec((nsteps, 2 * SSM_NSTATE), lambda i: (0, 0)),
                  pl.BlockSpec((nsteps, 2 * SSM_NSTATE), lambda i: (0, 0))],
        out_specs=pl.BlockSpec((1, nc, width), lambda i: (i, 0, 0)),
        out_shape=jax.ShapeDtypeStruct((b, nc, width), BF16),
        compiler_params=_cparams(("arbitrary",)),
        name="s5_scan",
    )(x, pr, pi)
    return out.reshape(b * nc, width)


def _s5_out_body(za_ref, zb_ref, s_ref, m_ref, w_ref, d_ref, gw_ref, gb_ref, g_ref, o_ref, nat_a, nat_b):
    tr = s_ref.shape[0]
    steps = _chunk_steps(za_ref, zb_ref, tr)
    y = _dot(s_ref[...], w_ref[...])
    for r, u in enumerate(steps):
        y = y + _dot(u.astype(BF16), m_ref[r * SSM_WIDTH:(r + 1) * SSM_WIDTH, :])
    c0 = math.sqrt(2.0 / math.pi)
    for r, u in enumerate(steps):
        v = y[:, r * SSM_WIDTH:(r + 1) * SSM_WIDTH] + d_ref[...] * u
        gl = 0.5 * v * (1.0 + jnp.tanh(c0 * (v + 0.044715 * (v * v * v))))
        gate = jax.nn.sigmoid(_dot(gl.astype(BF16), gw_ref[...]) + gb_ref[...])
        out = _rms(gl * gate, g_ref[...])
        nat_a[pl.ds(r, tr, stride=SSM_CHUNK), :] = out[:, :LANES]
        nat_b[pl.ds(r, tr, stride=SSM_CHUNK), :] = out[:, LANES:]
    o_ref[:, :LANES] = nat_a[...].astype(BF16)
    o_ref[:, LANES:] = nat_b[...].astype(BF16)


def _s5_out(zs, states, m_intra, w_out, d, glu_w, glu_b, g):
    rows = states.shape[0]
    tr = min(S5_ROWS, rows)
    width = 4 * SSM_NSTATE
    fix = lambda i: (0, 0)
    once = pl.Buffered(1)
    return pl.pallas_call(
        _s5_out_body,
        grid=(rows // tr,),
        in_specs=_halves(tr * SSM_CHUNK) + [
                  pl.BlockSpec((tr, width), lambda i: (i, 0)),
                  pl.BlockSpec((SSM_ROW, SSM_ROW), fix, pipeline_mode=once),
                  pl.BlockSpec((width, SSM_ROW), fix, pipeline_mode=once),
                  pl.BlockSpec((1, SSM_WIDTH), fix), pl.BlockSpec((SSM_WIDTH, SSM_WIDTH), fix),
                  pl.BlockSpec((1, SSM_WIDTH), fix), pl.BlockSpec((1, SSM_WIDTH), fix)],
        out_specs=pl.BlockSpec((tr * SSM_CHUNK, SSM_WIDTH), lambda i: (i, 0)),
        out_shape=jax.ShapeDtypeStruct((rows * SSM_CHUNK, SSM_WIDTH), BF16),
        scratch_shapes=[pltpu.VMEM((tr * SSM_CHUNK, LANES), F32) for _ in range(SSM_WIDTH // LANES)],
        compiler_params=_cparams(("arbitrary",)),
        name="s5_out",
    )(zs, zs, states, m_intra, w_out, d, glu_w, glu_b, g)


def _attn_geometry(s, d):
    ln = s // d
    bq = min(128, ln)
    bk = min(bq + 2 * ATT_HALF, ln)
    return ln, bq, bk, ln // bq


def _attn_body(slope_ref, q_ref, k_ref, v_ref, o_ref, nat, qd, kd, vd, qdb, kdb, vdb, b0, b1, b2, *acc, s):
    hp = pl.program_id(1)
    s4 = s // ATT_SPLIT
    lane = lax.broadcasted_iota(I32, (1, LANES), 1)
    first = lane < ATT_HEAD_DIM
    second = jnp.logical_not(first)
    slopes = (slope_ref[2 * hp], slope_ref[2 * hp + 1])
    log2e = math.log2(math.e)
    scale = ATT_HEAD_DIM ** -0.5 * log2e

    for (_, d), bias in zip(DILATED_PATTERNS, (b0, b1, b2)):
        _, bq, bk, _ = _attn_geometry(s, d)
        jk = lax.broadcasted_iota(I32, (1, bk), 1)
        for o in range(ATT_OFFSETS):
            rel = jnp.abs(lax.broadcasted_iota(I32, (bq, 1), 0) + o * ATT_HALF - jk)
            dist = (d * rel).astype(F32) * log2e
            for hh in range(2):
                bias[2 * o + hh] = jnp.where(rel <= ATT_HALF, -slopes[hh] * dist, NEG_INF)

    for src, dst_f, dst_b in ((q_ref, qd, qdb), (k_ref, kd, kdb), (v_ref, vd, vdb)):
        nat[...] = src[...].astype(F32)
        for c in range(ATT_SPLIT):
            x = nat[pl.ds(c, s4, stride=ATT_SPLIT), :]
            dst_f[c * s4:(c + 1) * s4, :] = x
            dst_b[c * s4:(c + 1) * s4, :] = x.astype(BF16)

    def pattern(p, d, refs, bias, locate, stride):
        acc_o, acc_m, acc_l = acc[3 * p], acc[3 * p + 1], acc[3 * p + 2]
        ln, bq, bk, nqb = _attn_geometry(s, d)

        def rows(c, j, size):
            start = locate(c, j)
            if stride == 1:
                return pl.ds(pl.multiple_of(start, ATT_HALF), size)
            return pl.ds(start, size, stride=stride)

        def scores(i):
            c = i // nqb
            j0 = (i % nqb) * bq
            ks = jnp.clip(j0 - ATT_HALF, 0, ln - bk)
            q = refs[0][rows(c, j0, bq), :].astype(BF16)
            k = refs[1][rows(c, ks, bk), :].astype(BF16)
            v = refs[2][rows(c, ks, bk), :].astype(BF16)
            off = (j0 - ks) // ATT_HALF
            scs = []
            for hh in range(2):
                qm = jnp.where(first if hh == 0 else second, q, jnp.zeros_like(q))
                scs.append(_dot_nt(qm, k) * scale + bias[2 * off + hh])
            return rows(c, j0, bq), v, scs

        def softmax(sc):
            m = jnp.max(sc, axis=-1, keepdims=True)
            pe = jnp.exp2(sc - m)
            return pe.astype(BF16), m, jnp.sum(pe, axis=-1, keepdims=True)

        def group(g, carry):
            staged = [scores(g * ATT_UNROLL + u) for u in range(ATT_UNROLL)]
            soft = [[softmax(sc) for sc in scs] for _, _, scs in staged]
            for (dst, v, _), ((p0, m0, l0), (p1, m1, l1)) in zip(staged, soft):
                acc_o[dst, :] = jnp.where(first, _dot(p0, v), _dot(p1, v))
                acc_m[dst, :] = jnp.where(first, m0, m1)
                acc_l[dst, :] = jnp.where(first, l0, l1)
            return carry

        lax.fori_loop(0, d * nqb // ATT_UNROLL, group, 0)

    (_, d1), (_, d2), (_, d3) = DILATED_PATTERNS
    assert d1 == 1 and d2 == ATT_SPLIT and d3 == ATT_SPLIT * ATT_SPLIT
    pattern(0, d1, (q_ref, k_ref, v_ref), b0, lambda c, j: j, 1)
    pattern(1, d2, (qdb, kdb, vdb), b1, lambda c, j: c * s4 + j, 1)
    pattern(2, d3, (qd, kd, vd), b2,
            lambda c, j: (c % ATT_SPLIT) * s4 + c // ATT_SPLIT + ATT_SPLIT * j, ATT_SPLIT)

    for c in range(ATT_SPLIT):
        part = pl.ds(c, s4, stride=ATT_SPLIT)
        blk = slice(c * s4, (c + 1) * s4)
        ms = (acc[1][part, :], acc[4][blk, :], acc[7][blk, :])
        os_ = (acc[0][part, :], acc[3][blk, :], acc[6][blk, :])
        ls = (acc[2][part, :], acc[5][blk, :], acc[8][blk, :])
        m = jnp.maximum(jnp.maximum(ms[0], ms[1]), ms[2])
        num = jnp.zeros((s4, LANES), F32)
        den = jnp.zeros((s4, LANES), F32)
        for p in range(3):
            w = jnp.exp2(ms[p] - m)
            num = num + w * os_[p]
            den = den + w * ls[p]
        nat[part, :] = num / den
    o_ref[...] = nat[...].astype(BF16)


def _attn(zqkv, slopes, b, s):
    n = b * s
    nhp = ATT_HEADS // 2
    col = lambda off: (lambda i, j, sl: (i, off + j))
    return pl.pallas_call(
        functools.partial(_attn_body, s=s),
        grid_spec=pltpu.PrefetchScalarGridSpec(
            num_scalar_prefetch=1,
            grid=(b, nhp),
            in_specs=[pl.BlockSpec((s, LANES), col(0)),
                      pl.BlockSpec((s, LANES), col(nhp)),
                      pl.BlockSpec((s, LANES), col(2 * nhp))],
            out_specs=pl.BlockSpec((s, LANES), lambda i, j, sl: (i, j)),
            scratch_shapes=([pltpu.VMEM((s, LANES), F32) for _ in range(4)]
                            + [pltpu.VMEM((s, LANES), BF16) for _ in range(3)]
                            + [pltpu.VMEM((2 * ATT_OFFSETS,) + _attn_geometry(s, d)[1:3], F32)
                               for _, d in DILATED_PATTERNS]
                            + [pltpu.VMEM((s, LANES), F32) for _ in range(9)]),
        ),
        out_shape=jax.ShapeDtypeStruct((n, ATT_WIDTH), BF16),
        compiler_params=_cparams(("arbitrary", "arbitrary")),
        name="attn",
    )(slopes, zqkv, zqkv, zqkv)


def _outproj_body(h_ref, mf_ref, ms_ref, oa_ref, ga_ref, w_ref, gf_ref, wr_ref,
                  h1_ref, m_ref, aff_ref):
    oa = _rms(oa_ref[...].astype(F32), ga_ref[...]).astype(BF16)
    o1, o2 = FN_WIDTH, FN_WIDTH + SSM_WIDTH
    acc = h_ref[...] + _dot(mf_ref[...], w_ref[0:o1, :]) + _dot(ms_ref[...], w_ref[o1:o2, :])
    acc = acc + _dot(oa, w_ref[o2:, :])
    h1_ref[...] = acc
    m = _rms(acc, gf_ref[...]).astype(BF16)
    bits = pltpu.bitcast(m.astype(F32), I32)
    m_ref[...] = lax.shift_right_logical(bits[:, :PACKED], 16) | (bits[:, PACKED:] & jnp.int32(-65536))
    lg = _dot_nt(wr_ref[...], m)
    e = jnp.exp(lg - jnp.max(lg, axis=0, keepdims=True))
    aff_ref[...] = e / jnp.sum(e, axis=0, keepdims=True)


def _outproj(h, mf, ms, oa, ga, w, gf, wr_t):
    n = h.shape[0]
    tm = 512
    row = lambda i: (i, 0)
    fix = lambda i: (0, 0)
    return pl.pallas_call(
        _outproj_body,
        grid=(n // tm,),
        in_specs=[pl.BlockSpec((tm, D_MODEL), row), pl.BlockSpec((tm, FN_WIDTH), row),
                  pl.BlockSpec((tm, SSM_WIDTH), row), pl.BlockSpec((tm, ATT_WIDTH), row),
                  pl.BlockSpec((1, ATT_WIDTH), fix), pl.BlockSpec((D_MODEL, D_MODEL), fix),
                  pl.BlockSpec((1, D_MODEL), fix), pl.BlockSpec((N_EXPERTS, D_MODEL), fix)],
        out_specs=[pl.BlockSpec((tm, D_MODEL), row), pl.BlockSpec((tm, PACKED), row),
                   pl.BlockSpec((N_EXPERTS, tm), lambda i: (0, i))],
        out_shape=[jax.ShapeDtypeStruct((n, D_MODEL), F32),
                   jax.ShapeDtypeStruct((n, PACKED), I32),
                   jax.ShapeDtypeStruct((N_EXPERTS, n), F32)],
        compiler_params=_cparams(("arbitrary",)),
        name="outproj",
    )(h, mf, ms, oa, ga, w, gf, wr_t)


def _select_body(aff_ref, rank_ref, off_ref, *, cap, nb):
    r128 = lax.broadcasted_iota(I32, (LANES, LANES), 0)
    c128 = lax.broadcasted_iota(I32, (LANES, LANES), 1)
    upper_incl = jnp.where(r128 <= c128, 1.0, 0.0).astype(BF16)
    ones = jnp.ones((LANES, LANES), BF16)
    rb = lax.broadcasted_iota(I32, (nb, nb), 0)
    cb = lax.broadcasted_iota(I32, (nb, nb), 1)
    lower_strict = jnp.where(cb < rb, 1.0, 0.0).astype(BF16)
    upper_strict = jnp.where(rb < cb, 1.0, 0.0).astype(BF16)
    ones8 = jnp.ones((8, LANES), BF16)

    def count(mask):
        c = jnp.sum(jnp.where(mask, 1.0, 0.0), axis=1, keepdims=True)
        return jnp.sum(c, axis=0, keepdims=True)

    def prefix(mask):
        mb = jnp.where(mask, 1.0, 0.0).astype(BF16)
        incl = _dot(mb, upper_incl)
        tot = _dot(mb, ones)
        offs = _dot(lower_strict, tot.astype(BF16))
        return offs + incl - 1.0, mb

    def per_expert(e, carry):
        bits = pltpu.bitcast(aff_ref[e], I32)

        def bitstep(i, t):
            cand = t | jnp.left_shift(jnp.int32(1), 30 - i)
            return jnp.where(count(bits >= cand) >= cap, cand, t)

        t = lax.fori_loop(0, 31, bitstep, jnp.zeros((1, 1), I32))
        gt = bits > t
        eq = bits == t
        need = cap - count(gt)
        eq_rank, _ = prefix(eq)
        sel = jnp.logical_or(gt, jnp.logical_and(eq, eq_rank < need))
        rank, mb = prefix(sel)
        rank_ref[e] = jnp.where(sel, rank.astype(I32), -1)
        tot_row = _dot_nt(ones8, mb)
        off_row = _dot(tot_row.astype(BF16), upper_strict)
        off_ref[pl.ds(e, 1), :] = off_row[0:1, :].astype(I32)
        return carry

    lax.fori_loop(0, N_EXPERTS, per_expert, 0)


def _select(aff_t, cap):
    n = aff_t.shape[1]
    nb = n // LANES
    a3 = aff_t.reshape(N_EXPERTS, nb, LANES)
    rank, off = pl.pallas_call(
        functools.partial(_select_body, cap=cap, nb=nb),
        grid=(1,),
        in_specs=[pl.BlockSpec((N_EXPERTS, nb, LANES), lambda i: (0, 0, 0))],
        out_specs=[pl.BlockSpec((N_EXPERTS, nb, LANES), lambda i: (0, 0, 0)),
                   pl.BlockSpec((N_EXPERTS, nb), lambda i: (0, 0))],
        out_shape=[jax.ShapeDtypeStruct((N_EXPERTS, nb, LANES), I32),
                   jax.ShapeDtypeStruct((N_EXPERTS, nb), I32)],
        compiler_params=_cparams(("arbitrary",)),
        name="select",
    )(a3)
    return rank.reshape(N_EXPERTS, n), off


def _ffn_body(x_ref, wg_ref, wu_ref, wd_ref, o_ref):
    w = x_ref[...]
    x = jnp.concatenate([pltpu.bitcast(lax.shift_left(w, 16), F32),
                         pltpu.bitcast(w & jnp.int32(-65536), F32)], axis=1).astype(BF16)
    tf = 512
    acc = jnp.zeros(o_ref.shape, F32)
    for j in range(D_FF_EXPERT // tf):
        g = _dot(x, wg_ref[:, j * tf:(j + 1) * tf])
        u = _dot(x, wu_ref[:, j * tf:(j + 1) * tf])
        hdn = (g * jax.nn.sigmoid(g) * u).astype(BF16)
        acc = acc + _dot(hdn, wd_ref[j * tf:(j + 1) * tf, :])
    o_ref[...] = acc.astype(BF16)


def _ffn(xe, wg, wu, wd, layer):
    e, cap, _ = xe.shape
    tm = min(512, cap)
    return pl.pallas_call(
        _ffn_body,
        grid=(e, cap // tm),
        in_specs=[pl.BlockSpec((None, tm, PACKED), lambda i, j: (i, j, 0)),
                  pl.BlockSpec((None, None, D_MODEL, D_FF_EXPERT), lambda i, j: (layer, i, 0, 0)),
                  pl.BlockSpec((None, None, D_MODEL, D_FF_EXPERT), lambda i, j: (layer, i, 0, 0)),
                  pl.BlockSpec((None, None, D_FF_EXPERT, D_MODEL), lambda i, j: (layer, i, 0, 0))],
        out_specs=pl.BlockSpec((None, tm, D_MODEL), lambda i, j: (i, j, 0)),
        out_shape=jax.ShapeDtypeStruct((e, cap, D_MODEL), BF16),
        compiler_params=_cparams(("arbitrary", "arbitrary")),
        name="ffn",
    )(xe, wg, wu, wd)


def _combine_body(off_ref, h_ref, rank_ref, aff_ref, p_ref, gp_ref, wg_ref, wp_ref, gfin_ref, ye_ref,
                  o_ref, stack, sems, *, cap, last):
    t = pl.program_id(0)
    nt = pl.num_programs(0)
    tile, win = COMBINE_TILE, COMBINE_WIN
    bpt = tile // LANES
    kdim = N_EXPERTS * win

    def base(tt, e):
        return (off_ref[e, tt * bpt] // 16) * 16

    def copy(e, start, slot):
        return pltpu.make_async_copy(ye_ref.at[e, pl.ds(pl.multiple_of(start, 16), win), :],
                                     stack.at[slot, pl.ds(e * win, win), :], sems.at[slot, e])

    def starts(tt, r):
        want = [base(tt, e) + r * win for e in range(N_EXPERTS)]
        return want, [jnp.minimum(w, cap - win) for w in want]

    def issue(tt, r, slot):
        _, got = starts(tt, r)
        for e in range(N_EXPERTS):
            copy(e, got[e], slot).start()

    def wait(tt, r, slot):
        _, got = starts(tt, r)
        for e in range(N_EXPERTS):
            copy(e, got[e], slot).wait()

    slot = t % 2

    @pl.when(t == 0)
    def _():
        issue(0, 0, 0)

    @pl.when(t + 1 < nt)
    def _():
        issue(t + 1, 0, 1 - slot)

    lane16 = lax.broadcasted_iota(I32, (1, N_EXPERTS), 1)
    er = lax.broadcasted_iota(I32, (N_EXPERTS, kdim), 0)
    ec = lax.broadcasted_iota(I32, (N_EXPERTS, kdim), 1)
    expand = jnp.where(ec // win == er, 1.0, 0.0).astype(BF16)
    lane_in = (lax.broadcasted_iota(I32, (1, kdim), 1) % win).astype(F32)

    aff = aff_ref[...]
    gates = _dot(aff.astype(BF16), expand)
    rk = rank_ref[...]

    def row_of(vals):
        r = jnp.zeros((1, N_EXPERTS), I32)
        for e in range(N_EXPERTS):
            r = jnp.where(lane16 == e, vals[e], r)
        return r

    def accumulate(r, slot_):
        want, got = starts(t, r)
        wrow, grow = row_of(want), row_of(got)
        ok = jnp.logical_and(rk >= wrow, rk < grow + win)
        rel = jnp.where(ok, rk - grow, -1).astype(F32).astype(BF16)
        hit = _dot(rel, expand) == lane_in
        w = jnp.where(hit, gates, 0.0).astype(BF16)
        o_ref[...] += _dot(w, stack[slot_])

    o_ref[...] = h_ref[...]
    wait(t, 0, slot)
    accumulate(0, slot)

    nr = jnp.int32(1)
    for e in range(N_EXPERTS):
        span = off_ref[e, (t + 1) * bpt] - base(t, e)
        nr = jnp.maximum(nr, (span + win - 1) // win)

    def extra(r, carry):
        issue(t, r, slot)
        wait(t, r, slot)
        accumulate(r, slot)
        return carry

    lax.fori_loop(1, nr, extra, 0)

    h2 = o_ref[...]
    gate = jax.nn.sigmoid(_dot(_rms(h2, gp_ref[...]).astype(BF16), wg_ref[...]))
    h3 = h2 + _dot(p_ref[...].astype(BF16), wp_ref[...]) * gate
    o_ref[...] = _rms(h3, gfin_ref[...]) if last else h3


def _combine(off, h1, rank_tok, aff_tok, p, layer, gp, wg, wp, gfin, ye, cap, last):
    n = h1.shape[0]
    tile = COMBINE_TILE
    row = lambda i, o: (i, 0)
    fix = lambda i, o: (0, 0)
    prow = lambda i, o: (layer * (n // tile) + i, 0)
    return pl.pallas_call(
        functools.partial(_combine_body, cap=cap, last=last),
        grid_spec=pltpu.PrefetchScalarGridSpec(
            num_scalar_prefetch=1,
            grid=(n // tile,),
            in_specs=[pl.BlockSpec((tile, D_MODEL), row),
                      pl.BlockSpec((tile, N_EXPERTS), row),
                      pl.BlockSpec((tile, N_EXPERTS), row),
                      pl.BlockSpec((tile, PLE_DIM), prow),
                      pl.BlockSpec((1, D_MODEL), fix),
                      pl.BlockSpec((D_MODEL, D_MODEL), fix),
                      pl.BlockSpec((PLE_DIM, D_MODEL), fix),
                      pl.BlockSpec((1, D_MODEL), fix),
                      pl.BlockSpec(memory_space=pl.ANY)],
            out_specs=pl.BlockSpec((tile, D_MODEL), row),
            scratch_shapes=[pltpu.VMEM((2, N_EXPERTS * COMBINE_WIN, D_MODEL), BF16),
                            pltpu.SemaphoreType.DMA((2, N_EXPERTS))],
        ),
        out_shape=jax.ShapeDtypeStruct((n, D_MODEL), F32),
        compiler_params=_cparams(("arbitrary",)),
        name="combine",
    )(off, h1, rank_tok, aff_tok, p, gp, wg, wp, gfin, ye)


def _gather_rows(table, rank, cap):
    n, w = table.shape
    workers_per_expert = SC_CORES * SC_SUBCORES // N_EXPERTS
    per = cap // workers_per_expert
    nch = per // GATHER_ROWS
    mesh = plsc.VectorSubcoreMesh(core_axis_name="c", subcore_axis_name="s",
                                  num_cores=SC_CORES, num_subcores=SC_SUBCORES)
    cp = dataclasses.replace(pltpu.CompilerParams(), needs_layout_passes=False)

    @functools.partial(
        pl.kernel, mesh=mesh, compiler_params=cp,
        out_type=jax.ShapeDtypeStruct((N_EXPERTS * cap, w), I32),
        scratch_types=[pltpu.VMEM((RANK_CHUNK,), I32),
                       pltpu.VMEM((nch, GATHER_ROWS), I32),
                       pltpu.VMEM((GATHER_ROWS, w), I32),
                       pltpu.SemaphoreType.DMA],
        name="sc_gather")
    def gather(table_hbm, rank_hbm, out_hbm, rbuf, idx, rows, sem):
        wid = lax.axis_index("s") * SC_CORES + lax.axis_index("c")
        e = wid // workers_per_expert
        lo = (wid % workers_per_expert) * per
        lane = lax.iota(I32, SC_LANES)

        @pl.loop(0, n // RANK_CHUNK)
        def _(c):
            pltpu.sync_copy(rank_hbm.at[e, pl.ds(c * RANK_CHUNK, RANK_CHUNK)], rbuf)

            @pl.loop(0, RANK_CHUNK // SC_LANES)
            def _(i):
                rel = rbuf[pl.ds(i * SC_LANES, SC_LANES)] - lo
                mask = jnp.logical_and(rel >= 0, rel < per)
                rel = jnp.where(mask, rel, 0)
                tok = c * RANK_CHUNK + i * SC_LANES + lane
                plsc.store_scatter(idx, [rel // GATHER_ROWS, rel % GATHER_ROWS], tok, mask=mask)

        @pl.loop(0, nch)
        def _(c):
            pltpu.async_copy(table_hbm.at[idx.at[c]], rows, sem).wait()
            pltpu.sync_copy(rows, out_hbm.at[pl.ds(e * cap + lo + c * GATHER_ROWS, GATHER_ROWS)])

    return gather(table, rank).reshape(N_EXPERTS, cap, w)


def _trunk(x, p, prm, b, s):
    n = b * s
    nc = s // SSM_CHUNK
    cap = EC_CAPACITY_FACTOR * n // N_EXPERTS
    cs, ss = _dft_tables(s)
    cs, ss = cs.astype(BF16), ss.astype(BF16)
    h = x.reshape(n, D_MODEL)
    depth = prm['w_in'].shape[0]
    p_rows = p.reshape(depth * n, PLE_DIM)
    for l in range(depth):
        lp = prm['layers'][l]
        zf, zs, zqkv = _inproj(h, lp['g_mix'], lp['w_in'])
        mf = _fourier(zf, cs, ss, prm['ccb'], prm['scb'], lp['w_fnet'], lp['g_f'], b, s)
        xl = _s5_in(zs, lp['s5_w_in'])
        pr, pi = _scan_powers(lp['s5_a_chunk'], int(math.log2(nc)))
        st = _s5_scan(xl, pr, pi, b, nc)
        ms = _s5_out(zs, st, lp['s5_m_intra'], lp['s5_w_out'], lp['s5_d'], lp['s5_glu_w'], lp['s5_glu_b'],
                     lp['g_s'])
        oa = _attn(zqkv, prm['slopes'], b, s)
        h1, m, aff_t = _outproj(h, mf, ms, oa, lp['g_a'], lp['w_out'], lp['g_ffn'], lp['w_router_t'])
        rank, off = _select(aff_t, cap)
        off = jnp.concatenate([off, jnp.full((N_EXPERTS, 1), cap, I32)], axis=1)
        xe = _gather_rows(m, rank, cap)
        ye = _ffn(xe, prm['w_gate'], prm['w_up'], prm['w_down'], l)
        h = _combine(off, h1, rank.T, aff_t.T, p_rows, l, lp['g_ple'], lp['w_ple_gate'],
                     lp['w_ple_proj'], prm['g_final'], ye, cap, l == depth - 1)
    return h.reshape(b, s, D_MODEL)


def kernel(x_prompt, x_sample, p_prompt, p_sample, norm_mix, w_in, w_fnet, ssm_a_re, ssm_a_im, ssm_log_dt, ssm_b_re, ssm_b_im, ssm_c_re, ssm_c_im, ssm_d, ssm_glu_w, ssm_glu_b, norm_branch, w_out, norm_ffn, w_router, w_exp_gate, w_exp_up, w_exp_down, norm_ple, w_ple_gate, w_ple_proj, norm_final):
    depth = w_in.shape[0]
    o1, o2 = FN_WIDTH, FN_WIDTH + SSM_WIDTH
    row = lambda v: v.reshape(1, -1).astype(F32)
    cc, sc = _dft_tables(FN_HEAD_DIM)
    tile4 = lambda mtx: _block_diag(jnp.broadcast_to(mtx[None], (FN_HEADS,) + mtx.shape)).astype(BF16)
    layers = []
    for l in range(depth):
        m_intra, s5_w_in, s5_w_out, a_chunk = _s5_matrices(
            ssm_a_re[l], ssm_a_im[l], ssm_log_dt[l], ssm_b_re[l], ssm_b_im[l], ssm_c_re[l], ssm_c_im[l])
        layers.append(dict(
            g_mix=row(norm_mix[l]), w_in=w_in[l].astype(BF16),
            w_fnet=_block_diag(w_fnet[l]).astype(BF16),
            g_f=row(norm_branch[l][:o1]), g_s=row(norm_branch[l][o1:o2]), g_a=row(norm_branch[l][o2:]),
            s5_m_intra=m_intra, s5_w_in=s5_w_in, s5_w_out=s5_w_out, s5_a_chunk=a_chunk,
            s5_d=row(ssm_d[l]), s5_glu_w=ssm_glu_w[l].astype(BF16), s5_glu_b=row(ssm_glu_b[l]),
            w_out=w_out[l].astype(BF16), g_ffn=row(norm_ffn[l]),
            w_router_t=w_router[l].T.astype(BF16),
            g_ple=row(norm_ple[l]), w_ple_gate=w_ple_gate[l].astype(BF16),
            w_ple_proj=w_ple_proj[l].astype(BF16)))
    slopes = jnp.asarray([2.0 ** (-8.0 * (i + 1) / ATT_HEADS) for i in range(ATT_HEADS)], F32)
    prm = dict(w_in=w_in, layers=layers, ccb=tile4(cc), scb=tile4(sc), slopes=slopes,
               w_gate=w_exp_gate.astype(BF16), w_up=w_exp_up.astype(BF16), w_down=w_exp_down.astype(BF16),
               g_final=row(norm_final))
    bp, sp = x_prompt.shape[0], x_prompt.shape[1]
    bs, ssq = x_sample.shape[0], x_sample.shape[1]
    y_prompt = _trunk(x_prompt, p_prompt, prm, bp, sp)
    y_sample = _trunk(x_sample, p_sample, prm, bs, ssq)
    return (y_prompt, y_sample)
```

```python
import dataclasses
import functools
import math

import jax
import jax.numpy as jnp
from jax import lax
from jax.experimental import pallas as pl
from jax.experimental.pallas import tpu as pltpu
from jax.experimental.pallas import tpu_sc as plsc

D_MODEL = 1024
FN_WIDTH = 256
FN_HEADS = 4
FN_HEAD_DIM = 64
SSM_WIDTH = 256
SSM_GROUP = 16
SSM_GROUPS = 16
SSM_STATE = 64
ATT_WIDTH = 512
ATT_HEAD_DIM = 64
ATT_HEADS = 8
DILATED_PATTERNS = ((128, 1), (512, 4), (2048, 16))
IN_PROJ_WIDTH = 2048
N_EXPERTS = 16
EC_CAPACITY_FACTOR = 2
D_FF_EXPERT = 2048
PLE_DIM = 256
RMS_EPS = 1e-6
NEG_INF = -1e30

LANES = 128
SSM_CHUNK = 8
SSM_ROW = SSM_CHUNK * SSM_WIDTH
SSM_NSTATE = SSM_GROUPS * SSM_STATE
S5_ROWS = 256
ATT_HALF = 64
ATT_UNROLL = 8
ATT_SPLIT = 4
ATT_OFFSETS = 3
COMBINE_TILE = 256
COMBINE_SUB = 2
COMBINE_WIN = 64
VMEM_LIMIT = 56 * 1024 * 1024
SC_CORES = 2
SC_SUBCORES = 16
SC_LANES = 16
GATHER_ROWS = 64
RANK_CHUNK = 2048
PACKED = D_MODEL // 2

F32 = jnp.float32
BF16 = jnp.bfloat16
I32 = jnp.int32


def _cparams(sem):
    return pltpu.CompilerParams(dimension_semantics=sem, vmem_limit_bytes=VMEM_LIMIT)


def _rms(x, g):
    return x * lax.rsqrt(jnp.mean(x * x, axis=-1, keepdims=True) + RMS_EPS) * g


def _dot(a, b):
    return jnp.dot(a, b, preferred_element_type=F32)


def _dot_nt(a, b):
    return lax.dot_general(a, b, (((1,), (1,)), ((), ())), preferred_element_type=F32)


def _inproj_body(h_ref, g_ref, w_ref, zf_ref, zs_ref, zqkv_ref):
    a = _rms(h_ref[...], g_ref[...]).astype(BF16)
    z = _dot(a, w_ref[...])
    zf_ref[...] = z[:, :FN_WIDTH]
    zs_ref[...] = z[:, FN_WIDTH:FN_WIDTH + SSM_WIDTH]
    zqkv_ref[...] = z[:, FN_WIDTH + SSM_WIDTH:].astype(BF16)


def _inproj(h, g, w):
    n = h.shape[0]
    tm = 512
    return pl.pallas_call(
        _inproj_body,
        grid=(n // tm,),
        in_specs=[pl.BlockSpec((tm, D_MODEL), lambda i: (i, 0)),
                  pl.BlockSpec((1, D_MODEL), lambda i: (0, 0)),
                  pl.BlockSpec((D_MODEL, IN_PROJ_WIDTH), lambda i: (0, 0))],
        out_specs=[pl.BlockSpec((tm, FN_WIDTH), lambda i: (i, 0)),
                   pl.BlockSpec((tm, SSM_WIDTH), lambda i: (i, 0)),
                   pl.BlockSpec((tm, 3 * ATT_WIDTH), lambda i: (i, 0))],
        out_shape=[jax.ShapeDtypeStruct((n, FN_WIDTH), F32),
                   jax.ShapeDtypeStruct((n, SSM_WIDTH), F32),
                   jax.ShapeDtypeStruct((n, 3 * ATT_WIDTH), BF16)],
        compiler_params=_cparams(("arbitrary",)),
        name="inproj",
    )(h, g, w)


def _fourier_body(x_ref, cs_ref, ss_ref, cc_ref, sc_ref, wb_ref, g_ref, o_ref):
    x = x_ref[0].astype(BF16)
    y = _dot(cs_ref[...], x).astype(BF16)
    z = _dot(ss_ref[...], x).astype(BF16)
    f = _dot(y, cc_ref[...]) - _dot(z, sc_ref[...])
    o = _dot(f.astype(BF16), wb_ref[...])
    o_ref[0] = _rms(o, g_ref[...]).astype(BF16)


def _dft_tables(n):
    def exact(rows):
        k = (rows[:, None] * jnp.arange(n, dtype=I32)[None, :]) % n
        ang = k.astype(F32) * (2.0 * math.pi / n)
        return jnp.cos(ang), jnp.sin(ang)
    if n <= LANES:
        return exact(jnp.arange(n, dtype=I32))
    ca, sa = exact(LANES * jnp.arange(n // LANES, dtype=I32))
    cb, sb = exact(jnp.arange(LANES, dtype=I32))
    cos = ca[:, None, :] * cb[None] - sa[:, None, :] * sb[None]
    sin = sa[:, None, :] * cb[None] + ca[:, None, :] * sb[None]
    return cos.reshape(n, n), sin.reshape(n, n)


def _fourier(zf, cs, ss, ccb, scb, wb, g, b, s):
    tr = 512
    x = zf.reshape(b, s, FN_WIDTH)
    out = pl.pallas_call(
        _fourier_body,
        grid=(s // tr, b),
        in_specs=[pl.BlockSpec((1, s, FN_WIDTH), lambda i, j: (j, 0, 0)),
                  pl.BlockSpec((tr, s), lambda i, j: (i, 0)),
                  pl.BlockSpec((tr, s), lambda i, j: (i, 0)),
                  pl.BlockSpec((FN_WIDTH, FN_WIDTH), lambda i, j: (0, 0)),
                  pl.BlockSpec((FN_WIDTH, FN_WIDTH), lambda i, j: (0, 0)),
                  pl.BlockSpec((FN_WIDTH, FN_WIDTH), lambda i, j: (0, 0)),
                  pl.BlockSpec((1, FN_WIDTH), lambda i, j: (0, 0))],
        out_specs=pl.BlockSpec((1, tr, FN_WIDTH), lambda i, j: (j, i, 0)),
        out_shape=jax.ShapeDtypeStruct((b, s, FN_WIDTH), BF16),
        compiler_params=_cparams(("arbitrary", "arbitrary")),
        name="fourier",
    )(x, cs, ss, ccb, scb, wb, g)
    return out.reshape(b * s, FN_WIDTH)


def _block_diag(blocks):
    h, a, bb = blocks.shape
    eye = jnp.eye(h, dtype=blocks.dtype)
    return jnp.einsum('hab,hg->hagb', blocks, eye).reshape(h * a, h * bb)


def _s5_matrices(a_re, a_im, log_dt, b_re, b_im, c_re, c_im):
    t = SSM_CHUNK
    g, p, c = SSM_GROUPS, SSM_STATE, SSM_GROUP
    lam = lax.complex(a_re.astype(F32), a_im.astype(F32))
    dt = jnp.exp(log_dt.astype(F32))[..., None]
    abar = jnp.exp(lam * dt)
    bbar = ((abar - 1.0) / lam)[..., None] * lax.complex(b_re.astype(F32), b_im.astype(F32))
    cmat = lax.complex(c_re.astype(F32), c_im.astype(F32))
    ks = jnp.arange(t + 1, dtype=F32)
    apow = jnp.exp((lam * dt)[:, None] * ks[None, :, None, None])
    eye = jnp.eye(g, dtype=F32)

    kern = jnp.real(jnp.einsum('dgcp,dkgp,dgpe->dkgce', cmat, apow[:, :t], bbar))
    lags = jnp.arange(-(t - 1), t)
    pick = lambda m: m[:, None, None, None]
    klag = (jnp.where(pick(lags >= 0), kern[0][jnp.clip(lags, 0, t - 1)], 0.0)
            + jnp.where(pick(lags <= 0), kern[1][jnp.clip(-lags, 0, t - 1)], 0.0))
    blocks = jnp.einsum('lgce,gh->lgehc', klag, eye).reshape(2 * t - 1, SSM_WIDTH, SSM_WIDTH)
    lag = jnp.arange(t)[None, :] - jnp.arange(t)[:, None]
    m_intra = blocks[lag + t - 1].transpose(0, 2, 1, 3).reshape(SSM_ROW, SSM_ROW)

    wf = apow[0, t - 1 - jnp.arange(t)][:, :, :, None] * bbar[0][None]
    wb = apow[1, jnp.arange(t)][:, :, :, None] * bbar[1][None]
    def _w_in(w):
        w6 = jnp.einsum('rgpe,gh->rgehp', w, eye.astype(w.dtype))
        return w6.reshape(SSM_ROW, SSM_NSTATE)
    wf, wb = _w_in(wf), _w_in(wb)
    w_in = jnp.concatenate([jnp.real(wf), jnp.imag(wf), jnp.real(wb), jnp.imag(wb)], axis=1)

    qf = cmat[0][None] * apow[0, 1 + jnp.arange(t)][:, :, None, :]
    qb = cmat[1][None] * apow[1, t - jnp.arange(t)][:, :, None, :]
    def _w_out(q):
        q6 = jnp.einsum('rgcp,gh->gprhc', q, eye.astype(q.dtype))
        return q6.reshape(SSM_NSTATE, SSM_ROW)
    qf, qb = _w_out(qf), _w_out(qb)
    w_out = jnp.concatenate([jnp.real(qf), -jnp.imag(qf), jnp.real(qb), -jnp.imag(qb)], axis=0)

    a_chunk = lam * dt * t
    return m_intra.astype(BF16), w_in.astype(BF16), w_out.astype(BF16), a_chunk


def _scan_powers(a_chunk, nsteps):
    e = jnp.exp(a_chunk[None] * (2.0 ** jnp.arange(nsteps, dtype=F32))[:, None, None, None])
    e = e.reshape(nsteps, 2 * SSM_NSTATE)
    return jnp.real(e), jnp.imag(e)


def _chunk_steps(za_ref, zb_ref, tr):
    return [jnp.concatenate([za_ref[pl.ds(r, tr, stride=SSM_CHUNK), :],
                             zb_ref[pl.ds(r, tr, stride=SSM_CHUNK), :]], axis=1) for r in range(SSM_CHUNK)]


def _halves(rows):
    return [pl.BlockSpec((rows, LANES), lambda i: (i, 0)), pl.BlockSpec((rows, LANES), lambda i: (i, 1))]


def _s5_in_body(za_ref, zb_ref, w_ref, o_ref):
    tr = o_ref.shape[0]
    acc = jnp.zeros(o_ref.shape, F32)
    for r, u in enumerate(_chunk_steps(za_ref, zb_ref, tr)):
        acc = acc + _dot(u.astype(BF16), w_ref[r * SSM_WIDTH:(r + 1) * SSM_WIDTH, :])
    o_ref[...] = acc


def _s5_in(zs, w_in):
    rows = zs.shape[0] // SSM_CHUNK
    tr = min(S5_ROWS, rows)
    width = 4 * SSM_NSTATE
    return pl.pallas_call(
        _s5_in_body,
        grid=(rows // tr,),
        in_specs=_halves(tr * SSM_CHUNK) + [
            pl.BlockSpec((SSM_ROW, width), lambda i: (0, 0), pipeline_mode=pl.Buffered(1))],
        out_specs=pl.BlockSpec((tr, width), lambda i: (i, 0)),
        out_shape=jax.ShapeDtypeStruct((rows, width), F32),
        compiler_params=_cparams(("arbitrary",)),
        name="s5_in",
    )(zs, zs, w_in)


def _s5_scan_body(x_ref, pr_ref, pi_ref, o_ref, *, nc, nsteps):
    ns = SSM_NSTATE
    x = x_ref[0]
    fre, fim = x[:, 0:ns], x[:, ns:2 * ns]
    bre, bim = x[:, 2 * ns:3 * ns], x[:, 3 * ns:4 * ns]
    row = lax.broadcasted_iota(I32, (nc, 1), 0)
    for k in range(nsteps):
        sh = 2 ** k
        far, fai = pr_ref[k:k + 1, 0:ns], pi_ref[k:k + 1, 0:ns]
        bar, bai = pr_ref[k:k + 1, ns:2 * ns], pi_ref[k:k + 1, ns:2 * ns]
        fmask = row >= sh
        sre = jnp.where(fmask, pltpu.roll(fre, sh, 0), 0.0)
        sim = jnp.where(fmask, pltpu.roll(fim, sh, 0), 0.0)
        fre, fim = fre + far * sre - fai * sim, fim + far * sim + fai * sre
        bmask = row < nc - sh
        sre = jnp.where(bmask, pltpu.roll(bre, nc - sh, 0), 0.0)
        sim = jnp.where(bmask, pltpu.roll(bim, nc - sh, 0), 0.0)
        bre, bim = bre + bar * sre - bai * sim, bim + bar * sim + bai * sre
    fmask = row >= 1
    bmask = row < nc - 1
    o_ref[0, :, 0:ns] = jnp.where(fmask, pltpu.roll(fre, 1, 0), 0.0).astype(BF16)
    o_ref[0, :, ns:2 * ns] = jnp.where(fmask, pltpu.roll(fim, 1, 0), 0.0).astype(BF16)
    o_ref[0, :, 2 * ns:3 * ns] = jnp.where(bmask, pltpu.roll(bre, nc - 1, 0), 0.0).astype(BF16)
    o_ref[0, :, 3 * ns:4 * ns] = jnp.where(bmask, pltpu.roll(bim, nc - 1, 0), 0.0).astype(BF16)


def _s5_scan(xl, pr, pi, b, nc):
    nsteps = int(math.log2(nc))
    width = 4 * SSM_NSTATE
    x = xl.reshape(b, nc, width)
    out = pl.pallas_call(
        functools.partial(_s5_scan_body, nc=nc, nsteps=nsteps),
        grid=(b,),
        in_specs=[pl.BlockSpec((1, nc, width), lambda i: (i, 0, 0)),
                  pl.BlockSpec((nsteps, 2 * SSM_NSTATE), lambda i: (0, 0)),
                  pl.BlockSpec((nsteps, 2 * SSM_NSTATE), lambda i: (0, 0))],
        out_specs=pl.BlockSpec((1, nc, width), lambda i: (i, 0, 0)),
        out_shape=jax.ShapeDtypeStruct((b, nc, width), BF16),
        compiler_params=_cparams(("arbitrary",)),
        name="s5_scan",
    )(x, pr, pi)
    return out.reshape(b * nc, width)


def _s5_out_body(za_ref, zb_ref, s_ref, m_ref, w_ref, d_ref, gw_ref, gb_ref, g_ref, o_ref, nat_a, nat_b):
    tr = s_ref.shape[0]
    steps = _chunk_steps(za_ref, zb_ref, tr)
    y = _dot(s_ref[...], w_ref[...])
    for r, u in enumerate(steps):
        y = y + _dot(u.astype(BF16), m_ref[r * SSM_WIDTH:(r + 1) * SSM_WIDTH, :])
    c0 = math.sqrt(2.0 / math.pi)
    for r, u in enumerate(steps):
        v = y[:, r * SSM_WIDTH:(r + 1) * SSM_WIDTH] + d_ref[...] * u
        gl = 0.5 * v * (1.0 + jnp.tanh(c0 * (v + 0.044715 * (v * v * v))))
        gate = jax.nn.sigmoid(_dot(gl.astype(BF16), gw_ref[...]) + gb_ref[...])
        out = _rms(gl * gate, g_ref[...])
        nat_a[pl.ds(r, tr, stride=SSM_CHUNK), :] = out[:, :LANES]
        nat_b[pl.ds(r, tr, stride=SSM_CHUNK), :] = out[:, LANES:]
    o_ref[:, :LANES] = nat_a[...].astype(BF16)
    o_ref[:, LANES:] = nat_b[...].astype(BF16)


def _s5_out(zs, states, m_intra, w_out, d, glu_w, glu_b, g):
    rows = states.shape[0]
    tr = min(S5_ROWS, rows)
    width = 4 * SSM_NSTATE
    fix = lambda i: (0, 0)
    once = pl.Buffered(1)
    return pl.pallas_call(
        _s5_out_body,
        grid=(rows // tr,),
        in_specs=_halves(tr * SSM_CHUNK) + [
                  pl.BlockSpec((tr, width), lambda i: (i, 0)),
                  pl.BlockSpec((SSM_ROW, SSM_ROW), fix, pipeline_mode=once),
                  pl.BlockSpec((width, SSM_ROW), fix, pipeline_mode=once),
                  pl.BlockSpec((1, SSM_WIDTH), fix), pl.BlockSpec((SSM_WIDTH, SSM_WIDTH), fix),
                  pl.BlockSpec((1, SSM_WIDTH), fix), pl.BlockSpec((1, SSM_WIDTH), fix)],
        out_specs=pl.BlockSpec((tr * SSM_CHUNK, SSM_WIDTH), lambda i: (i, 0)),
        out_shape=jax.ShapeDtypeStruct((rows * SSM_CHUNK, SSM_WIDTH), BF16),
        scratch_shapes=[pltpu.VMEM((tr * SSM_CHUNK, LANES), F32) for _ in range(SSM_WIDTH // LANES)],
        compiler_params=_cparams(("arbitrary",)),
        name="s5_out",
    )(zs, zs, states, m_intra, w_out, d, glu_w, glu_b, g)


def _attn_geometry(s, d):
    ln = s // d
    bq = min(128, ln)
    bk = min(bq + 2 * ATT_HALF, ln)
    return ln, bq, bk, ln // bq


def _attn_body(slope_ref, q_ref, k_ref, v_ref, o_ref, nat, qd, kd, vd, qdb, kdb, vdb, b0, b1, b2, *acc, s):
    hp = pl.program_id(1)
    s4 = s // ATT_SPLIT
    lane = lax.broadcasted_iota(I32, (1, LANES), 1)
    first = lane < ATT_HEAD_DIM
    second = jnp.logical_not(first)
    slopes = (slope_ref[2 * hp], slope_ref[2 * hp + 1])
    log2e = math.log2(math.e)
    scale = ATT_HEAD_DIM ** -0.5 * log2e

    for (_, d), bias in zip(DILATED_PATTERNS, (b0, b1, b2)):
        _, bq, bk, _ = _attn_geometry(s, d)
        jk = lax.broadcasted_iota(I32, (1, bk), 1)
        for o in range(ATT_OFFSETS):
            rel = jnp.abs(lax.broadcasted_iota(I32, (bq, 1), 0) + o * ATT_HALF - jk)
            dist = (d * rel).astype(F32) * log2e
            for hh in range(2):
                bias[2 * o + hh] = jnp.where(rel <= ATT_HALF, -slopes[hh] * dist, NEG_INF)

    for src, dst_f, dst_b in ((q_ref, qd, qdb), (k_ref, kd, kdb), (v_ref, vd, vdb)):
        nat[...] = src[...].astype(F32)
        for c in range(ATT_SPLIT):
            x = nat[pl.ds(c, s4, stride=ATT_SPLIT), :]
            dst_f[c * s4:(c + 1) * s4, :] = x
            dst_b[c * s4:(c + 1) * s4, :] = x.astype(BF16)

    def pattern(p, d, refs, bias, locate, stride):
        acc_o, acc_m, acc_l = acc[3 * p], acc[3 * p + 1], acc[3 * p + 2]
        ln, bq, bk, nqb = _attn_geometry(s, d)

        def rows(c, j, size):
            start = locate(c, j)
            if stride == 1:
                return pl.ds(pl.multiple_of(start, ATT_HALF), size)
            return pl.ds(start, size, stride=stride)

        def scores(i):
            c = i // nqb
            j0 = (i % nqb) * bq
            ks = jnp.clip(j0 - ATT_HALF, 0, ln - bk)
            q = refs[0][rows(c, j0, bq), :].astype(BF16)
            k = refs[1][rows(c, ks, bk), :].astype(BF16)
            v = refs[2][rows(c, ks, bk), :].astype(BF16)
            off = (j0 - ks) // ATT_HALF
            scs = []
            for hh in range(2):
                qm = jnp.where(first if hh == 0 else second, q, jnp.zeros_like(q))
                scs.append(_dot_nt(qm, k) * scale + bias[2 * off + hh])
            return rows(c, j0, bq), v, scs

        def softmax(sc):
            m = jnp.max(sc, axis=-1, keepdims=True)
            pe = jnp.exp2(sc - m)
            return pe.astype(BF16), m, jnp.sum(pe, axis=-1, keepdims=True)

        def group(g, carry):
            staged = [scores(g * ATT_UNROLL + u) for u in range(ATT_UNROLL)]
            soft = [[softmax(sc) for sc in scs] for _, _, scs in staged]
            for (dst, v, _), ((p0, m0, l0), (p1, m1, l1)) in zip(staged, soft):
                acc_o[dst, :] = jnp.where(first, _dot(p0, v), _dot(p1, v))
                acc_m[dst, :] = jnp.where(first, m0, m1)
                acc_l[dst, :] = jnp.where(first, l0, l1)
            return carry

        lax.fori_loop(0, d * nqb // ATT_UNROLL, group, 0)

    (_, d1), (_, d2), (_, d3) = DILATED_PATTERNS
    assert d1 == 1 and d2 == ATT_SPLIT and d3 == ATT_SPLIT * ATT_SPLIT
    pattern(0, d1, (q_ref, k_ref, v_ref), b0, lambda c, j: j, 1)
    pattern(1, d2, (qdb, kdb, vdb), b1, lambda c, j: c * s4 + j, 1)
    pattern(2, d3, (qd, kd, vd), b2,
            lambda c, j: (c % ATT_SPLIT) * s4 + c // ATT_SPLIT + ATT_SPLIT * j, ATT_SPLIT)

    for c in range(ATT_SPLIT):
        part = pl.ds(c, s4, stride=ATT_SPLIT)
        blk = slice(c * s4, (c + 1) * s4)
        ms = (acc[1][part, :], acc[4][blk, :], acc[7][blk, :])
        os_ = (acc[0][part, :], acc[3][blk, :], acc[6][blk, :])
        ls = (acc[2][part, :], acc[5][blk, :], acc[8][blk, :])
        m = jnp.maximum(jnp.maximum(ms[0], ms[1]), ms[2])
        num = jnp.zeros((s4, LANES), F32)
        den = jnp.zeros((s4, LANES), F32)
        for p in range(3):
            w = jnp.exp2(ms[p] - m)
            num = num + w * os_[p]
            den = den + w * ls[p]
        nat[part, :] = num / den
    o_ref[...] = nat[...].astype(BF16)


def _attn(zqkv, slopes, b, s):
    n = b * s
    nhp = ATT_HEADS // 2
    col = lambda off: (lambda i, j, sl: (i, off + j))
    return pl.pallas_call(
        functools.partial(_attn_body, s=s),
        grid_spec=pltpu.PrefetchScalarGridSpec(
            num_scalar_prefetch=1,
            grid=(b, nhp),
            in_specs=[pl.BlockSpec((s, LANES), col(0)),
                      pl.BlockSpec((s, LANES), col(nhp)),
                      pl.BlockSpec((s, LANES), col(2 * nhp))],
            out_specs=pl.BlockSpec((s, LANES), lambda i, j, sl: (i, j)),
            scratch_shapes=([pltpu.VMEM((s, LANES), F32) for _ in range(4)]
                            + [pltpu.VMEM((s, LANES), BF16) for _ in range(3)]
                            + [pltpu.VMEM((2 * ATT_OFFSETS,) + _attn_geometry(s, d)[1:3], F32)
                               for _, d in DILATED_PATTERNS]
                            + [pltpu.VMEM((s, LANES), F32) for _ in range(9)]),
        ),
        out_shape=jax.ShapeDtypeStruct((n, ATT_WIDTH), BF16),
        compiler_params=_cparams(("arbitrary", "arbitrary")),
        name="attn",
    )(slopes, zqkv, zqkv, zqkv)


def _outproj_body(h_ref, mf_ref, ms_ref, oa_ref, ga_ref, w_ref, gf_ref, wr_ref,
                  h1_ref, m_ref, aff_ref):
    oa = _rms(oa_ref[...].astype(F32), ga_ref[...]).astype(BF16)
    o1, o2 = FN_WIDTH, FN_WIDTH + SSM_WIDTH
    acc = h_ref[...] + _dot(mf_ref[...], w_ref[0:o1, :]) + _dot(ms_ref[...], w_ref[o1:o2, :])
    acc = acc + _dot(oa, w_ref[o2:, :])
    h1_ref[...] = acc
    m = _rms(acc, gf_ref[...]).astype(BF16)
    bits = pltpu.bitcast(m.astype(F32), I32)
    m_ref[...] = lax.shift_right_logical(bits[:, :PACKED], 16) | (bits[:, PACKED:] & jnp.int32(-65536))
    lg = _dot_nt(wr_ref[...], m)
    e = jnp.exp(lg - jnp.max(lg, axis=0, keepdims=True))
    aff_ref[...] = e / jnp.sum(e, axis=0, keepdims=True)


def _outproj(h, mf, ms, oa, ga, w, gf, wr_t):
    n = h.shape[0]
    tm = 512
    row = lambda i: (i, 0)
    fix = lambda i: (0, 0)
    return pl.pallas_call(
        _outproj_body,
        grid=(n // tm,),
        in_specs=[pl.BlockSpec((tm, D_MODEL), row), pl.BlockSpec((tm, FN_WIDTH), row),
                  pl.BlockSpec((tm, SSM_WIDTH), row), pl.BlockSpec((tm, ATT_WIDTH), row),
                  pl.BlockSpec((1, ATT_WIDTH), fix), pl.BlockSpec((D_MODEL, D_MODEL), fix),
                  pl.BlockSpec((1, D_MODEL), fix), pl.BlockSpec((N_EXPERTS, D_MODEL), fix)],
        out_specs=[pl.BlockSpec((tm, D_MODEL), row), pl.BlockSpec((tm, PACKED), row),
                   pl.BlockSpec((N_EXPERTS, tm), lambda i: (0, i))],
        out_shape=[jax.ShapeDtypeStruct((n, D_MODEL), F32),
                   jax.ShapeDtypeStruct((n, PACKED), I32),
                   jax.ShapeDtypeStruct((N_EXPERTS, n), F32)],
        compiler_params=_cparams(("arbitrary",)),
        name="outproj",
    )(h, mf, ms, oa, ga, w, gf, wr_t)


def _select_body(aff_ref, rank_ref, off_ref, *, cap, nb):
    r128 = lax.broadcasted_iota(I32, (LANES, LANES), 0)
    c128 = lax.broadcasted_iota(I32, (LANES, LANES), 1)
    upper_incl = jnp.where(r128 <= c128, 1.0, 0.0).astype(BF16)
    ones = jnp.ones((LANES, LANES), BF16)
    rb = lax.broadcasted_iota(I32, (nb, nb), 0)
    cb = lax.broadcasted_iota(I32, (nb, nb), 1)
    lower_strict = jnp.where(cb < rb, 1.0, 0.0).astype(BF16)
    upper_strict = jnp.where(rb < cb, 1.0, 0.0).astype(BF16)
    ones8 = jnp.ones((8, LANES), BF16)

    def count(mask):
        c = jnp.sum(jnp.where(mask, 1.0, 0.0), axis=1, keepdims=True)
        return jnp.sum(c, axis=0, keepdims=True)

    def prefix(mask):
        mb = jnp.where(mask, 1.0, 0.0).astype(BF16)
        incl = _dot(mb, upper_incl)
        tot = _dot(mb, ones)
        offs = _dot(lower_strict, tot.astype(BF16))
        return offs + incl - 1.0, mb

    def per_expert(e, carry):
        bits = pltpu.bitcast(aff_ref[e], I32)

        def bitstep(i, t):
            cand = t | jnp.left_shift(jnp.int32(1), 30 - i)
            return jnp.where(count(bits >= cand) >= cap, cand, t)

        t = lax.fori_loop(0, 31, bitstep, jnp.zeros((1, 1), I32))
        gt = bits > t
        eq = bits == t
        need = cap - count(gt)
        eq_rank, _ = prefix(eq)
        sel = jnp.logical_or(gt, jnp.logical_and(eq, eq_rank < need))
        rank, mb = prefix(sel)
        rank_ref[e] = jnp.where(sel, rank.astype(I32), -1)
        tot_row = _dot_nt(ones8, mb)
        off_row = _dot(tot_row.astype(BF16), upper_strict)
        off_ref[pl.ds(e, 1), :] = off_row[0:1, :].astype(I32)
        return carry

    lax.fori_loop(0, N_EXPERTS, per_expert, 0)


def _select(aff_t, cap):
    n = aff_t.shape[1]
    nb = n // LANES
    a3 = aff_t.reshape(N_EXPERTS, nb, LANES)
    rank, off = pl.pallas_call(
        functools.partial(_select_body, cap=cap, nb=nb),
        grid=(1,),
        in_specs=[pl.BlockSpec((N_EXPERTS, nb, LANES), lambda i: (0, 0, 0))],
        out_specs=[pl.BlockSpec((N_EXPERTS, nb, LANES), lambda i: (0, 0, 0)),
                   pl.BlockSpec((N_EXPERTS, nb), lambda i: (0, 0))],
        out_shape=[jax.ShapeDtypeStruct((N_EXPERTS, nb, LANES), I32),
                   jax.ShapeDtypeStruct((N_EXPERTS, nb), I32)],
        compiler_params=_cparams(("arbitrary",)),
        name="select",
    )(a3)
    return rank.reshape(N_EXPERTS, n), off


def _ffn_body(x_ref, wg_ref, wu_ref, wd_ref, o_ref):
    w = x_ref[...]
    x = jnp.concatenate([pltpu.bitcast(lax.shift_left(w, 16), F32),
                         pltpu.bitcast(w & jnp.int32(-65536), F32)], axis=1).astype(BF16)
    tf = 512
    acc = jnp.zeros(o_ref.shape, F32)
    for j in range(D_FF_EXPERT // tf):
        g = _dot(x, wg_ref[:, j * tf:(j + 1) * tf])
        u = _dot(x, wu_ref[:, j * tf:(j + 1) * tf])
        hdn = (g * jax.nn.sigmoid(g) * u).astype(BF16)
        acc = acc + _dot(hdn, wd_ref[j * tf:(j + 1) * tf, :])
    o_ref[...] = acc.astype(BF16)


def _ffn(xe, wg, wu, wd, layer):
    e, cap, _ = xe.shape
    tm = min(512, cap)
    return pl.pallas_call(
        _ffn_body,
        grid=(e, cap // tm),
        in_specs=[pl.BlockSpec((None, tm, PACKED), lambda i, j: (i, j, 0)),
                  pl.BlockSpec((None, None, D_MODEL, D_FF_EXPERT), lambda i, j: (layer, i, 0, 0)),
                  pl.BlockSpec((None, None, D_MODEL, D_FF_EXPERT), lambda i, j: (layer, i, 0, 0)),
                  pl.BlockSpec((None, None, D_FF_EXPERT, D_MODEL), lambda i, j: (layer, i, 0, 0))],
        out_specs=pl.BlockSpec((None, tm, D_MODEL), lambda i, j: (i, j, 0)),
        out_shape=jax.ShapeDtypeStruct((e, cap, D_MODEL), BF16),
        compiler_params=_cparams(("arbitrary", "arbitrary")),
        name="ffn",
    )(xe, wg, wu, wd)


def _combine_body(off_ref, h_ref, rank_ref, aff_ref, p_ref, gp_ref, wg_ref, wp_ref, gfin_ref, ye_ref,
                  o_ref, stack, sems, *, cap, last):
    t = pl.program_id(0)
    nt = pl.num_programs(0)
    tile, win, nsub = COMBINE_TILE, COMBINE_WIN, COMBINE_SUB
    bpt = tile // LANES
    kdim = N_EXPERTS * win

    def base(tt, e):
        return (off_ref[e, tt * bpt] // 16) * 16

    def copy(e, start, slot, sub):
        return pltpu.make_async_copy(ye_ref.at[e, pl.ds(pl.multiple_of(start, 16), win), :],
                                     stack.at[slot, sub, pl.ds(e * win, win), :], sems.at[slot, sub, e])

    def starts(tt, r):
        want = [base(tt, e) + r * win for e in range(N_EXPERTS)]
        return want, [jnp.minimum(w, cap - win) for w in want]

    def issue(tt, r, slot, sub):
        _, got = starts(tt, r)
        for e in range(N_EXPERTS):
            copy(e, got[e], slot, sub).start()

    def wait(tt, r, slot, sub):
        _, got = starts(tt, r)
        for e in range(N_EXPERTS):
            copy(e, got[e], slot, sub).wait()

    slot = t % 2

    @pl.when(t == 0)
    def _():
        for sub in range(nsub):
            issue(sub, 0, 0, sub)

    @pl.when(t + 1 < nt)
    def _():
        for sub in range(nsub):
            issue((t + 1) * nsub + sub, 0, 1 - slot, sub)

    lane16 = lax.broadcasted_iota(I32, (1, N_EXPERTS), 1)
    er = lax.broadcasted_iota(I32, (N_EXPERTS, kdim), 0)
    ec = lax.broadcasted_iota(I32, (N_EXPERTS, kdim), 1)
    expand = jnp.where(ec // win == er, 1.0, 0.0).astype(BF16)
    lane_in = (lax.broadcasted_iota(I32, (1, kdim), 1) % win).astype(F32)

    def row_of(vals):
        r = jnp.zeros((1, N_EXPERTS), I32)
        for e in range(N_EXPERTS):
            r = jnp.where(lane16 == e, vals[e], r)
        return r

    def accumulate(sub, r):
        rows = pl.ds(sub * tile, tile)
        want, got = starts(t * nsub + sub, r)
        wrow, grow = row_of(want), row_of(got)
        rk = rank_ref[rows, :]
        ok = jnp.logical_and(rk >= wrow, rk < grow + win)
        rel = jnp.where(ok, rk - grow, -1).astype(F32).astype(BF16)
        hit = _dot(rel, expand) == lane_in
        gates = _dot(aff_ref[rows, :].astype(BF16), expand)
        w = jnp.where(hit, gates, 0.0).astype(BF16)
        return _dot(w, stack[slot, sub])

    for sub in range(nsub):
        wait(t * nsub + sub, 0, slot, sub)
    for sub in range(nsub):
        rows = pl.ds(sub * tile, tile)
        o_ref[rows, :] = h_ref[rows, :] + accumulate(sub, 0)

    for sub in range(nsub):
        tt = t * nsub + sub
        nr = jnp.int32(1)
        for e in range(N_EXPERTS):
            span = off_ref[e, (tt + 1) * bpt] - base(tt, e)
            nr = jnp.maximum(nr, (span + win - 1) // win)

        def extra(r, carry, sub=sub, tt=tt):
            issue(tt, r, slot, sub)
            wait(tt, r, slot, sub)
            o_ref[pl.ds(sub * tile, tile), :] += accumulate(sub, r)
            return carry

        lax.fori_loop(1, nr, extra, 0)

    for sub in range(nsub):
        rows = pl.ds(sub * tile, tile)
        h2 = o_ref[rows, :]
        gate = jax.nn.sigmoid(_dot(_rms(h2, gp_ref[...]).astype(BF16), wg_ref[...]))
        h3 = h2 + _dot(p_ref[rows, :].astype(BF16), wp_ref[...]) * gate
        o_ref[rows, :] = _rms(h3, gfin_ref[...]) if last else h3


def _combine(off, h1, rank_tok, aff_tok, p, layer, gp, wg, wp, gfin, ye, cap, last):
    n = h1.shape[0]
    tile = COMBINE_TILE * COMBINE_SUB
    row = lambda i, o: (i, 0)
    fix = lambda i, o: (0, 0)
    prow = lambda i, o: (layer * (n // tile) + i, 0)
    return pl.pallas_call(
        functools.partial(_combine_body, cap=cap, last=last),
        grid_spec=pltpu.PrefetchScalarGridSpec(
            num_scalar_prefetch=1,
            grid=(n // tile,),
            in_specs=[pl.BlockSpec((tile, D_MODEL), row),
                      pl.BlockSpec((tile, N_EXPERTS), row),
                      pl.BlockSpec((tile, N_EXPERTS), row),
                      pl.BlockSpec((tile, PLE_DIM), prow),
                      pl.BlockSpec((1, D_MODEL), fix),
                      pl.BlockSpec((D_MODEL, D_MODEL), fix),
                      pl.BlockSpec((PLE_DIM, D_MODEL), fix),
                      pl.BlockSpec((1, D_MODEL), fix),
                      pl.BlockSpec(memory_space=pl.ANY)],
            out_specs=pl.BlockSpec((tile, D_MODEL), row),
            scratch_shapes=[pltpu.VMEM((2, COMBINE_SUB, N_EXPERTS * COMBINE_WIN, D_MODEL), BF16),
                            pltpu.SemaphoreType.DMA((2, COMBINE_SUB, N_EXPERTS))],
        ),
        out_shape=jax.ShapeDtypeStruct((n, D_MODEL), F32),
        compiler_params=_cparams(("arbitrary",)),
        name="combine",
    )(off, h1, rank_tok, aff_tok, p, gp, wg, wp, gfin, ye)


def _gather_rows(table, rank, cap):
    n, w = table.shape
    workers_per_expert = SC_CORES * SC_SUBCORES // N_EXPERTS
    per = cap // workers_per_expert
    nch = per // GATHER_ROWS
    mesh = plsc.VectorSubcoreMesh(core_axis_name="c", subcore_axis_name="s",
                                  num_cores=SC_CORES, num_subcores=SC_SUBCORES)
    cp = dataclasses.replace(pltpu.CompilerParams(), needs_layout_passes=False)

    @functools.partial(
        pl.kernel, mesh=mesh, compiler_params=cp,
        out_type=jax.ShapeDtypeStruct((N_EXPERTS * cap, w), I32),
        scratch_types=[pltpu.VMEM((RANK_CHUNK,), I32),
                       pltpu.VMEM((nch, GATHER_ROWS), I32),
                       pltpu.VMEM((GATHER_ROWS, w), I32),
                       pltpu.SemaphoreType.DMA],
        name="sc_gather")
    def gather(table_hbm, rank_hbm, out_hbm, rbuf, idx, rows, sem):
        wid = lax.axis_index("s") * SC_CORES + lax.axis_index("c")
        e = wid // workers_per_expert
        lo = (wid % workers_per_expert) * per
        lane = lax.iota(I32, SC_LANES)

        @pl.loop(0, n // RANK_CHUNK)
        def _(c):
            pltpu.sync_copy(rank_hbm.at[e, pl.ds(c * RANK_CHUNK, RANK_CHUNK)], rbuf)

            @pl.loop(0, RANK_CHUNK // SC_LANES)
            def _(i):
                rel = rbuf[pl.ds(i * SC_LANES, SC_LANES)] - lo
                mask = jnp.logical_and(rel >= 0, rel < per)
                rel = jnp.where(mask, rel, 0)
                tok = c * RANK_CHUNK + i * SC_LANES + lane
                plsc.store_scatter(idx, [rel // GATHER_ROWS, rel % GATHER_ROWS], tok, mask=mask)

        @pl.loop(0, nch)
        def _(c):
            pltpu.async_copy(table_hbm.at[idx.at[c]], rows, sem).wait()
            pltpu.sync_copy(rows, out_hbm.at[pl.ds(e * cap + lo + c * GATHER_ROWS, GATHER_ROWS)])

    return gather(table, rank).reshape(N_EXPERTS, cap, w)


def _trunk(x, p, prm, b, s):
    n = b * s
    nc = s // SSM_CHUNK
    cap = EC_CAPACITY_FACTOR * n // N_EXPERTS
    cs, ss = _dft_tables(s)
    cs, ss = cs.astype(BF16), ss.astype(BF16)
    h = x.reshape(n, D_MODEL)
    depth = prm['w_in'].shape[0]
    p_rows = p.reshape(depth * n, PLE_DIM)
    for l in range(depth):
        lp = prm['layers'][l]
        zf, zs, zqkv = _inproj(h, lp['g_mix'], lp['w_in'])
        mf = _fourier(zf, cs, ss, prm['ccb'], prm['scb'], lp['w_fnet'], lp['g_f'], b, s)
        xl = _s5_in(zs, lp['s5_w_in'])
        pr, pi = _scan_powers(lp['s5_a_chunk'], int(math.log2(nc)))
        st = _s5_scan(xl, pr, pi, b, nc)
        ms = _s5_out(zs, st, lp['s5_m_intra'], lp['s5_w_out'], lp['s5_d'], lp['s5_glu_w'], lp['s5_glu_b'],
                     lp['g_s'])
        oa = _attn(zqkv, prm['slopes'], b, s)
        h1, m, aff_t = _outproj(h, mf, ms, oa, lp['g_a'], lp['w_out'], lp['g_ffn'], lp['w_router_t'])
        rank, off = _select(aff_t, cap)
        off = jnp.concatenate([off, jnp.full((N_EXPERTS, 1), cap, I32)], axis=1)
        xe = _gather_rows(m, rank, cap)
        ye = _ffn(xe, prm['w_gate'], prm['w_up'], prm['w_down'], l)
        h = _combine(off, h1, rank.T, aff_t.T, p_rows, l, lp['g_ple'], lp['w_ple_gate'],
                     lp['w_ple_proj'], prm['g_final'], ye, cap, l == depth - 1)
    return h.reshape(b, s, D_MODEL)


def kernel(x_prompt, x_sample, p_prompt, p_sample, norm_mix, w_in, w_fnet, ssm_a_re, ssm_a_im, ssm_log_dt, ssm_b_re, ssm_b_im, ssm_c_re, ssm_c_im, ssm_d, ssm_glu_w, ssm_glu_b, norm_branch, w_out, norm_ffn, w_router, w_exp_gate, w_exp_up, w_exp_down, norm_ple, w_ple_gate, w_ple_proj, norm_final):
    depth = w_in.shape[0]
    o1, o2 = FN_WIDTH, FN_WIDTH + SSM_WIDTH
    row = lambda v: v.reshape(1, -1).astype(F32)
    cc, sc = _dft_tables(FN_HEAD_DIM)
    tile4 = lambda mtx: _block_diag(jnp.broadcast_to(mtx[None], (FN_HEADS,) + mtx.shape)).astype(BF16)
    layers = []
    for l in range(depth):
        m_intra, s5_w_in, s5_w_out, a_chunk = _s5_matrices(
            ssm_a_re[l], ssm_a_im[l], ssm_log_dt[l], ssm_b_re[l], ssm_b_im[l], ssm_c_re[l], ssm_c_im[l])
        layers.append(dict(
            g_mix=row(norm_mix[l]), w_in=w_in[l].astype(BF16),
            w_fnet=_block_diag(w_fnet[l]).astype(BF16),
            g_f=row(norm_branch[l][:o1]), g_s=row(norm_branch[l][o1:o2]), g_a=row(norm_branch[l][o2:]),
            s5_m_intra=m_intra, s5_w_in=s5_w_in, s5_w_out=s5_w_out, s5_a_chunk=a_chunk,
            s5_d=row(ssm_d[l]), s5_glu_w=ssm_glu_w[l].astype(BF16), s5_glu_b=row(ssm_glu_b[l]),
            w_out=w_out[l].astype(BF16), g_ffn=row(norm_ffn[l]),
            w_router_t=w_router[l].T.astype(BF16),
            g_ple=row(norm_ple[l]), w_ple_gate=w_ple_gate[l].astype(BF16),
            w_ple_proj=w_ple_proj[l].astype(BF16)))
    slopes = jnp.asarray([2.0 ** (-8.0 * (i + 1) / ATT_HEADS) for i in range(ATT_HEADS)], F32)
    prm = dict(w_in=w_in, layers=layers, ccb=tile4(cc), scb=tile4(sc), slopes=slopes,
               w_gate=w_exp_gate.astype(BF16), w_up=w_exp_up.astype(BF16), w_down=w_exp_down.astype(BF16),
               g_final=row(norm_final))
    bp, sp = x_prompt.shape[0], x_prompt.shape[1]
    bs, ssq = x_sample.shape[0], x_sample.shape[1]
    y_prompt = _trunk(x_prompt, p_prompt, prm, bp, sp)
    y_sample = _trunk(x_sample, p_sample, prm, bs, ssq)
    return (y_prompt, y_sample)
```

```python
import dataclasses
import functools
import math

import jax
import jax.numpy as jnp
from jax import lax
from jax.experimental import pallas as pl
from jax.experimental.pallas import tpu as pltpu
from jax.experimental.pallas import tpu_sc as plsc

D_MODEL = 1024
FN_WIDTH = 256
FN_HEADS = 4
FN_HEAD_DIM = 64
SSM_WIDTH = 256
SSM_GROUP = 16
SSM_GROUPS = 16
SSM_STATE = 64
ATT_WIDTH = 512
ATT_HEAD_DIM = 64
ATT_HEADS = 8
DILATED_PATTERNS = ((128, 1), (512, 4), (2048, 16))
IN_PROJ_WIDTH = 2048
N_EXPERTS = 16
EC_CAPACITY_FACTOR = 2
D_FF_EXPERT = 2048
PLE_DIM = 256
RMS_EPS = 1e-6
NEG_INF = -1e30

LANES = 128
SSM_CHUNK = 8
SSM_ROW = SSM_CHUNK * SSM_WIDTH
SSM_NSTATE = SSM_GROUPS * SSM_STATE
S5_ROWS = 256
ATT_HALF = 64
ATT_UNROLL = 8
ATT_SPLIT = 4
ATT_OFFSETS = 3
FFN_ROWS = 1024
COMBINE_TILE = 256
COMBINE_SUB = 2
COMBINE_WIN = 64
VMEM_LIMIT = 56 * 1024 * 1024
SC_CORES = 2
SC_SUBCORES = 16
SC_LANES = 16
GATHER_ROWS = 64
RANK_CHUNK = 2048
PACKED = D_MODEL // 2

F32 = jnp.float32
BF16 = jnp.bfloat16
I32 = jnp.int32


def _cparams(sem):
    return pltpu.CompilerParams(dimension_semantics=sem, vmem_limit_bytes=VMEM_LIMIT)


def _rms(x, g):
    return x * lax.rsqrt(jnp.mean(x * x, axis=-1, keepdims=True) + RMS_EPS) * g


def _dot(a, b):
    return jnp.dot(a, b, preferred_element_type=F32)


def _dot_nt(a, b):
    return lax.dot_general(a, b, (((1,), (1,)), ((), ())), preferred_element_type=F32)


def _inproj_body(h_ref, g_ref, w_ref, zf_ref, zs_ref, zqkv_ref):
    a = _rms(h_ref[...], g_ref[...]).astype(BF16)
    z = _dot(a, w_ref[...])
    zf_ref[...] = z[:, :FN_WIDTH]
    zs_ref[...] = z[:, FN_WIDTH:FN_WIDTH + SSM_WIDTH]
    zqkv_ref[...] = z[:, FN_WIDTH + SSM_WIDTH:].astype(BF16)


def _inproj(h, g, w):
    n = h.shape[0]
    tm = 512
    return pl.pallas_call(
        _inproj_body,
        grid=(n // tm,),
        in_specs=[pl.BlockSpec((tm, D_MODEL), lambda i: (i, 0)),
                  pl.BlockSpec((1, D_MODEL), lambda i: (0, 0)),
                  pl.BlockSpec((D_MODEL, IN_PROJ_WIDTH), lambda i: (0, 0))],
        out_specs=[pl.BlockSpec((tm, FN_WIDTH), lambda i: (i, 0)),
                   pl.BlockSpec((tm, SSM_WIDTH), lambda i: (i, 0)),
                   pl.BlockSpec((tm, 3 * ATT_WIDTH), lambda i: (i, 0))],
        out_shape=[jax.ShapeDtypeStruct((n, FN_WIDTH), F32),
                   jax.ShapeDtypeStruct((n, SSM_WIDTH), F32),
                   jax.ShapeDtypeStruct((n, 3 * ATT_WIDTH), BF16)],
        compiler_params=_cparams(("arbitrary",)),
        name="inproj",
    )(h, g, w)


def _fourier_body(x_ref, cs_ref, ss_ref, cc_ref, sc_ref, wb_ref, g_ref, o_ref):
    x = x_ref[0].astype(BF16)
    y = _dot(cs_ref[...], x).astype(BF16)
    z = _dot(ss_ref[...], x).astype(BF16)
    f = _dot(y, cc_ref[...]) - _dot(z, sc_ref[...])
    o = _dot(f.astype(BF16), wb_ref[...])
    o_ref[0] = _rms(o, g_ref[...]).astype(BF16)


def _dft_tables(n):
    def exact(rows):
        k = (rows[:, None] * jnp.arange(n, dtype=I32)[None, :]) % n
        ang = k.astype(F32) * (2.0 * math.pi / n)
        return jnp.cos(ang), jnp.sin(ang)
    if n <= LANES:
        return exact(jnp.arange(n, dtype=I32))
    ca, sa = exact(LANES * jnp.arange(n // LANES, dtype=I32))
    cb, sb = exact(jnp.arange(LANES, dtype=I32))
    cos = ca[:, None, :] * cb[None] - sa[:, None, :] * sb[None]
    sin = sa[:, None, :] * cb[None] + ca[:, None, :] * sb[None]
    return cos.reshape(n, n), sin.reshape(n, n)


def _fourier(zf, cs, ss, ccb, scb, wb, g, b, s):
    tr = 512
    x = zf.reshape(b, s, FN_WIDTH)
    out = pl.pallas_call(
        _fourier_body,
        grid=(s // tr, b),
        in_specs=[pl.BlockSpec((1, s, FN_WIDTH), lambda i, j: (j, 0, 0)),
                  pl.BlockSpec((tr, s), lambda i, j: (i, 0)),
                  pl.BlockSpec((tr, s), lambda i, j: (i, 0)),
                  pl.BlockSpec((FN_WIDTH, FN_WIDTH), lambda i, j: (0, 0)),
                  pl.BlockSpec((FN_WIDTH, FN_WIDTH), lambda i, j: (0, 0)),
                  pl.BlockSpec((FN_WIDTH, FN_WIDTH), lambda i, j: (0, 0)),
                  pl.BlockSpec((1, FN_WIDTH), lambda i, j: (0, 0))],
        out_specs=pl.BlockSpec((1, tr, FN_WIDTH), lambda i, j: (j, i, 0)),
        out_shape=jax.ShapeDtypeStruct((b, s, FN_WIDTH), BF16),
        compiler_params=_cparams(("arbitrary", "arbitrary")),
        name="fourier",
    )(x, cs, ss, ccb, scb, wb, g)
    return out.reshape(b * s, FN_WIDTH)


def _block_diag(blocks):
    h, a, bb = blocks.shape
    eye = jnp.eye(h, dtype=blocks.dtype)
    return jnp.einsum('hab,hg->hagb', blocks, eye).reshape(h * a, h * bb)


def _s5_matrices(a_re, a_im, log_dt, b_re, b_im, c_re, c_im):
    t = SSM_CHUNK
    g, p, c = SSM_GROUPS, SSM_STATE, SSM_GROUP
    lam = lax.complex(a_re.astype(F32), a_im.astype(F32))
    dt = jnp.exp(log_dt.astype(F32))[..., None]
    abar = jnp.exp(lam * dt)
    bbar = ((abar - 1.0) / lam)[..., None] * lax.complex(b_re.astype(F32), b_im.astype(F32))
    cmat = lax.complex(c_re.astype(F32), c_im.astype(F32))
    ks = jnp.arange(t + 1, dtype=F32)
    apow = jnp.exp((lam * dt)[:, None] * ks[None, :, None, None])
    eye = jnp.eye(g, dtype=F32)

    kern = jnp.real(jnp.einsum('dgcp,dkgp,dgpe->dkgce', cmat, apow[:, :t], bbar))
    lags = jnp.arange(-(t - 1), t)
    pick = lambda m: m[:, None, None, None]
    klag = (jnp.where(pick(lags >= 0), kern[0][jnp.clip(lags, 0, t - 1)], 0.0)
            + jnp.where(pick(lags <= 0), kern[1][jnp.clip(-lags, 0, t - 1)], 0.0))
    blocks = jnp.einsum('lgce,gh->lgehc', klag, eye).reshape(2 * t - 1, SSM_WIDTH, SSM_WIDTH)
    lag = jnp.arange(t)[None, :] - jnp.arange(t)[:, None]
    m_intra = blocks[lag + t - 1].transpose(0, 2, 1, 3).reshape(SSM_ROW, SSM_ROW)

    wf = apow[0, t - 1 - jnp.arange(t)][:, :, :, None] * bbar[0][None]
    wb = apow[1, jnp.arange(t)][:, :, :, None] * bbar[1][None]
    def _w_in(w):
        w6 = jnp.einsum('rgpe,gh->rgehp', w, eye.astype(w.dtype))
        return w6.reshape(SSM_ROW, SSM_NSTATE)
    wf, wb = _w_in(wf), _w_in(wb)
    w_in = jnp.concatenate([jnp.real(wf), jnp.imag(wf), jnp.real(wb), jnp.imag(wb)], axis=1)

    qf = cmat[0][None] * apow[0, 1 + jnp.arange(t)][:, :, None, :]
    qb = cmat[1][None] * apow[1, t - jnp.arange(t)][:, :, None, :]
    def _w_out(q):
        q6 = jnp.einsum('rgcp,gh->gprhc', q, eye.astype(q.dtype))
        return q6.reshape(SSM_NSTATE, SSM_ROW)
    qf, qb = _w_out(qf), _w_out(qb)
    w_out = jnp.concatenate([jnp.real(qf), -jnp.imag(qf), jnp.real(qb), -jnp.imag(qb)], axis=0)

    a_chunk = lam * dt * t
    return m_intra.astype(BF16), w_in.astype(BF16), w_out.astype(BF16), a_chunk


def _scan_powers(a_chunk, nsteps):
    e = jnp.exp(a_chunk[None] * (2.0 ** jnp.arange(nsteps, dtype=F32))[:, None, None, None])
    e = e.reshape(nsteps, 2 * SSM_NSTATE)
    return jnp.real(e), jnp.imag(e)


def _chunk_steps(za_ref, zb_ref, tr):
    return [jnp.concatenate([za_ref[pl.ds(r, tr, stride=SSM_CHUNK), :],
                             zb_ref[pl.ds(r, tr, stride=SSM_CHUNK), :]], axis=1) for r in range(SSM_CHUNK)]


def _halves(rows):
    return [pl.BlockSpec((rows, LANES), lambda i: (i, 0)), pl.BlockSpec((rows, LANES), lambda i: (i, 1))]


def _s5_in_body(za_ref, zb_ref, w_ref, o_ref):
    tr = o_ref.shape[0]
    acc = jnp.zeros(o_ref.shape, F32)
    for r, u in enumerate(_chunk_steps(za_ref, zb_ref, tr)):
        acc = acc + _dot(u.astype(BF16), w_ref[r * SSM_WIDTH:(r + 1) * SSM_WIDTH, :])
    o_ref[...] = acc


def _s5_in(zs, w_in):
    rows = zs.shape[0] // SSM_CHUNK
    tr = min(S5_ROWS, rows)
    width = 4 * SSM_NSTATE
    return pl.pallas_call(
        _s5_in_body,
        grid=(rows // tr,),
        in_specs=_halves(tr * SSM_CHUNK) + [
            pl.BlockSpec((SSM_ROW, width), lambda i: (0, 0), pipeline_mode=pl.Buffered(1))],
        out_specs=pl.BlockSpec((tr, width), lambda i: (i, 0)),
        out_shape=jax.ShapeDtypeStruct((rows, width), F32),
        compiler_params=_cparams(("arbitrary",)),
        name="s5_in",
    )(zs, zs, w_in)


def _s5_scan_body(x_ref, pr_ref, pi_ref, o_ref, *, nc, nsteps):
    ns = SSM_NSTATE
    x = x_ref[0]
    fre, fim = x[:, 0:ns], x[:, ns:2 * ns]
    bre, bim = x[:, 2 * ns:3 * ns], x[:, 3 * ns:4 * ns]
    row = lax.broadcasted_iota(I32, (nc, 1), 0)
    for k in range(nsteps):
        sh = 2 ** k
        far, fai = pr_ref[k:k + 1, 0:ns], pi_ref[k:k + 1, 0:ns]
        bar, bai = pr_ref[k:k + 1, ns:2 * ns], pi_ref[k:k + 1, ns:2 * ns]
        fmask = row >= sh
        sre = jnp.where(fmask, pltpu.roll(fre, sh, 0), 0.0)
        sim = jnp.where(fmask, pltpu.roll(fim, sh, 0), 0.0)
        fre, fim = fre + far * sre - fai * sim, fim + far * sim + fai * sre
        bmask = row < nc - sh
        sre = jnp.where(bmask, pltpu.roll(bre, nc - sh, 0), 0.0)
        sim = jnp.where(bmask, pltpu.roll(bim, nc - sh, 0), 0.0)
        bre, bim = bre + bar * sre - bai * sim, bim + bar * sim + bai * sre
    fmask = row >= 1
    bmask = row < nc - 1
    o_ref[0, :, 0:ns] = jnp.where(fmask, pltpu.roll(fre, 1, 0), 0.0).astype(BF16)
    o_ref[0, :, ns:2 * ns] = jnp.where(fmask, pltpu.roll(fim, 1, 0), 0.0).astype(BF16)
    o_ref[0, :, 2 * ns:3 * ns] = jnp.where(bmask, pltpu.roll(bre, nc - 1, 0), 0.0).astype(BF16)
    o_ref[0, :, 3 * ns:4 * ns] = jnp.where(bmask, pltpu.roll(bim, nc - 1, 0), 0.0).astype(BF16)


def _s5_scan(xl, pr, pi, b, nc):
    nsteps = int(math.log2(nc))
    width = 4 * SSM_NSTATE
    x = xl.reshape(b, nc, width)
    out = pl.pallas_call(
        functools.partial(_s5_scan_body, nc=nc, nsteps=nsteps),
        grid=(b,),
        in_specs=[pl.BlockSpec((1, nc, width), lambda i: (i, 0, 0)),
                  pl.BlockSpec((nsteps, 2 * SSM_NSTATE), lambda i: (0, 0)),
                  pl.BlockSpec((nsteps, 2 * SSM_NSTATE), lambda i: (0, 0))],
        out_specs=pl.BlockSpec((1, nc, width), lambda i: (i, 0, 0)),
        out_shape=jax.ShapeDtypeStruct((b, nc, width), BF16),
        compiler_params=_cparams(("arbitrary",)),
        name="s5_scan",
    )(x, pr, pi)
    return out.reshape(b * nc, width)


def _s5_out_body(za_ref, zb_ref, s_ref, m_ref, w_ref, d_ref, gw_ref, gb_ref, g_ref, o_ref, nat_a, nat_b):
    tr = s_ref.shape[0]
    steps = _chunk_steps(za_ref, zb_ref, tr)
    y = _dot(s_ref[...], w_ref[...])
    for r, u in enumerate(steps):
        y = y + _dot(u.astype(BF16), m_ref[r * SSM_WIDTH:(r + 1) * SSM_WIDTH, :])
    c0 = math.sqrt(2.0 / math.pi)
    for r, u in enumerate(steps):
        v = y[:, r * SSM_WIDTH:(r + 1) * SSM_WIDTH] + d_ref[...] * u
        gl = 0.5 * v * (1.0 + jnp.tanh(c0 * (v + 0.044715 * (v * v * v))))
        gate = jax.nn.sigmoid(_dot(gl.astype(BF16), gw_ref[...]) + gb_ref[...])
        out = _rms(gl * gate, g_ref[...])
        nat_a[pl.ds(r, tr, stride=SSM_CHUNK), :] = out[:, :LANES]
        nat_b[pl.ds(r, tr, stride=SSM_CHUNK), :] = out[:, LANES:]
    o_ref[:, :LANES] = nat_a[...].astype(BF16)
    o_ref[:, LANES:] = nat_b[...].astype(BF16)


def _s5_out(zs, states, m_intra, w_out, d, glu_w, glu_b, g):
    rows = states.shape[0]
    tr = min(S5_ROWS, rows)
    width = 4 * SSM_NSTATE
    fix = lambda i: (0, 0)
    once = pl.Buffered(1)
    return pl.pallas_call(
        _s5_out_body,
        grid=(rows // tr,),
        in_specs=_halves(tr * SSM_CHUNK) + [
                  pl.BlockSpec((tr, width), lambda i: (i, 0)),
                  pl.BlockSpec((SSM_ROW, SSM_ROW), fix, pipeline_mode=once),
                  pl.BlockSpec((width, SSM_ROW), fix, pipeline_mode=once),
                  pl.BlockSpec((1, SSM_WIDTH), fix), pl.BlockSpec((SSM_WIDTH, SSM_WIDTH), fix),
                  pl.BlockSpec((1, SSM_WIDTH), fix), pl.BlockSpec((1, SSM_WIDTH), fix)],
        out_specs=pl.BlockSpec((tr * SSM_CHUNK, SSM_WIDTH), lambda i: (i, 0)),
        out_shape=jax.ShapeDtypeStruct((rows * SSM_CHUNK, SSM_WIDTH), BF16),
        scratch_shapes=[pltpu.VMEM((tr * SSM_CHUNK, LANES), F32) for _ in range(SSM_WIDTH // LANES)],
        compiler_params=_cparams(("arbitrary",)),
        name="s5_out",
    )(zs, zs, states, m_intra, w_out, d, glu_w, glu_b, g)


def _attn_geometry(s, d):
    ln = s // d
    bq = min(128, ln)
    bk = min(bq + 2 * ATT_HALF, ln)
    return ln, bq, bk, ln // bq


def _attn_body(slope_ref, q_ref, k_ref, v_ref, o_ref, nat, qd, kd, vd, qdb, kdb, vdb, b0, b1, b2, *acc, s):
    hp = pl.program_id(1)
    s4 = s // ATT_SPLIT
    lane = lax.broadcasted_iota(I32, (1, LANES), 1)
    first = lane < ATT_HEAD_DIM
    second = jnp.logical_not(first)
    slopes = (slope_ref[2 * hp], slope_ref[2 * hp + 1])
    log2e = math.log2(math.e)
    scale = ATT_HEAD_DIM ** -0.5 * log2e

    @pl.when(pl.program_id(0) == 0)
    def _():
        for (_, d), bias in zip(DILATED_PATTERNS, (b0, b1, b2)):
            _, bq, bk, _ = _attn_geometry(s, d)
            jk = lax.broadcasted_iota(I32, (1, bk), 1)
            for o in range(ATT_OFFSETS):
                rel = jnp.abs(lax.broadcasted_iota(I32, (bq, 1), 0) + o * ATT_HALF - jk)
                dist = (d * rel).astype(F32) * log2e
                for hh in range(2):
                    bias[hp, 2 * o + hh] = jnp.where(rel <= ATT_HALF, -slopes[hh] * dist, NEG_INF)

    for src, dst_f, dst_b in ((q_ref, qd, qdb), (k_ref, kd, kdb), (v_ref, vd, vdb)):
        nat[...] = src[...].astype(F32)
        for c in range(ATT_SPLIT):
            x = nat[pl.ds(c, s4, stride=ATT_SPLIT), :]
            dst_f[c * s4:(c + 1) * s4, :] = x
            dst_b[c * s4:(c + 1) * s4, :] = x.astype(BF16)

    def pattern(p, d, refs, bias, locate, stride):
        acc_o, acc_m, acc_l = acc[3 * p], acc[3 * p + 1], acc[3 * p + 2]
        ln, bq, bk, nqb = _attn_geometry(s, d)

        def rows(c, j, size):
            start = locate(c, j)
            if stride == 1:
                return pl.ds(pl.multiple_of(start, ATT_HALF), size)
            return pl.ds(start, size, stride=stride)

        def scores(i):
            c = i // nqb
            j0 = (i % nqb) * bq
            ks = jnp.clip(j0 - ATT_HALF, 0, ln - bk)
            q = refs[0][rows(c, j0, bq), :].astype(BF16)
            k = refs[1][rows(c, ks, bk), :].astype(BF16)
            v = refs[2][rows(c, ks, bk), :].astype(BF16)
            off = (j0 - ks) // ATT_HALF
            scs = []
            for hh in range(2):
                qm = jnp.where(first if hh == 0 else second, q, jnp.zeros_like(q))
                scs.append(_dot_nt(qm, k) * scale + bias[hp, 2 * off + hh])
            return rows(c, j0, bq), v, scs

        def softmax(sc):
            m = jnp.max(sc, axis=-1, keepdims=True)
            pe = jnp.exp2(sc - m)
            return pe.astype(BF16), m, jnp.sum(pe, axis=-1, keepdims=True)

        def group(g, carry):
            staged = [scores(g * ATT_UNROLL + u) for u in range(ATT_UNROLL)]
            soft = [[softmax(sc) for sc in scs] for _, _, scs in staged]
            for (dst, v, _), ((p0, m0, l0), (p1, m1, l1)) in zip(staged, soft):
                acc_o[dst, :] = jnp.where(first, _dot(p0, v), _dot(p1, v))
                acc_m[dst, :] = jnp.where(first, m0, m1)
                acc_l[dst, :] = jnp.where(first, l0, l1)
            return carry

        lax.fori_loop(0, d * nqb // ATT_UNROLL, group, 0)

    (_, d1), (_, d2), (_, d3) = DILATED_PATTERNS
    assert d1 == 1 and d2 == ATT_SPLIT and d3 == ATT_SPLIT * ATT_SPLIT
    pattern(0, d1, (q_ref, k_ref, v_ref), b0, lambda c, j: j, 1)
    pattern(1, d2, (qdb, kdb, vdb), b1, lambda c, j: c * s4 + j, 1)
    pattern(2, d3, (qd, kd, vd), b2,
            lambda c, j: (c % ATT_SPLIT) * s4 + c // ATT_SPLIT + ATT_SPLIT * j, ATT_SPLIT)

    for c in range(ATT_SPLIT):
        part = pl.ds(c, s4, stride=ATT_SPLIT)
        blk = slice(c * s4, (c + 1) * s4)
        ms = (acc[1][part, :], acc[4][blk, :], acc[7][blk, :])
        os_ = (acc[0][part, :], acc[3][blk, :], acc[6][blk, :])
        ls = (acc[2][part, :], acc[5][blk, :], acc[8][blk, :])
        m = jnp.maximum(jnp.maximum(ms[0], ms[1]), ms[2])
        num = jnp.zeros((s4, LANES), F32)
        den = jnp.zeros((s4, LANES), F32)
        for p in range(3):
            w = jnp.exp2(ms[p] - m)
            num = num + w * os_[p]
            den = den + w * ls[p]
        nat[part, :] = num / den
    o_ref[...] = nat[...].astype(BF16)


def _attn(zqkv, slopes, b, s):
    n = b * s
    nhp = ATT_HEADS // 2
    col = lambda off: (lambda i, j, sl: (i, off + j))
    return pl.pallas_call(
        functools.partial(_attn_body, s=s),
        grid_spec=pltpu.PrefetchScalarGridSpec(
            num_scalar_prefetch=1,
            grid=(b, nhp),
            in_specs=[pl.BlockSpec((s, LANES), col(0)),
                      pl.BlockSpec((s, LANES), col(nhp)),
                      pl.BlockSpec((s, LANES), col(2 * nhp))],
            out_specs=pl.BlockSpec((s, LANES), lambda i, j, sl: (i, j)),
            scratch_shapes=([pltpu.VMEM((s, LANES), F32) for _ in range(4)]
                            + [pltpu.VMEM((s, LANES), BF16) for _ in range(3)]
                            + [pltpu.VMEM((nhp, 2 * ATT_OFFSETS) + _attn_geometry(s, d)[1:3], F32)
                               for _, d in DILATED_PATTERNS]
                            + [pltpu.VMEM((s, LANES), F32) for _ in range(9)]),
        ),
        out_shape=jax.ShapeDtypeStruct((n, ATT_WIDTH), BF16),
        compiler_params=_cparams(("arbitrary", "arbitrary")),
        name="attn",
    )(slopes, zqkv, zqkv, zqkv)


def _outproj_body(h_ref, mf_ref, ms_ref, oa_ref, ga_ref, w_ref, gf_ref, wr_ref,
                  h1_ref, m_ref, aff_ref):
    oa = _rms(oa_ref[...].astype(F32), ga_ref[...]).astype(BF16)
    o1, o2 = FN_WIDTH, FN_WIDTH + SSM_WIDTH
    acc = h_ref[...] + _dot(mf_ref[...], w_ref[0:o1, :]) + _dot(ms_ref[...], w_ref[o1:o2, :])
    acc = acc + _dot(oa, w_ref[o2:, :])
    h1_ref[...] = acc
    m = _rms(acc, gf_ref[...]).astype(BF16)
    bits = pltpu.bitcast(m.astype(F32), I32)
    m_ref[...] = lax.shift_right_logical(bits[:, :PACKED], 16) | (bits[:, PACKED:] & jnp.int32(-65536))
    lg = _dot(m, wr_ref[...])
    e = jnp.exp(lg - jnp.max(lg, axis=1, keepdims=True))
    aff_ref[...] = e / jnp.sum(e, axis=1, keepdims=True)


def _outproj(h, mf, ms, oa, ga, w, gf, wr):
    n = h.shape[0]
    tm = 512
    row = lambda i: (i, 0)
    fix = lambda i: (0, 0)
    return pl.pallas_call(
        _outproj_body,
        grid=(n // tm,),
        in_specs=[pl.BlockSpec((tm, D_MODEL), row), pl.BlockSpec((tm, FN_WIDTH), row),
                  pl.BlockSpec((tm, SSM_WIDTH), row), pl.BlockSpec((tm, ATT_WIDTH), row),
                  pl.BlockSpec((1, ATT_WIDTH), fix), pl.BlockSpec((D_MODEL, D_MODEL), fix),
                  pl.BlockSpec((1, D_MODEL), fix), pl.BlockSpec((D_MODEL, N_EXPERTS), fix)],
        out_specs=[pl.BlockSpec((tm, D_MODEL), row), pl.BlockSpec((tm, PACKED), row),
                   pl.BlockSpec((tm, N_EXPERTS), row)],
        out_shape=[jax.ShapeDtypeStruct((n, D_MODEL), F32),
                   jax.ShapeDtypeStruct((n, PACKED), I32),
                   jax.ShapeDtypeStruct((n, N_EXPERTS), F32)],
        compiler_params=_cparams(("arbitrary",)),
        name="outproj",
    )(h, mf, ms, oa, ga, w, gf, wr)


def _select_body(aff_ref, rank_ref, off_ref, thr_ref, *, cap, nb):
    r128 = lax.broadcasted_iota(I32, (LANES, LANES), 0)
    c128 = lax.broadcasted_iota(I32, (LANES, LANES), 1)
    upper_incl = jnp.where(r128 <= c128, 1.0, 0.0).astype(BF16)
    ones = jnp.ones((LANES, LANES), BF16)
    rb = lax.broadcasted_iota(I32, (nb, nb), 0)
    cb = lax.broadcasted_iota(I32, (nb, nb), 1)
    lower_strict = jnp.where(cb < rb, 1.0, 0.0).astype(BF16)
    upper_strict = jnp.where(rb < cb, 1.0, 0.0).astype(BF16)
    ones8 = jnp.ones((8, LANES), BF16)

    def count(mask):
        c = jnp.sum(jnp.where(mask, 1.0, 0.0), axis=0, keepdims=True)
        return jnp.sum(c, axis=1, keepdims=True)

    bits_all = pltpu.bitcast(aff_ref[...], I32)

    def bitstep(i, t):
        cand = t | jnp.left_shift(jnp.int32(1), 30 - i)
        above = jnp.sum(jnp.where(bits_all >= cand, 1.0, 0.0), axis=1, keepdims=True)
        return jnp.where(jnp.sum(above, axis=2, keepdims=True) >= cap, cand, t)

    thr = lax.fori_loop(0, 31, bitstep, jnp.zeros((N_EXPERTS, 1, 1), I32))
    thr_ref[...] = jnp.broadcast_to(thr, thr_ref.shape)

    def prefix(mask):
        mb = jnp.where(mask, 1.0, 0.0).astype(BF16)
        incl = _dot(mb, upper_incl)
        tot = _dot(mb, ones)
        offs = _dot(lower_strict, tot.astype(BF16))
        return offs + incl - 1.0, mb

    def per_expert(e, carry):
        bits = pltpu.bitcast(aff_ref[e], I32)
        t = thr_ref[e][0:1, 0:1]
        gt = bits > t
        eq = bits == t
        need = cap - count(gt)
        eq_rank, _ = prefix(eq)
        sel = jnp.logical_or(gt, jnp.logical_and(eq, eq_rank < need))
        rank, mb = prefix(sel)
        rank_ref[e] = jnp.where(sel, rank.astype(I32), -1)
        tot_row = _dot_nt(ones8, mb)
        off_row = _dot(tot_row.astype(BF16), upper_strict)
        off_ref[pl.ds(e, 1), :] = off_row[0:1, :].astype(I32)
        return carry

    lax.fori_loop(0, N_EXPERTS, per_expert, 0)


def _select(aff_t, cap):
    n = aff_t.shape[1]
    nb = n // LANES
    a3 = aff_t.reshape(N_EXPERTS, nb, LANES)
    rank, off = pl.pallas_call(
        functools.partial(_select_body, cap=cap, nb=nb),
        grid=(1,),
        in_specs=[pl.BlockSpec((N_EXPERTS, nb, LANES), lambda i: (0, 0, 0))],
        out_specs=[pl.BlockSpec((N_EXPERTS, nb, LANES), lambda i: (0, 0, 0)),
                   pl.BlockSpec((N_EXPERTS, nb), lambda i: (0, 0))],
        out_shape=[jax.ShapeDtypeStruct((N_EXPERTS, nb, LANES), I32),
                   jax.ShapeDtypeStruct((N_EXPERTS, nb), I32)],
        scratch_shapes=[pltpu.VMEM((N_EXPERTS, 8, LANES), I32)],
        compiler_params=_cparams(("arbitrary",)),
        name="select",
    )(a3)
    return rank.reshape(N_EXPERTS, n), off


def _ffn_body(x_ref, wg_ref, wu_ref, wd_ref, o_ref):
    w = x_ref[...]
    x = jnp.concatenate([pltpu.bitcast(lax.shift_left(w, 16), F32),
                         pltpu.bitcast(w & jnp.int32(-65536), F32)], axis=1).astype(BF16)
    tf = 512
    acc = jnp.zeros(o_ref.shape, F32)
    for j in range(D_FF_EXPERT // tf):
        g = _dot(x, wg_ref[:, j * tf:(j + 1) * tf])
        u = _dot(x, wu_ref[:, j * tf:(j + 1) * tf])
        hdn = (g * jax.nn.sigmoid(g) * u).astype(BF16)
        acc = acc + _dot(hdn, wd_ref[j * tf:(j + 1) * tf, :])
    o_ref[...] = acc.astype(BF16)


def _ffn(xe, wg, wu, wd, layer):
    e, cap, _ = xe.shape
    tm = min(FFN_ROWS, cap)
    return pl.pallas_call(
        _ffn_body,
        grid=(e, cap // tm),
        in_specs=[pl.BlockSpec((None, tm, PACKED), lambda i, j: (i, j, 0)),
                  pl.BlockSpec((None, None, D_MODEL, D_FF_EXPERT), lambda i, j: (layer, i, 0, 0)),
                  pl.BlockSpec((None, None, D_MODEL, D_FF_EXPERT), lambda i, j: (layer, i, 0, 0)),
                  pl.BlockSpec((None, None, D_FF_EXPERT, D_MODEL), lambda i, j: (layer, i, 0, 0))],
        out_specs=pl.BlockSpec((None, tm, D_MODEL), lambda i, j: (i, j, 0)),
        out_shape=jax.ShapeDtypeStruct((e, cap, D_MODEL), BF16),
        compiler_params=_cparams(("arbitrary", "arbitrary")),
        name="ffn",
    )(xe, wg, wu, wd)


def _combine_body(off_ref, h_ref, rank_ref, aff_ref, p_ref, gp_ref, wg_ref, wp_ref, gfin_ref, ye_ref,
                  o_ref, stack, sems, *, cap, last):
    t = pl.program_id(0)
    nt = pl.num_programs(0)
    tile, win, nsub = COMBINE_TILE, COMBINE_WIN, COMBINE_SUB
    bpt = tile // LANES
    kdim = N_EXPERTS * win

    def base(tt, e):
        return (off_ref[e, tt * bpt] // 16) * 16

    def copy(e, start, slot, sub):
        return pltpu.make_async_copy(ye_ref.at[e, pl.ds(pl.multiple_of(start, 16), win), :],
                                     stack.at[slot, sub, pl.ds(e * win, win), :], sems.at[slot, sub, e])

    def starts(tt, r):
        want = [base(tt, e) + r * win for e in range(N_EXPERTS)]
        return want, [jnp.minimum(w, cap - win) for w in want]

    def issue(tt, r, slot, sub):
        _, got = starts(tt, r)
        for e in range(N_EXPERTS):
            copy(e, got[e], slot, sub).start()

    def wait(tt, r, slot, sub):
        _, got = starts(tt, r)
        for e in range(N_EXPERTS):
            copy(e, got[e], slot, sub).wait()

    slot = t % 2

    @pl.when(t == 0)
    def _():
        for sub in range(nsub):
            issue(sub, 0, 0, sub)

    @pl.when(t + 1 < nt)
    def _():
        for sub in range(nsub):
            issue((t + 1) * nsub + sub, 0, 1 - slot, sub)

    lane16 = lax.broadcasted_iota(I32, (1, N_EXPERTS), 1)
    er = lax.broadcasted_iota(I32, (N_EXPERTS, kdim), 0)
    ec = lax.broadcasted_iota(I32, (N_EXPERTS, kdim), 1)
    expand = jnp.where(ec // win == er, 1.0, 0.0).astype(BF16)
    lane_in = (lax.broadcasted_iota(I32, (1, kdim), 1) % win).astype(F32)

    def row_of(vals):
        r = jnp.zeros((1, N_EXPERTS), I32)
        for e in range(N_EXPERTS):
            r = jnp.where(lane16 == e, vals[e], r)
        return r

    def accumulate(sub, r):
        rows = pl.ds(sub * tile, tile)
        want, got = starts(t * nsub + sub, r)
        wrow, grow = row_of(want), row_of(got)
        rk = rank_ref[rows, :]
        ok = jnp.logical_and(rk >= wrow, rk < grow + win)
        rel = jnp.where(ok, rk - grow, -1).astype(F32).astype(BF16)
        hit = _dot(rel, expand) == lane_in
        gates = _dot(aff_ref[rows, :].astype(BF16), expand)
        w = jnp.where(hit, gates, 0.0).astype(BF16)
        return _dot(w, stack[slot, sub])

    for sub in range(nsub):
        wait(t * nsub + sub, 0, slot, sub)
    for sub in range(nsub):
        rows = pl.ds(sub * tile, tile)
        o_ref[rows, :] = h_ref[rows, :] + accumulate(sub, 0)

    for sub in range(nsub):
        tt = t * nsub + sub
        nr = jnp.int32(1)
        for e in range(N_EXPERTS):
            span = off_ref[e, (tt + 1) * bpt] - base(tt, e)
            nr = jnp.maximum(nr, (span + win - 1) // win)

        def extra(r, carry, sub=sub, tt=tt):
            issue(tt, r, slot, sub)
            wait(tt, r, slot, sub)
            o_ref[pl.ds(sub * tile, tile), :] += accumulate(sub, r)
            return carry

        lax.fori_loop(1, nr, extra, 0)

    for sub in range(nsub):
        rows = pl.ds(sub * tile, tile)
        h2 = o_ref[rows, :]
        gate = jax.nn.sigmoid(_dot(_rms(h2, gp_ref[...]).astype(BF16), wg_ref[...]))
        h3 = h2 + _dot(p_ref[rows, :].astype(BF16), wp_ref[...]) * gate
        o_ref[rows, :] = _rms(h3, gfin_ref[...]) if last else h3


def _combine(off, h1, rank_tok, aff_tok, p, layer, gp, wg, wp, gfin, ye, cap, last):
    n = h1.shape[0]
    tile = COMBINE_TILE * COMBINE_SUB
    row = lambda i, o: (i, 0)
    fix = lambda i, o: (0, 0)
    prow = lambda i, o: (layer * (n // tile) + i, 0)
    return pl.pallas_call(
        functools.partial(_combine_body, cap=cap, last=last),
        grid_spec=pltpu.PrefetchScalarGridSpec(
            num_scalar_prefetch=1,
            grid=(n // tile,),
            in_specs=[pl.BlockSpec((tile, D_MODEL), row),
                      pl.BlockSpec((tile, N_EXPERTS), row),
                      pl.BlockSpec((tile, N_EXPERTS), row),
                      pl.BlockSpec((tile, PLE_DIM), prow),
                      pl.BlockSpec((1, D_MODEL), fix),
                      pl.BlockSpec((D_MODEL, D_MODEL), fix),
                      pl.BlockSpec((PLE_DIM, D_MODEL), fix),
                      pl.BlockSpec((1, D_MODEL), fix),
                      pl.BlockSpec(memory_space=pl.ANY)],
            out_specs=pl.BlockSpec((tile, D_MODEL), row),
            scratch_shapes=[pltpu.VMEM((2, COMBINE_SUB, N_EXPERTS * COMBINE_WIN, D_MODEL), BF16),
                            pltpu.SemaphoreType.DMA((2, COMBINE_SUB, N_EXPERTS))],
        ),
        out_shape=jax.ShapeDtypeStruct((n, D_MODEL), F32),
        compiler_params=_cparams(("arbitrary",)),
        name="combine",
    )(off, h1, rank_tok, aff_tok, p, gp, wg, wp, gfin, ye)


def _gather_rows(table, rank, cap):
    n, w = table.shape
    workers_per_expert = SC_CORES * SC_SUBCORES // N_EXPERTS
    per = cap // workers_per_expert
    nch = per // GATHER_ROWS
    mesh = plsc.VectorSubcoreMesh(core_axis_name="c", subcore_axis_name="s",
                                  num_cores=SC_CORES, num_subcores=SC_SUBCORES)
    cp = dataclasses.replace(pltpu.CompilerParams(), needs_layout_passes=False)

    @functools.partial(
        pl.kernel, mesh=mesh, compiler_params=cp,
        out_type=jax.ShapeDtypeStruct((N_EXPERTS * cap, w), I32),
        scratch_types=[pltpu.VMEM((RANK_CHUNK,), I32),
                       pltpu.VMEM((nch, GATHER_ROWS), I32),
                       pltpu.VMEM((GATHER_ROWS, w), I32),
                       pltpu.SemaphoreType.DMA],
        name="sc_gather")
    def gather(table_hbm, rank_hbm, out_hbm, rbuf, idx, rows, sem):
        wid = lax.axis_index("s") * SC_CORES + lax.axis_index("c")
        e = wid // workers_per_expert
        lo = (wid % workers_per_expert) * per
        lane = lax.iota(I32, SC_LANES)

        @pl.loop(0, n // RANK_CHUNK)
        def _(c):
            pltpu.sync_copy(rank_hbm.at[e, pl.ds(c * RANK_CHUNK, RANK_CHUNK)], rbuf)

            @pl.loop(0, RANK_CHUNK // SC_LANES)
            def _(i):
                rel = rbuf[pl.ds(i * SC_LANES, SC_LANES)] - lo
                mask = jnp.logical_and(rel >= 0, rel < per)
                rel = jnp.where(mask, rel, 0)
                tok = c * RANK_CHUNK + i * SC_LANES + lane
                plsc.store_scatter(idx, [rel // GATHER_ROWS, rel % GATHER_ROWS], tok, mask=mask)

        @pl.loop(0, nch)
        def _(c):
            pltpu.async_copy(table_hbm.at[idx.at[c]], rows, sem).wait()
            pltpu.sync_copy(rows, out_hbm.at[pl.ds(e * cap + lo + c * GATHER_ROWS, GATHER_ROWS)])

    return gather(table, rank).reshape(N_EXPERTS, cap, w)


def _trunk(x, p, prm, b, s):
    n = b * s
    nc = s // SSM_CHUNK
    cap = EC_CAPACITY_FACTOR * n // N_EXPERTS
    cs, ss = _dft_tables(s)
    cs, ss = cs.astype(BF16), ss.astype(BF16)
    h = x.reshape(n, D_MODEL)
    depth = prm['w_in'].shape[0]
    p_rows = p.reshape(depth * n, PLE_DIM)
    for l in range(depth):
        lp = prm['layers'][l]
        zf, zs, zqkv = _inproj(h, lp['g_mix'], lp['w_in'])
        mf = _fourier(zf, cs, ss, prm['ccb'], prm['scb'], lp['w_fnet'], lp['g_f'], b, s)
        xl = _s5_in(zs, lp['s5_w_in'])
        pr, pi = _scan_powers(lp['s5_a_chunk'], int(math.log2(nc)))
        st = _s5_scan(xl, pr, pi, b, nc)
        ms = _s5_out(zs, st, lp['s5_m_intra'], lp['s5_w_out'], lp['s5_d'], lp['s5_glu_w'], lp['s5_glu_b'],
                     lp['g_s'])
        oa = _attn(zqkv, prm['slopes'], b, s)
        h1, m, aff = _outproj(h, mf, ms, oa, lp['g_a'], lp['w_out'], lp['g_ffn'], lp['w_router'])
        rank, off = _select(aff.T, cap)
        off = jnp.concatenate([off, jnp.full((N_EXPERTS, 1), cap, I32)], axis=1)
        xe = _gather_rows(m, rank, cap)
        ye = _ffn(xe, prm['w_gate'], prm['w_up'], prm['w_down'], l)
        h = _combine(off, h1, rank.T, aff, p_rows, l, lp['g_ple'], lp['w_ple_gate'],
                     lp['w_ple_proj'], prm['g_final'], ye, cap, l == depth - 1)
    return h.reshape(b, s, D_MODEL)


def kernel(x_prompt, x_sample, p_prompt, p_sample, norm_mix, w_in, w_fnet, ssm_a_re, ssm_a_im, ssm_log_dt, ssm_b_re, ssm_b_im, ssm_c_re, ssm_c_im, ssm_d, ssm_glu_w, ssm_glu_b, norm_branch, w_out, norm_ffn, w_router, w_exp_gate, w_exp_up, w_exp_down, norm_ple, w_ple_gate, w_ple_proj, norm_final):
    depth = w_in.shape[0]
    o1, o2 = FN_WIDTH, FN_WIDTH + SSM_WIDTH
    row = lambda v: v.reshape(1, -1).astype(F32)
    cc, sc = _dft_tables(FN_HEAD_DIM)
    tile4 = lambda mtx: _block_diag(jnp.broadcast_to(mtx[None], (FN_HEADS,) + mtx.shape)).astype(BF16)
    layers = []
    for l in range(depth):
        m_intra, s5_w_in, s5_w_out, a_chunk = _s5_matrices(
            ssm_a_re[l], ssm_a_im[l], ssm_log_dt[l], ssm_b_re[l], ssm_b_im[l], ssm_c_re[l], ssm_c_im[l])
        layers.append(dict(
            g_mix=row(norm_mix[l]), w_in=w_in[l].astype(BF16),
            w_fnet=_block_diag(w_fnet[l]).astype(BF16),
            g_f=row(norm_branch[l][:o1]), g_s=row(norm_branch[l][o1:o2]), g_a=row(norm_branch[l][o2:]),
            s5_m_intra=m_intra, s5_w_in=s5_w_in, s5_w_out=s5_w_out, s5_a_chunk=a_chunk,
            s5_d=row(ssm_d[l]), s5_glu_w=ssm_glu_w[l].astype(BF16), s5_glu_b=row(ssm_glu_b[l]),
            w_out=w_out[l].astype(BF16), g_ffn=row(norm_ffn[l]),
            w_router=w_router[l].astype(BF16),
            g_ple=row(norm_ple[l]), w_ple_gate=w_ple_gate[l].astype(BF16),
            w_ple_proj=w_ple_proj[l].astype(BF16)))
    slopes = jnp.asarray([2.0 ** (-8.0 * (i + 1) / ATT_HEADS) for i in range(ATT_HEADS)], F32)
    prm = dict(w_in=w_in, layers=layers, ccb=tile4(cc), scb=tile4(sc), slopes=slopes,
               w_gate=w_exp_gate.astype(BF16), w_up=w_exp_up.astype(BF16), w_down=w_exp_down.astype(BF16),
               g_final=row(norm_final))
    bp, sp = x_prompt.shape[0], x_prompt.shape[1]
    bs, ssq = x_sample.shape[0], x_sample.shape[1]
    y_prompt = _trunk(x_prompt, p_prompt, prm, bp, sp)
    y_sample = _trunk(x_sample, p_sample, prm, bs, ssq)
    return (y_prompt, y_sample)
```

```python
import dataclasses
import functools
import math

import jax
import jax.numpy as jnp
from jax import lax
from jax.experimental import pallas as pl
from jax.experimental.pallas import tpu as pltpu
from jax.experimental.pallas import tpu_sc as plsc

D_MODEL = 1024
FN_WIDTH = 256
FN_HEADS = 4
FN_HEAD_DIM = 64
SSM_WIDTH = 256
SSM_GROUP = 16
SSM_GROUPS = 16
SSM_STATE = 64
ATT_WIDTH = 512
ATT_HEAD_DIM = 64
ATT_HEADS = 8
DILATED_PATTERNS = ((128, 1), (512, 4), (2048, 16))
IN_PROJ_WIDTH = 2048
N_EXPERTS = 16
EC_CAPACITY_FACTOR = 2
D_FF_EXPERT = 2048
PLE_DIM = 256
RMS_EPS = 1e-6
NEG_INF = -1e30

LANES = 128
SSM_CHUNK = 8
SSM_ROW = SSM_CHUNK * SSM_WIDTH
SSM_NSTATE = SSM_GROUPS * SSM_STATE
S5_ROWS = 256
ATT_HALF = 64
ATT_UNROLL = 8
ATT_SPLIT = 4
ATT_OFFSETS = 3
FFN_ROWS = 1024
COMBINE_TILE = 256
COMBINE_SUB = 2
COMBINE_WIN = 64
VMEM_LIMIT = 56 * 1024 * 1024
SC_CORES = 2
SC_SUBCORES = 16
SC_LANES = 16
GATHER_ROWS = 64
RANK_CHUNK = 2048
PACKED = D_MODEL // 2

F32 = jnp.float32
BF16 = jnp.bfloat16
I32 = jnp.int32


def _cparams(sem):
    return pltpu.CompilerParams(dimension_semantics=sem, vmem_limit_bytes=VMEM_LIMIT)


def _rms(x, g):
    return x * lax.rsqrt(jnp.mean(x * x, axis=-1, keepdims=True) + RMS_EPS) * g


def _dot(a, b):
    return jnp.dot(a, b, preferred_element_type=F32)


def _dot_nt(a, b):
    return lax.dot_general(a, b, (((1,), (1,)), ((), ())), preferred_element_type=F32)


def _inproj_body(h_ref, g_ref, w_ref, zf_ref, zs_ref, zqkv_ref):
    a = _rms(h_ref[...], g_ref[...]).astype(BF16)
    z = _dot(a, w_ref[...])
    zf_ref[...] = z[:, :FN_WIDTH]
    zs_ref[...] = z[:, FN_WIDTH:FN_WIDTH + SSM_WIDTH]
    zqkv_ref[...] = z[:, FN_WIDTH + SSM_WIDTH:].astype(BF16)


def _inproj(h, g, w):
    n = h.shape[0]
    tm = 512
    return pl.pallas_call(
        _inproj_body,
        grid=(n // tm,),
        in_specs=[pl.BlockSpec((tm, D_MODEL), lambda i: (i, 0)),
                  pl.BlockSpec((1, D_MODEL), lambda i: (0, 0)),
                  pl.BlockSpec((D_MODEL, IN_PROJ_WIDTH), lambda i: (0, 0))],
        out_specs=[pl.BlockSpec((tm, FN_WIDTH), lambda i: (i, 0)),
                   pl.BlockSpec((tm, SSM_WIDTH), lambda i: (i, 0)),
                   pl.BlockSpec((tm, 3 * ATT_WIDTH), lambda i: (i, 0))],
        out_shape=[jax.ShapeDtypeStruct((n, FN_WIDTH), F32),
                   jax.ShapeDtypeStruct((n, SSM_WIDTH), F32),
                   jax.ShapeDtypeStruct((n, 3 * ATT_WIDTH), BF16)],
        compiler_params=_cparams(("arbitrary",)),
        name="inproj",
    )(h, g, w)


def _fourier_body(xa_ref, xb_ref, cs_ref, ss_ref, tc_ref, ts_ref, cc_ref, sc_ref, wb_ref, g_ref, o_ref, *, s):
    half = s // 2
    def parity(p):
        return jnp.concatenate([xa_ref[0, pl.ds(p, half, stride=2), :],
                                xb_ref[0, pl.ds(p, half, stride=2), :]], axis=1)
    x2 = jnp.concatenate([parity(0), parity(1)], axis=1).astype(BF16)
    y = _dot(cs_ref[...], x2)
    z = _dot(ss_ref[...], x2)
    ec, oc = y[:, :FN_WIDTH], y[:, FN_WIDTH:]
    es, os_ = z[:, :FN_WIDTH], z[:, FN_WIDTH:]
    c = jnp.concatenate([tc_ref[...]] * (FN_WIDTH // LANES), axis=1)
    sn = jnp.concatenate([ts_ref[...]] * (FN_WIDTH // LANES), axis=1)
    rc = c * oc - sn * os_
    rs = c * os_ + sn * oc
    for part, (yy, zz) in enumerate(((ec + rc, es + rs), (ec - rc, es - rs))):
        f = _dot(yy.astype(BF16), cc_ref[...]) - _dot(zz.astype(BF16), sc_ref[...])
        o = _dot(f.astype(BF16), wb_ref[...])
        o_ref[0, part] = _rms(o, g_ref[...]).astype(BF16)


def _dft_tables(n):
    def exact(rows):
        k = (rows[:, None] * jnp.arange(n, dtype=I32)[None, :]) % n
        ang = k.astype(F32) * (2.0 * math.pi / n)
        return jnp.cos(ang), jnp.sin(ang)
    if n <= LANES:
        return exact(jnp.arange(n, dtype=I32))
    ca, sa = exact(LANES * jnp.arange(n // LANES, dtype=I32))
    cb, sb = exact(jnp.arange(LANES, dtype=I32))
    cos = ca[:, None, :] * cb[None] - sa[:, None, :] * sb[None]
    sin = sa[:, None, :] * cb[None] + ca[:, None, :] * sb[None]
    return cos.reshape(n, n), sin.reshape(n, n)


def _fourier(zf, tables, ccb, scb, wb, g, b, s):
    cs, ss, tc, ts = tables
    half = s // 2
    tr = 512
    x = zf.reshape(b, s, FN_WIDTH)
    fix = lambda i, j: (0, 0)
    out = pl.pallas_call(
        functools.partial(_fourier_body, s=s),
        grid=(half // tr, b),
        in_specs=[pl.BlockSpec((1, s, LANES), lambda i, j: (j, 0, 0)),
                  pl.BlockSpec((1, s, LANES), lambda i, j: (j, 0, 1)),
                  pl.BlockSpec((tr, half), lambda i, j: (i, 0)),
                  pl.BlockSpec((tr, half), lambda i, j: (i, 0)),
                  pl.BlockSpec((tr, LANES), lambda i, j: (i, 0)),
                  pl.BlockSpec((tr, LANES), lambda i, j: (i, 0)),
                  pl.BlockSpec((FN_WIDTH, FN_WIDTH), fix),
                  pl.BlockSpec((FN_WIDTH, FN_WIDTH), fix),
                  pl.BlockSpec((FN_WIDTH, FN_WIDTH), fix),
                  pl.BlockSpec((1, FN_WIDTH), fix)],
        out_specs=pl.BlockSpec((1, 2, tr, FN_WIDTH), lambda i, j: (j, 0, i, 0)),
        out_shape=jax.ShapeDtypeStruct((b, 2, half, FN_WIDTH), BF16),
        compiler_params=_cparams(("arbitrary", "arbitrary")),
        name="fourier",
    )(x, x, cs, ss, tc, ts, ccb, scb, wb, g)
    return out.reshape(b * s, FN_WIDTH)


def _fourier_tables(s):
    cs, ss = _dft_tables(s // 2)
    ang = jnp.arange(s // 2, dtype=F32) * (2.0 * math.pi / s)
    wide = lambda v: jnp.broadcast_to(v[:, None], (s // 2, LANES))
    return cs.astype(BF16), ss.astype(BF16), wide(jnp.cos(ang)), wide(jnp.sin(ang))


def _block_diag(blocks):
    h, a, bb = blocks.shape
    eye = jnp.eye(h, dtype=blocks.dtype)
    return jnp.einsum('hab,hg->hagb', blocks, eye).reshape(h * a, h * bb)


def _s5_matrices(a_re, a_im, log_dt, b_re, b_im, c_re, c_im):
    t = SSM_CHUNK
    g, p, c = SSM_GROUPS, SSM_STATE, SSM_GROUP
    lam = lax.complex(a_re.astype(F32), a_im.astype(F32))
    dt = jnp.exp(log_dt.astype(F32))[..., None]
    abar = jnp.exp(lam * dt)
    bbar = ((abar - 1.0) / lam)[..., None] * lax.complex(b_re.astype(F32), b_im.astype(F32))
    cmat = lax.complex(c_re.astype(F32), c_im.astype(F32))
    ks = jnp.arange(t + 1, dtype=F32)
    apow = jnp.exp((lam * dt)[:, None] * ks[None, :, None, None])

    kern = jnp.real(jnp.einsum('dgcp,dkgp,dgpe->dkgce', cmat, apow[:, :t], bbar))
    gh, hw, hs = g // 2, SSM_WIDTH // 2, SSM_NSTATE // 2
    lags = jnp.arange(-(t - 1), t)
    pick = lambda m: m[:, None, None, None]
    klag = (jnp.where(pick(lags >= 0), kern[0][jnp.clip(lags, 0, t - 1)], 0.0)
            + jnp.where(pick(lags <= 0), kern[1][jnp.clip(-lags, 0, t - 1)], 0.0))
    eye_h = jnp.eye(gh, dtype=F32)
    lag = jnp.arange(t)[None, :] - jnp.arange(t)[:, None]
    def _intra(h):
        blocks = jnp.einsum('lgce,gh->lgehc', klag[:, h * gh:(h + 1) * gh], eye_h).reshape(2 * t - 1, hw, hw)
        return blocks[lag + t - 1].transpose(0, 2, 1, 3).reshape(t * hw, t * hw)
    m_intra = jnp.stack([_intra(0), _intra(1)])

    wf = apow[0, t - 1 - jnp.arange(t)][:, :, :, None] * bbar[0][None]
    wb = apow[1, jnp.arange(t)][:, :, :, None] * bbar[1][None]
    same_group = ((jnp.arange(t * hw)[:, None] // c) % gh) == (jnp.arange(hs)[None, :] // p)
    def _spread(x, h):
        xh = x[:, h * gh:(h + 1) * gh].reshape(t * hw, p)
        return jnp.where(same_group, jnp.tile(xh, (1, gh)), 0.0)
    def _parts(parts, h):
        return jnp.concatenate([sign * _spread(x, h) for sign, x in parts], axis=1)
    wf, wb = jnp.swapaxes(wf, 2, 3), jnp.swapaxes(wb, 2, 3)
    in_parts = [(1.0, jnp.real(wf)), (1.0, jnp.imag(wf)), (1.0, jnp.real(wb)), (1.0, jnp.imag(wb))]
    w_in = jnp.stack([_parts(in_parts, 0), _parts(in_parts, 1)])

    qf = cmat[0][None] * apow[0, 1 + jnp.arange(t)][:, :, None, :]
    qb = cmat[1][None] * apow[1, t - jnp.arange(t)][:, :, None, :]
    out_parts = [(1.0, jnp.real(qf)), (-1.0, jnp.imag(qf)), (1.0, jnp.real(qb)), (-1.0, jnp.imag(qb))]
    w_out = jnp.stack([_parts(out_parts, 0).T, _parts(out_parts, 1).T])

    a_chunk = lam * dt * t
    return m_intra.astype(BF16), w_in.astype(BF16), w_out.astype(BF16), a_chunk


def _scan_powers(a_chunk, nsteps):
    e = jnp.exp(a_chunk[None] * (2.0 ** jnp.arange(nsteps, dtype=F32))[:, None, None, None])
    e = e.reshape(nsteps, 2 * SSM_NSTATE)
    return jnp.real(e), jnp.imag(e)


def _half_steps(z_ref, tr):
    return jnp.concatenate([z_ref[pl.ds(r, tr, stride=SSM_CHUNK), :] for r in range(SSM_CHUNK)], axis=1)


def _halves(rows):
    return [pl.BlockSpec((rows, LANES), lambda i: (i, 0)), pl.BlockSpec((rows, LANES), lambda i: (i, 1))]


def _s5_in_body(za_ref, zb_ref, w_ref, o_ref):
    tr = o_ref.shape[0]
    hs = SSM_NSTATE // 2
    for h, z_ref in enumerate((za_ref, zb_ref)):
        res = _dot(_half_steps(z_ref, tr).astype(BF16), w_ref[h])
        for part in range(4):
            lo = part * SSM_NSTATE + h * hs
            o_ref[:, lo:lo + hs] = res[:, part * hs:(part + 1) * hs]


def _s5_in(zs, w_in):
    rows = zs.shape[0] // SSM_CHUNK
    tr = min(S5_ROWS, rows)
    width = 4 * SSM_NSTATE
    return pl.pallas_call(
        _s5_in_body,
        grid=(rows // tr,),
        in_specs=_halves(tr * SSM_CHUNK) + [
            pl.BlockSpec((2, SSM_ROW // 2, width // 2), lambda i: (0, 0, 0), pipeline_mode=pl.Buffered(1))],
        out_specs=pl.BlockSpec((tr, width), lambda i: (i, 0)),
        out_shape=jax.ShapeDtypeStruct((rows, width), F32),
        compiler_params=_cparams(("arbitrary",)),
        name="s5_in",
    )(zs, zs, w_in)


def _s5_scan_body(x_ref, pr_ref, pi_ref, o_ref, *, nc, nsteps):
    ns = SSM_NSTATE
    x = x_ref[0]
    fre, fim = x[:, 0:ns], x[:, ns:2 * ns]
    bre, bim = x[:, 2 * ns:3 * ns], x[:, 3 * ns:4 * ns]
    row = lax.broadcasted_iota(I32, (nc, 1), 0)
    for k in range(nsteps):
        sh = 2 ** k
        far, fai = pr_ref[k:k + 1, 0:ns], pi_ref[k:k + 1, 0:ns]
        bar, bai = pr_ref[k:k + 1, ns:2 * ns], pi_ref[k:k + 1, ns:2 * ns]
        fmask = row >= sh
        sre = jnp.where(fmask, pltpu.roll(fre, sh, 0), 0.0)
        sim = jnp.where(fmask, pltpu.roll(fim, sh, 0), 0.0)
        fre, fim = fre + far * sre - fai * sim, fim + far * sim + fai * sre
        bmask = row < nc - sh
        sre = jnp.where(bmask, pltpu.roll(bre, nc - sh, 0), 0.0)
        sim = jnp.where(bmask, pltpu.roll(bim, nc - sh, 0), 0.0)
        bre, bim = bre + bar * sre - bai * sim, bim + bar * sim + bai * sre
    fmask = row >= 1
    bmask = row < nc - 1
    o_ref[0, :, 0:ns] = jnp.where(fmask, pltpu.roll(fre, 1, 0), 0.0).astype(BF16)
    o_ref[0, :, ns:2 * ns] = jnp.where(fmask, pltpu.roll(fim, 1, 0), 0.0).astype(BF16)
    o_ref[0, :, 2 * ns:3 * ns] = jnp.where(bmask, pltpu.roll(bre, nc - 1, 0), 0.0).astype(BF16)
    o_ref[0, :, 3 * ns:4 * ns] = jnp.where(bmask, pltpu.roll(bim, nc - 1, 0), 0.0).astype(BF16)


def _s5_scan(xl, pr, pi, b, nc):
    nsteps = int(math.log2(nc))
    width = 4 * SSM_NSTATE
    x = xl.reshape(b, nc, width)
    out = pl.pallas_call(
        functools.partial(_s5_scan_body, nc=nc, nsteps=nsteps),
        grid=(b,),
        in_specs=[pl.BlockSpec((1, nc, width), lambda i: (i, 0, 0)),
                  pl.BlockSpec((nsteps, 2 * SSM_NSTATE), lambda i: (0, 0)),
                  pl.BlockSpec((nsteps, 2 * SSM_NSTATE), lambda i: (0, 0))],
        out_specs=pl.BlockSpec((1, nc, width), lambda i: (i, 0, 0)),
        out_shape=jax.ShapeDtypeStruct((b, nc, width), BF16),
        compiler_params=_cparams(("arbitrary",)),
        name="s5_scan",
    )(x, pr, pi)
    return out.reshape(b * nc, width)


def _s5_out_body(za_ref, zb_ref, s_ref, m_ref, w_ref, d_ref, gw_ref, gb_ref, g_ref, o_ref, nat_a, nat_b):
    tr = s_ref.shape[0]
    hs = SSM_NSTATE // 2
    us, ys = [], []
    for h, z_ref in enumerate((za_ref, zb_ref)):
        u = _half_steps(z_ref, tr)
        st = jnp.concatenate([s_ref[:, part * SSM_NSTATE + h * hs:part * SSM_NSTATE + (h + 1) * hs]
                              for part in range(4)], axis=1)
        us.append(u)
        ys.append(_dot(st, w_ref[h]) + _dot(u.astype(BF16), m_ref[h]))
    c0 = math.sqrt(2.0 / math.pi)
    for r in range(SSM_CHUNK):
        step = slice(r * LANES, (r + 1) * LANES)
        y = jnp.concatenate([ys[0][:, step], ys[1][:, step]], axis=1)
        u = jnp.concatenate([us[0][:, step], us[1][:, step]], axis=1)
        v = y + d_ref[...] * u
        gl = 0.5 * v * (1.0 + jnp.tanh(c0 * (v + 0.044715 * (v * v * v))))
        gate = jax.nn.sigmoid(_dot(gl.astype(BF16), gw_ref[...]) + gb_ref[...])
        out = _rms(gl * gate, g_ref[...])
        nat_a[pl.ds(r, tr, stride=SSM_CHUNK), :] = out[:, :LANES]
        nat_b[pl.ds(r, tr, stride=SSM_CHUNK), :] = out[:, LANES:]
    o_ref[:, :LANES] = nat_a[...].astype(BF16)
    o_ref[:, LANES:] = nat_b[...].astype(BF16)


def _s5_out(zs, states, m_intra, w_out, d, glu_w, glu_b, g):
    rows = states.shape[0]
    tr = min(S5_ROWS, rows)
    width = 4 * SSM_NSTATE
    fix = lambda i: (0, 0)
    fix3 = lambda i: (0, 0, 0)
    once = pl.Buffered(1)
    return pl.pallas_call(
        _s5_out_body,
        grid=(rows // tr,),
        in_specs=_halves(tr * SSM_CHUNK) + [
                  pl.BlockSpec((tr, width), lambda i: (i, 0)),
                  pl.BlockSpec((2, SSM_ROW // 2, SSM_ROW // 2), fix3, pipeline_mode=once),
                  pl.BlockSpec((2, width // 2, SSM_ROW // 2), fix3, pipeline_mode=once),
                  pl.BlockSpec((1, SSM_WIDTH), fix), pl.BlockSpec((SSM_WIDTH, SSM_WIDTH), fix),
                  pl.BlockSpec((1, SSM_WIDTH), fix), pl.BlockSpec((1, SSM_WIDTH), fix)],
        out_specs=pl.BlockSpec((tr * SSM_CHUNK, SSM_WIDTH), lambda i: (i, 0)),
        out_shape=jax.ShapeDtypeStruct((rows * SSM_CHUNK, SSM_WIDTH), BF16),
        scratch_shapes=[pltpu.VMEM((tr * SSM_CHUNK, LANES), F32) for _ in range(SSM_WIDTH // LANES)],
        compiler_params=_cparams(("arbitrary",)),
        name="s5_out",
    )(zs, zs, states, m_intra, w_out, d, glu_w, glu_b, g)


def _attn_geometry(s, d):
    ln = s // d
    bq = min(128, ln)
    bk = min(bq + 2 * ATT_HALF, ln)
    return ln, bq, bk, ln // bq


def _attn_body(slope_ref, q_ref, k_ref, v_ref, o_ref, nat, qd, kd, vd, qdb, kdb, vdb, b0, b1, b2, *acc, s):
    hp = pl.program_id(1)
    s4 = s // ATT_SPLIT
    lane = lax.broadcasted_iota(I32, (1, LANES), 1)
    first = lane < ATT_HEAD_DIM
    second = jnp.logical_not(first)
    slopes = (slope_ref[2 * hp], slope_ref[2 * hp + 1])
    log2e = math.log2(math.e)
    scale = ATT_HEAD_DIM ** -0.5 * log2e

    @pl.when(pl.program_id(0) == 0)
    def _():
        for (_, d), bias in zip(DILATED_PATTERNS, (b0, b1, b2)):
            _, bq, bk, _ = _attn_geometry(s, d)
            jk = lax.broadcasted_iota(I32, (1, bk), 1)
            for o in range(ATT_OFFSETS):
                rel = jnp.abs(lax.broadcasted_iota(I32, (bq, 1), 0) + o * ATT_HALF - jk)
                dist = (d * rel).astype(F32) * log2e
                for hh in range(2):
                    bias[hp, 2 * o + hh] = jnp.where(rel <= ATT_HALF, -slopes[hh] * dist, NEG_INF)

    for src, dst_f, dst_b in ((q_ref, qd, qdb), (k_ref, kd, kdb), (v_ref, vd, vdb)):
        nat[...] = src[...].astype(F32)
        for c in range(ATT_SPLIT):
            x = nat[pl.ds(c, s4, stride=ATT_SPLIT), :]
            dst_f[c * s4:(c + 1) * s4, :] = x
            dst_b[c * s4:(c + 1) * s4, :] = x.astype(BF16)

    def pattern(p, d, refs, bias, locate, stride):
        acc_o, acc_m, acc_l = acc[3 * p], acc[3 * p + 1], acc[3 * p + 2]
        ln, bq, bk, nqb = _attn_geometry(s, d)

        def rows(c, j, size):
            start = locate(c, j)
            if stride == 1:
                return pl.ds(pl.multiple_of(start, ATT_HALF), size)
            return pl.ds(start, size, stride=stride)

        def scores(i):
            c = i // nqb
            j0 = (i % nqb) * bq
            ks = jnp.clip(j0 - ATT_HALF, 0, ln - bk)
            q = refs[0][rows(c, j0, bq), :].astype(BF16)
            k = refs[1][rows(c, ks, bk), :].astype(BF16)
            v = refs[2][rows(c, ks, bk), :].astype(BF16)
            off = (j0 - ks) // ATT_HALF
            scs = []
            for hh in range(2):
                qm = jnp.where(first if hh == 0 else second, q, jnp.zeros_like(q))
                scs.append(_dot_nt(qm, k) * scale + bias[hp, 2 * off + hh])
            return rows(c, j0, bq), v, scs

        def softmax(sc):
            m = jnp.max(sc, axis=-1, keepdims=True)
            pe = jnp.exp2(sc - m)
            return pe.astype(BF16), m, jnp.sum(pe, axis=-1, keepdims=True)

        def group(g, carry):
            staged = [scores(g * ATT_UNROLL + u) for u in range(ATT_UNROLL)]
            soft = [[softmax(sc) for sc in scs] for _, _, scs in staged]
            for (dst, v, _), ((p0, m0, l0), (p1, m1, l1)) in zip(staged, soft):
                acc_o[dst, :] = jnp.where(first, _dot(p0, v), _dot(p1, v))
                acc_m[dst, :] = jnp.where(first, m0, m1)
                acc_l[dst, :] = jnp.where(first, l0, l1)
            return carry

        lax.fori_loop(0, d * nqb // ATT_UNROLL, group, 0)

    (_, d1), (_, d2), (_, d3) = DILATED_PATTERNS
    assert d1 == 1 and d2 == ATT_SPLIT and d3 == ATT_SPLIT * ATT_SPLIT
    pattern(0, d1, (q_ref, k_ref, v_ref), b0, lambda c, j: j, 1)
    pattern(1, d2, (qdb, kdb, vdb), b1, lambda c, j: c * s4 + j, 1)
    pattern(2, d3, (qd, kd, vd), b2,
            lambda c, j: (c % ATT_SPLIT) * s4 + c // ATT_SPLIT + ATT_SPLIT * j, ATT_SPLIT)

    for c in range(ATT_SPLIT):
        part = pl.ds(c, s4, stride=ATT_SPLIT)
        blk = slice(c * s4, (c + 1) * s4)
        ms = (acc[1][part, :], acc[4][blk, :], acc[7][blk, :])
        os_ = (acc[0][part, :], acc[3][blk, :], acc[6][blk, :])
        ls = (acc[2][part, :], acc[5][blk, :], acc[8][blk, :])
        m = jnp.maximum(jnp.maximum(ms[0], ms[1]), ms[2])
        num = jnp.zeros((s4, LANES), F32)
        den = jnp.zeros((s4, LANES), F32)
        for p in range(3):
            w = jnp.exp2(ms[p] - m)
            num = num + w * os_[p]
            den = den + w * ls[p]
        nat[part, :] = num / den
    o_ref[...] = nat[...].astype(BF16)


def _attn(zqkv, slopes, b, s):
    n = b * s
    nhp = ATT_HEADS // 2
    col = lambda off: (lambda i, j, sl: (i, off + j))
    return pl.pallas_call(
        functools.partial(_attn_body, s=s),
        grid_spec=pltpu.PrefetchScalarGridSpec(
            num_scalar_prefetch=1,
            grid=(b, nhp),
            in_specs=[pl.BlockSpec((s, LANES), col(0)),
                      pl.BlockSpec((s, LANES), col(nhp)),
                      pl.BlockSpec((s, LANES), col(2 * nhp))],
            out_specs=pl.BlockSpec((s, LANES), lambda i, j, sl: (i, j)),
            scratch_shapes=([pltpu.VMEM((s, LANES), F32) for _ in range(4)]
                            + [pltpu.VMEM((s, LANES), BF16) for _ in range(3)]
                            + [pltpu.VMEM((nhp, 2 * ATT_OFFSETS) + _attn_geometry(s, d)[1:3], F32)
                               for _, d in DILATED_PATTERNS]
                            + [pltpu.VMEM((s, LANES), F32) for _ in range(9)]),
        ),
        out_shape=jax.ShapeDtypeStruct((n, ATT_WIDTH), BF16),
        compiler_params=_cparams(("arbitrary", "arbitrary")),
        name="attn",
    )(slopes, zqkv, zqkv, zqkv)


def _outproj_body(h_ref, mf_ref, ms_ref, oa_ref, ga_ref, w_ref, gf_ref, wr_ref,
                  h1_ref, m_ref, aff_ref):
    oa = _rms(oa_ref[...].astype(F32), ga_ref[...]).astype(BF16)
    o1, o2 = FN_WIDTH, FN_WIDTH + SSM_WIDTH
    acc = h_ref[...] + _dot(mf_ref[...], w_ref[0:o1, :]) + _dot(ms_ref[...], w_ref[o1:o2, :])
    acc = acc + _dot(oa, w_ref[o2:, :])
    h1_ref[...] = acc
    m = _rms(acc, gf_ref[...]).astype(BF16)
    bits = pltpu.bitcast(m.astype(F32), I32)
    m_ref[...] = lax.shift_right_logical(bits[:, :PACKED], 16) | (bits[:, PACKED:] & jnp.int32(-65536))
    lg = _dot(m, wr_ref[...])
    e = jnp.exp(lg - jnp.max(lg, axis=1, keepdims=True))
    aff_ref[...] = e / jnp.sum(e, axis=1, keepdims=True)


def _outproj(h, mf, ms, oa, ga, w, gf, wr):
    n = h.shape[0]
    tm = 512
    row = lambda i: (i, 0)
    fix = lambda i: (0, 0)
    return pl.pallas_call(
        _outproj_body,
        grid=(n // tm,),
        in_specs=[pl.BlockSpec((tm, D_MODEL), row), pl.BlockSpec((tm, FN_WIDTH), row),
                  pl.BlockSpec((tm, SSM_WIDTH), row), pl.BlockSpec((tm, ATT_WIDTH), row),
                  pl.BlockSpec((1, ATT_WIDTH), fix), pl.BlockSpec((D_MODEL, D_MODEL), fix),
                  pl.BlockSpec((1, D_MODEL), fix), pl.BlockSpec((D_MODEL, N_EXPERTS), fix)],
        out_specs=[pl.BlockSpec((tm, D_MODEL), row), pl.BlockSpec((tm, PACKED), row),
                   pl.BlockSpec((tm, N_EXPERTS), row)],
        out_shape=[jax.ShapeDtypeStruct((n, D_MODEL), F32),
                   jax.ShapeDtypeStruct((n, PACKED), I32),
                   jax.ShapeDtypeStruct((n, N_EXPERTS), F32)],
        compiler_params=_cparams(("arbitrary",)),
        name="outproj",
    )(h, mf, ms, oa, ga, w, gf, wr)


def _select_body(aff_ref, rank_ref, off_ref, thr_ref, *, cap, nb):
    r128 = lax.broadcasted_iota(I32, (LANES, LANES), 0)
    c128 = lax.broadcasted_iota(I32, (LANES, LANES), 1)
    upper_incl = jnp.where(r128 <= c128, 1.0, 0.0).astype(BF16)
    ones = jnp.ones((LANES, LANES), BF16)
    rb = lax.broadcasted_iota(I32, (nb, nb), 0)
    cb = lax.broadcasted_iota(I32, (nb, nb), 1)
    lower_strict = jnp.where(cb < rb, 1.0, 0.0).astype(BF16)
    upper_strict = jnp.where(rb < cb, 1.0, 0.0).astype(BF16)
    ones8 = jnp.ones((8, LANES), BF16)

    def count(mask):
        c = jnp.sum(jnp.where(mask, 1.0, 0.0), axis=0, keepdims=True)
        return jnp.sum(c, axis=1, keepdims=True)

    bits_all = pltpu.bitcast(aff_ref[...], I32)

    def bitstep(i, t):
        cand = t | jnp.left_shift(jnp.int32(1), 30 - i)
        above = jnp.sum(jnp.where(bits_all >= cand, 1.0, 0.0), axis=1, keepdims=True)
        return jnp.where(jnp.sum(above, axis=2, keepdims=True) >= cap, cand, t)

    thr = lax.fori_loop(0, 31, bitstep, jnp.zeros((N_EXPERTS, 1, 1), I32))
    thr_ref[...] = jnp.broadcast_to(thr, thr_ref.shape)

    def prefix(mask):
        mb = jnp.where(mask, 1.0, 0.0).astype(BF16)
        incl = _dot(mb, upper_incl)
        tot = _dot(mb, ones)
        offs = _dot(lower_strict, tot.astype(BF16))
        return offs + incl - 1.0, mb

    def per_expert(e, carry):
        bits = pltpu.bitcast(aff_ref[e], I32)
        t = thr_ref[e][0:1, 0:1]
        gt = bits > t
        eq = bits == t
        need = cap - count(gt)
        eq_rank, _ = prefix(eq)
        sel = jnp.logical_or(gt, jnp.logical_and(eq, eq_rank < need))
        rank, mb = prefix(sel)
        rank_ref[e] = jnp.where(sel, rank.astype(I32), -1)
        tot_row = _dot_nt(ones8, mb)
        off_row = _dot(tot_row.astype(BF16), upper_strict)
        off_ref[pl.ds(e, 1), :] = off_row[0:1, :].astype(I32)
        return carry

    lax.fori_loop(0, N_EXPERTS, per_expert, 0)


def _select(aff_t, cap):
    n = aff_t.shape[1]
    nb = n // LANES
    a3 = aff_t.reshape(N_EXPERTS, nb, LANES)
    rank, off = pl.pallas_call(
        functools.partial(_select_body, cap=cap, nb=nb),
        grid=(1,),
        in_specs=[pl.BlockSpec((N_EXPERTS, nb, LANES), lambda i: (0, 0, 0))],
        out_specs=[pl.BlockSpec((N_EXPERTS, nb, LANES), lambda i: (0, 0, 0)),
                   pl.BlockSpec((N_EXPERTS, nb), lambda i: (0, 0))],
        out_shape=[jax.ShapeDtypeStruct((N_EXPERTS, nb, LANES), I32),
                   jax.ShapeDtypeStruct((N_EXPERTS, nb), I32)],
        scratch_shapes=[pltpu.VMEM((N_EXPERTS, 8, LANES), I32)],
        compiler_params=_cparams(("arbitrary",)),
        name="select",
    )(a3)
    return rank.reshape(N_EXPERTS, n), off


def _ffn_body(x_ref, wg_ref, wu_ref, wd_ref, o_ref):
    w = x_ref[...]
    x = jnp.concatenate([pltpu.bitcast(lax.shift_left(w, 16), F32),
                         pltpu.bitcast(w & jnp.int32(-65536), F32)], axis=1).astype(BF16)
    tf = 512
    acc = jnp.zeros(o_ref.shape, F32)
    for j in range(D_FF_EXPERT // tf):
        g = _dot(x, wg_ref[:, j * tf:(j + 1) * tf])
        u = _dot(x, wu_ref[:, j * tf:(j + 1) * tf])
        hdn = (g * jax.nn.sigmoid(g) * u).astype(BF16)
        acc = acc + _dot(hdn, wd_ref[j * tf:(j + 1) * tf, :])
    o_ref[...] = acc.astype(BF16)


def _ffn(xe, wg, wu, wd, layer):
    e, cap, _ = xe.shape
    tm = min(FFN_ROWS, cap)
    return pl.pallas_call(
        _ffn_body,
        grid=(e, cap // tm),
        in_specs=[pl.BlockSpec((None, tm, PACKED), lambda i, j: (i, j, 0)),
                  pl.BlockSpec((None, None, D_MODEL, D_FF_EXPERT), lambda i, j: (layer, i, 0, 0)),
                  pl.BlockSpec((None, None, D_MODEL, D_FF_EXPERT), lambda i, j: (layer, i, 0, 0)),
                  pl.BlockSpec((None, None, D_FF_EXPERT, D_MODEL), lambda i, j: (layer, i, 0, 0))],
        out_specs=pl.BlockSpec((None, tm, D_MODEL), lambda i, j: (i, j, 0)),
        out_shape=jax.ShapeDtypeStruct((e, cap, D_MODEL), BF16),
        compiler_params=_cparams(("arbitrary", "arbitrary")),
        name="ffn",
    )(xe, wg, wu, wd)


def _combine_body(off_ref, h_ref, rank_ref, aff_ref, p_ref, gp_ref, wg_ref, wp_ref, gfin_ref, ye_ref,
                  o_ref, stack, sems, *, cap, last):
    t = pl.program_id(0)
    nt = pl.num_programs(0)
    tile, win, nsub = COMBINE_TILE, COMBINE_WIN, COMBINE_SUB
    bpt = tile // LANES
    kdim = N_EXPERTS * win

    def base(tt, e):
        return (off_ref[e, tt * bpt] // 16) * 16

    def copy(e, start, slot, sub):
        return pltpu.make_async_copy(ye_ref.at[e, pl.ds(pl.multiple_of(start, 16), win), :],
                                     stack.at[slot, sub, pl.ds(e * win, win), :], sems.at[slot, sub, e])

    def starts(tt, r):
        want = [base(tt, e) + r * win for e in range(N_EXPERTS)]
        return want, [jnp.minimum(w, cap - win) for w in want]

    def issue(tt, r, slot, sub):
        _, got = starts(tt, r)
        for e in range(N_EXPERTS):
            copy(e, got[e], slot, sub).start()

    def wait(tt, r, slot, sub):
        _, got = starts(tt, r)
        for e in range(N_EXPERTS):
            copy(e, got[e], slot, sub).wait()

    slot = t % 2

    @pl.when(t == 0)
    def _():
        for sub in range(nsub):
            issue(sub, 0, 0, sub)

    @pl.when(t + 1 < nt)
    def _():
        for sub in range(nsub):
            issue((t + 1) * nsub + sub, 0, 1 - slot, sub)

    lane16 = lax.broadcasted_iota(I32, (1, N_EXPERTS), 1)
    er = lax.broadcasted_iota(I32, (N_EXPERTS, kdim), 0)
    ec = lax.broadcasted_iota(I32, (N_EXPERTS, kdim), 1)
    expand = jnp.where(ec // win == er, 1.0, 0.0).astype(BF16)
    lane_in = (lax.broadcasted_iota(I32, (1, kdim), 1) % win).astype(F32)

    def row_of(vals):
        r = jnp.zeros((1, N_EXPERTS), I32)
        for e in range(N_EXPERTS):
            r = jnp.where(lane16 == e, vals[e], r)
        return r

    def accumulate(sub, r):
        rows = pl.ds(sub * tile, tile)
        want, got = starts(t * nsub + sub, r)
        wrow, grow = row_of(want), row_of(got)
        rk = rank_ref[rows, :]
        ok = jnp.logical_and(rk >= wrow, rk < grow + win)
        rel = jnp.where(ok, rk - grow, -1).astype(F32).astype(BF16)
        hit = _dot(rel, expand) == lane_in
        gates = _dot(aff_ref[rows, :].astype(BF16), expand)
        w = jnp.where(hit, gates, 0.0).astype(BF16)
        return _dot(w, stack[slot, sub])

    for sub in range(nsub):
        wait(t * nsub + sub, 0, slot, sub)
    for sub in range(nsub):
        rows = pl.ds(sub * tile, tile)
        o_ref[rows, :] = h_ref[rows, :] + accumulate(sub, 0)

    for sub in range(nsub):
        tt = t * nsub + sub
        nr = jnp.int32(1)
        for e in range(N_EXPERTS):
            span = off_ref[e, (tt + 1) * bpt] - base(tt, e)
            nr = jnp.maximum(nr, (span + win - 1) // win)

        def extra(r, carry, sub=sub, tt=tt):
            issue(tt, r, slot, sub)
            wait(tt, r, slot, sub)
            o_ref[pl.ds(sub * tile, tile), :] += accumulate(sub, r)
            return carry

        lax.fori_loop(1, nr, extra, 0)

    for sub in range(nsub):
        rows = pl.ds(sub * tile, tile)
        h2 = o_ref[rows, :]
        gate = jax.nn.sigmoid(_dot(_rms(h2, gp_ref[...]).astype(BF16), wg_ref[...]))
        h3 = h2 + _dot(p_ref[rows, :].astype(BF16), wp_ref[...]) * gate
        o_ref[rows, :] = _rms(h3, gfin_ref[...]) if last else h3


def _combine(off, h1, rank_tok, aff_tok, p, layer, gp, wg, wp, gfin, ye, cap, last):
    n = h1.shape[0]
    tile = COMBINE_TILE * COMBINE_SUB
    row = lambda i, o: (i, 0)
    fix = lambda i, o: (0, 0)
    prow = lambda i, o: (layer * (n // tile) + i, 0)
    return pl.pallas_call(
        functools.partial(_combine_body, cap=cap, last=last),
        grid_spec=pltpu.PrefetchScalarGridSpec(
            num_scalar_prefetch=1,
            grid=(n // tile,),
            in_specs=[pl.BlockSpec((tile, D_MODEL), row),
                      pl.BlockSpec((tile, N_EXPERTS), row),
                      pl.BlockSpec((tile, N_EXPERTS), row),
                      pl.BlockSpec((tile, PLE_DIM), prow),
                      pl.BlockSpec((1, D_MODEL), fix),
                      pl.BlockSpec((D_MODEL, D_MODEL), fix),
                      pl.BlockSpec((PLE_DIM, D_MODEL), fix),
                      pl.BlockSpec((1, D_MODEL), fix),
                      pl.BlockSpec(memory_space=pl.ANY)],
            out_specs=pl.BlockSpec((tile, D_MODEL), row),
            scratch_shapes=[pltpu.VMEM((2, COMBINE_SUB, N_EXPERTS * COMBINE_WIN, D_MODEL), BF16),
                            pltpu.SemaphoreType.DMA((2, COMBINE_SUB, N_EXPERTS))],
        ),
        out_shape=jax.ShapeDtypeStruct((n, D_MODEL), F32),
        compiler_params=_cparams(("arbitrary",)),
        name="combine",
    )(off, h1, rank_tok, aff_tok, p, gp, wg, wp, gfin, ye)


def _gather_rows(table, rank, cap):
    n, w = table.shape
    workers_per_expert = SC_CORES * SC_SUBCORES // N_EXPERTS
    per = cap // workers_per_expert
    nch = per // GATHER_ROWS
    mesh = plsc.VectorSubcoreMesh(core_axis_name="c", subcore_axis_name="s",
                                  num_cores=SC_CORES, num_subcores=SC_SUBCORES)
    cp = dataclasses.replace(pltpu.CompilerParams(), needs_layout_passes=False)

    @functools.partial(
        pl.kernel, mesh=mesh, compiler_params=cp,
        out_type=jax.ShapeDtypeStruct((N_EXPERTS * cap, w), I32),
        scratch_types=[pltpu.VMEM((RANK_CHUNK,), I32),
                       pltpu.VMEM((nch, GATHER_ROWS), I32),
                       pltpu.VMEM((GATHER_ROWS, w), I32),
                       pltpu.SemaphoreType.DMA],
        name="sc_gather")
    def gather(table_hbm, rank_hbm, out_hbm, rbuf, idx, rows, sem):
        wid = lax.axis_index("s") * SC_CORES + lax.axis_index("c")
        e = wid // workers_per_expert
        lo = (wid % workers_per_expert) * per
        lane = lax.iota(I32, SC_LANES)

        @pl.loop(0, n // RANK_CHUNK)
        def _(c):
            pltpu.sync_copy(rank_hbm.at[e, pl.ds(c * RANK_CHUNK, RANK_CHUNK)], rbuf)

            @pl.loop(0, RANK_CHUNK // SC_LANES)
            def _(i):
                rel = rbuf[pl.ds(i * SC_LANES, SC_LANES)] - lo
                mask = jnp.logical_and(rel >= 0, rel < per)
                rel = jnp.where(mask, rel, 0)
                tok = c * RANK_CHUNK + i * SC_LANES + lane
                plsc.store_scatter(idx, [rel // GATHER_ROWS, rel % GATHER_ROWS], tok, mask=mask)

        @pl.loop(0, nch)
        def _(c):
            pltpu.async_copy(table_hbm.at[idx.at[c]], rows, sem).wait()
            pltpu.sync_copy(rows, out_hbm.at[pl.ds(e * cap + lo + c * GATHER_ROWS, GATHER_ROWS)])

    return gather(table, rank).reshape(N_EXPERTS, cap, w)


def _trunk(x, p, prm, b, s):
    n = b * s
    nc = s // SSM_CHUNK
    cap = EC_CAPACITY_FACTOR * n // N_EXPERTS
    tables = _fourier_tables(s)
    h = x.reshape(n, D_MODEL)
    depth = prm['w_in'].shape[0]
    p_rows = p.reshape(depth * n, PLE_DIM)
    for l in range(depth):
        lp = prm['layers'][l]
        zf, zs, zqkv = _inproj(h, lp['g_mix'], lp['w_in'])
        mf = _fourier(zf, tables, prm['ccb'], prm['scb'], lp['w_fnet'], lp['g_f'], b, s)
        xl = _s5_in(zs, lp['s5_w_in'])
        pr, pi = _scan_powers(lp['s5_a_chunk'], int(math.log2(nc)))
        st = _s5_scan(xl, pr, pi, b, nc)
        ms = _s5_out(zs, st, lp['s5_m_intra'], lp['s5_w_out'], lp['s5_d'], lp['s5_glu_w'], lp['s5_glu_b'],
                     lp['g_s'])
        oa = _attn(zqkv, prm['slopes'], b, s)
        h1, m, aff = _outproj(h, mf, ms, oa, lp['g_a'], lp['w_out'], lp['g_ffn'], lp['w_router'])
        rank, off = _select(aff.T, cap)
        off = jnp.concatenate([off, jnp.full((N_EXPERTS, 1), cap, I32)], axis=1)
        xe = _gather_rows(m, rank, cap)
        ye = _ffn(xe, prm['w_gate'], prm['w_up'], prm['w_down'], l)
        h = _combine(off, h1, rank.T, aff, p_rows, l, lp['g_ple'], lp['w_ple_gate'],
                     lp['w_ple_proj'], prm['g_final'], ye, cap, l == depth - 1)
    return h.reshape(b, s, D_MODEL)


def kernel(x_prompt, x_sample, p_prompt, p_sample, norm_mix, w_in, w_fnet, ssm_a_re, ssm_a_im, ssm_log_dt, ssm_b_re, ssm_b_im, ssm_c_re, ssm_c_im, ssm_d, ssm_glu_w, ssm_glu_b, norm_branch, w_out, norm_ffn, w_router, w_exp_gate, w_exp_up, w_exp_down, norm_ple, w_ple_gate, w_ple_proj, norm_final):
    depth = w_in.shape[0]
    o1, o2 = FN_WIDTH, FN_WIDTH + SSM_WIDTH
    row = lambda v: v.reshape(1, -1).astype(F32)
    cc, sc = _dft_tables(FN_HEAD_DIM)
    tile4 = lambda mtx: _block_diag(jnp.broadcast_to(mtx[None], (FN_HEADS,) + mtx.shape)).astype(BF16)
    layers = []
    for l in range(depth):
        m_intra, s5_w_in, s5_w_out, a_chunk = _s5_matrices(
            ssm_a_re[l], ssm_a_im[l], ssm_log_dt[l], ssm_b_re[l], ssm_b_im[l], ssm_c_re[l], ssm_c_im[l])
        layers.append(dict(
            g_mix=row(norm_mix[l]), w_in=w_in[l].astype(BF16),
            w_fnet=_block_diag(w_fnet[l]).astype(BF16),
            g_f=row(norm_branch[l][:o1]), g_s=row(norm_branch[l][o1:o2]), g_a=row(norm_branch[l][o2:]),
            s5_m_intra=m_intra, s5_w_in=s5_w_in, s5_w_out=s5_w_out, s5_a_chunk=a_chunk,
            s5_d=row(ssm_d[l]), s5_glu_w=ssm_glu_w[l].astype(BF16), s5_glu_b=row(ssm_glu_b[l]),
            w_out=w_out[l].astype(BF16), g_ffn=row(norm_ffn[l]),
            w_router=w_router[l].astype(BF16),
            g_ple=row(norm_ple[l]), w_ple_gate=w_ple_gate[l].astype(BF16),
            w_ple_proj=w_ple_proj[l].astype(BF16)))
    slopes = jnp.asarray([2.0 ** (-8.0 * (i + 1) / ATT_HEADS) for i in range(ATT_HEADS)], F32)
    prm = dict(w_in=w_in, layers=layers, ccb=tile4(cc), scb=tile4(sc), slopes=slopes,
               w_gate=w_exp_gate.astype(BF16), w_up=w_exp_up.astype(BF16), w_down=w_exp_down.astype(BF16),
               g_final=row(norm_final))
    bp, sp = x_prompt.shape[0], x_prompt.shape[1]
    bs, ssq = x_sample.shape[0], x_sample.shape[1]
    y_prompt = _trunk(x_prompt, p_prompt, prm, bp, sp)
    y_sample = _trunk(x_sample, p_sample, prm, bs, ssq)
    return (y_prompt, y_sample)
```

```python
import dataclasses
import functools
import math

import jax
import jax.numpy as jnp
from jax import lax
from jax.experimental import pallas as pl
from jax.experimental.pallas import tpu as pltpu
from jax.experimental.pallas import tpu_sc as plsc

D_MODEL = 1024
FN_WIDTH = 256
FN_HEADS = 4
FN_HEAD_DIM = 64
SSM_WIDTH = 256
SSM_GROUP = 16
SSM_GROUPS = 16
SSM_STATE = 64
ATT_WIDTH = 512
ATT_HEAD_DIM = 64
ATT_HEADS = 8
DILATED_PATTERNS = ((128, 1), (512, 4), (2048, 16))
IN_PROJ_WIDTH = 2048
N_EXPERTS = 16
EC_CAPACITY_FACTOR = 2
D_FF_EXPERT = 2048
PLE_DIM = 256
RMS_EPS = 1e-6
NEG_INF = -1e30

LANES = 128
SSM_CHUNK = 8
SSM_ROW = SSM_CHUNK * SSM_WIDTH
SSM_NSTATE = SSM_GROUPS * SSM_STATE
S5_ROWS = 256
ATT_HALF = 64
ATT_UNROLL = 8
ATT_SPLIT = 4
ATT_OFFSETS = 3
FFN_ROWS = 1024
COMBINE_TILE = 256
COMBINE_SUB = 4
PROJ_ROWS = 1024
FFT_RADIX = 4
COMBINE_WIN = 64
VMEM_LIMIT = 56 * 1024 * 1024
SC_CORES = 2
SC_SUBCORES = 16
SC_LANES = 16
GATHER_ROWS = 64
RANK_CHUNK = 2048
PACKED = D_MODEL // 2

F32 = jnp.float32
BF16 = jnp.bfloat16
I32 = jnp.int32


def _cparams(sem):
    return pltpu.CompilerParams(dimension_semantics=sem, vmem_limit_bytes=VMEM_LIMIT)


def _rms(x, g):
    return x * lax.rsqrt(jnp.mean(x * x, axis=-1, keepdims=True) + RMS_EPS) * g


def _dot(a, b):
    return jnp.dot(a, b, preferred_element_type=F32)


def _dot_nt(a, b):
    return lax.dot_general(a, b, (((1,), (1,)), ((), ())), preferred_element_type=F32)


def _inproj_body(h_ref, g_ref, w_ref, zf_ref, zs_ref, zqkv_ref):
    a = _rms(h_ref[...], g_ref[...]).astype(BF16)
    z = _dot(a, w_ref[...])
    zf_ref[...] = z[:, :FN_WIDTH]
    zs_ref[...] = z[:, FN_WIDTH:FN_WIDTH + SSM_WIDTH]
    zqkv_ref[...] = z[:, FN_WIDTH + SSM_WIDTH:].astype(BF16)


def _inproj(h, g, w):
    n = h.shape[0]
    tm = PROJ_ROWS
    return pl.pallas_call(
        _inproj_body,
        grid=(n // tm,),
        in_specs=[pl.BlockSpec((tm, D_MODEL), lambda i: (i, 0)),
                  pl.BlockSpec((1, D_MODEL), lambda i: (0, 0)),
                  pl.BlockSpec((D_MODEL, IN_PROJ_WIDTH), lambda i: (0, 0))],
        out_specs=[pl.BlockSpec((tm, FN_WIDTH), lambda i: (i, 0)),
                   pl.BlockSpec((tm, SSM_WIDTH), lambda i: (i, 0)),
                   pl.BlockSpec((tm, 3 * ATT_WIDTH), lambda i: (i, 0))],
        out_shape=[jax.ShapeDtypeStruct((n, FN_WIDTH), F32),
                   jax.ShapeDtypeStruct((n, SSM_WIDTH), F32),
                   jax.ShapeDtypeStruct((n, 3 * ATT_WIDTH), BF16)],
        compiler_params=_cparams(("arbitrary",)),
        name="inproj",
    )(h, g, w)


def _fourier_body(xa_ref, xb_ref, cs_ref, ss_ref, tc_ref, ts_ref, cc_ref, sc_ref, wb_ref, g_ref, o_ref, *, s):
    quarter = s // FFT_RADIX
    def phase(q):
        return jnp.concatenate([xa_ref[0, pl.ds(q, quarter, stride=FFT_RADIX), :],
                                xb_ref[0, pl.ds(q, quarter, stride=FFT_RADIX), :]], axis=1)
    x4 = jnp.concatenate([phase(q) for q in range(FFT_RADIX)], axis=1).astype(BF16)
    y = _dot(cs_ref[...], x4)
    z = _dot(ss_ref[...], x4)
    wide = lambda t: jnp.concatenate([t] * (FN_WIDTH // LANES), axis=1)
    tcs, tss = [], []
    for q in range(FFT_RADIX):
        fc = y[:, q * FN_WIDTH:(q + 1) * FN_WIDTH]
        fs = z[:, q * FN_WIDTH:(q + 1) * FN_WIDTH]
        if q == 0:
            tcs.append(fc)
            tss.append(fs)
        else:
            c, sn = wide(tc_ref[q - 1]), wide(ts_ref[q - 1])
            tcs.append(c * fc - sn * fs)
            tss.append(c * fs + sn * fc)
    (c0, c1, c2, c3), (s0, s1, s2, s3) = tcs, tss
    parts = ((c0 + c1 + c2 + c3, s0 + s1 + s2 + s3),
             (c0 - s1 - c2 + s3, s0 + c1 - s2 - c3),
             (c0 - c1 + c2 - c3, s0 - s1 + s2 - s3),
             (c0 + s1 - c2 - s3, s0 - c1 - s2 + c3))
    for part, (yy, zz) in enumerate(parts):
        f = _dot(yy.astype(BF16), cc_ref[...]) - _dot(zz.astype(BF16), sc_ref[...])
        o = _dot(f.astype(BF16), wb_ref[...])
        o_ref[0, part] = _rms(o, g_ref[...]).astype(BF16)


def _dft_tables(n):
    def exact(rows):
        k = (rows[:, None] * jnp.arange(n, dtype=I32)[None, :]) % n
        ang = k.astype(F32) * (2.0 * math.pi / n)
        return jnp.cos(ang), jnp.sin(ang)
    if n <= LANES:
        return exact(jnp.arange(n, dtype=I32))
    ca, sa = exact(LANES * jnp.arange(n // LANES, dtype=I32))
    cb, sb = exact(jnp.arange(LANES, dtype=I32))
    cos = ca[:, None, :] * cb[None] - sa[:, None, :] * sb[None]
    sin = sa[:, None, :] * cb[None] + ca[:, None, :] * sb[None]
    return cos.reshape(n, n), sin.reshape(n, n)


def _fourier(zf, tables, ccb, scb, wb, g, b, s):
    cs, ss, tc, ts = tables
    quarter = s // FFT_RADIX
    tr = min(512, quarter)
    x = zf.reshape(b, s, FN_WIDTH)
    fix = lambda i, j: (0, 0)
    out = pl.pallas_call(
        functools.partial(_fourier_body, s=s),
        grid=(quarter // tr, b),
        in_specs=[pl.BlockSpec((1, s, LANES), lambda i, j: (j, 0, 0)),
                  pl.BlockSpec((1, s, LANES), lambda i, j: (j, 0, 1)),
                  pl.BlockSpec((tr, quarter), lambda i, j: (i, 0)),
                  pl.BlockSpec((tr, quarter), lambda i, j: (i, 0)),
                  pl.BlockSpec((FFT_RADIX - 1, tr, LANES), lambda i, j: (0, i, 0)),
                  pl.BlockSpec((FFT_RADIX - 1, tr, LANES), lambda i, j: (0, i, 0)),
                  pl.BlockSpec((FN_WIDTH, FN_WIDTH), fix),
                  pl.BlockSpec((FN_WIDTH, FN_WIDTH), fix),
                  pl.BlockSpec((FN_WIDTH, FN_WIDTH), fix),
                  pl.BlockSpec((1, FN_WIDTH), fix)],
        out_specs=pl.BlockSpec((1, FFT_RADIX, tr, FN_WIDTH), lambda i, j: (j, 0, i, 0)),
        out_shape=jax.ShapeDtypeStruct((b, FFT_RADIX, quarter, FN_WIDTH), BF16),
        compiler_params=_cparams(("arbitrary", "arbitrary")),
        name="fourier",
    )(x, x, cs, ss, tc, ts, ccb, scb, wb, g)
    return out.reshape(b * s, FN_WIDTH)


def _fourier_tables(s):
    quarter = s // FFT_RADIX
    cs, ss = _dft_tables(quarter)
    kq = jnp.arange(1, FFT_RADIX, dtype=F32)[:, None] * jnp.arange(quarter, dtype=F32)[None, :]
    ang = kq * (2.0 * math.pi / s)
    wide = lambda v: jnp.broadcast_to(v[:, :, None], (FFT_RADIX - 1, quarter, LANES))
    return cs.astype(BF16), ss.astype(BF16), wide(jnp.cos(ang)), wide(jnp.sin(ang))


def _block_diag(blocks):
    h, a, bb = blocks.shape
    eye = jnp.eye(h, dtype=blocks.dtype)
    return jnp.einsum('hab,hg->hagb', blocks, eye).reshape(h * a, h * bb)


def _s5_matrices(a_re, a_im, log_dt, b_re, b_im, c_re, c_im):
    t = SSM_CHUNK
    g, p, c = SSM_GROUPS, SSM_STATE, SSM_GROUP
    lam = lax.complex(a_re.astype(F32), a_im.astype(F32))
    dt = jnp.exp(log_dt.astype(F32))[..., None]
    abar = jnp.exp(lam * dt)
    bbar = ((abar - 1.0) / lam)[..., None] * lax.complex(b_re.astype(F32), b_im.astype(F32))
    cmat = lax.complex(c_re.astype(F32), c_im.astype(F32))
    ks = jnp.arange(t + 1, dtype=F32)
    apow = jnp.exp((lam * dt)[:, None] * ks[None, :, None, None])

    kern = jnp.real(jnp.einsum('dgcp,dkgp,dgpe->dkgce', cmat, apow[:, :t], bbar))
    gh, hw, hs = g // 2, SSM_WIDTH // 2, SSM_NSTATE // 2
    lags = jnp.arange(-(t - 1), t)
    pick = lambda m: m[:, None, None, None]
    klag = (jnp.where(pick(lags >= 0), kern[0][jnp.clip(lags, 0, t - 1)], 0.0)
            + jnp.where(pick(lags <= 0), kern[1][jnp.clip(-lags, 0, t - 1)], 0.0))
    eye_h = jnp.eye(gh, dtype=F32)
    lag = jnp.arange(t)[None, :] - jnp.arange(t)[:, None]
    def _intra(h):
        blocks = jnp.einsum('lgce,gh->lgehc', klag[:, h * gh:(h + 1) * gh], eye_h).reshape(2 * t - 1, hw, hw)
        return blocks[lag + t - 1].transpose(0, 2, 1, 3).reshape(t * hw, t * hw)
    m_intra = jnp.stack([_intra(0), _intra(1)])

    wf = apow[0, t - 1 - jnp.arange(t)][:, :, :, None] * bbar[0][None]
    wb = apow[1, jnp.arange(t)][:, :, :, None] * bbar[1][None]
    same_group = ((jnp.arange(t * hw)[:, None] // c) % gh) == (jnp.arange(hs)[None, :] // p)
    def _spread(x, h):
        xh = x[:, h * gh:(h + 1) * gh].reshape(t * hw, p)
        return jnp.where(same_group, jnp.tile(xh, (1, gh)), 0.0)
    def _parts(parts, h):
        return jnp.concatenate([sign * _spread(x, h) for sign, x in parts], axis=1)
    wf, wb = jnp.swapaxes(wf, 2, 3), jnp.swapaxes(wb, 2, 3)
    in_parts = [(1.0, jnp.real(wf)), (1.0, jnp.imag(wf)), (1.0, jnp.real(wb)), (1.0, jnp.imag(wb))]
    w_in = jnp.stack([_parts(in_parts, 0), _parts(in_parts, 1)])

    qf = cmat[0][None] * apow[0, 1 + jnp.arange(t)][:, :, None, :]
    qb = cmat[1][None] * apow[1, t - jnp.arange(t)][:, :, None, :]
    out_parts = [(1.0, jnp.real(qf)), (-1.0, jnp.imag(qf)), (1.0, jnp.real(qb)), (-1.0, jnp.imag(qb))]
    w_out = jnp.stack([_parts(out_parts, 0).T, _parts(out_parts, 1).T])

    a_chunk = lam * dt * t
    return m_intra.astype(BF16), w_in.astype(BF16), w_out.astype(BF16), a_chunk


def _scan_powers(a_chunk, nsteps):
    e = jnp.exp(a_chunk[None] * (2.0 ** jnp.arange(nsteps, dtype=F32))[:, None, None, None])
    e = e.reshape(nsteps, 2 * SSM_NSTATE)
    return jnp.real(e), jnp.imag(e)


def _half_steps(z_ref, tr):
    return jnp.concatenate([z_ref[pl.ds(r, tr, stride=SSM_CHUNK), :] for r in range(SSM_CHUNK)], axis=1)


def _halves(rows):
    return [pl.BlockSpec((rows, LANES), lambda i: (i, 0)), pl.BlockSpec((rows, LANES), lambda i: (i, 1))]


def _s5_in_body(za_ref, zb_ref, w_ref, o_ref):
    tr = o_ref.shape[0]
    hs = SSM_NSTATE // 2
    for h, z_ref in enumerate((za_ref, zb_ref)):
        res = _dot(_half_steps(z_ref, tr).astype(BF16), w_ref[h])
        for part in range(4):
            lo = part * SSM_NSTATE + h * hs
            o_ref[:, lo:lo + hs] = res[:, part * hs:(part + 1) * hs]


def _s5_in(zs, w_in):
    rows = zs.shape[0] // SSM_CHUNK
    tr = min(S5_ROWS, rows)
    width = 4 * SSM_NSTATE
    return pl.pallas_call(
        _s5_in_body,
        grid=(rows // tr,),
        in_specs=_halves(tr * SSM_CHUNK) + [
            pl.BlockSpec((2, SSM_ROW // 2, width // 2), lambda i: (0, 0, 0), pipeline_mode=pl.Buffered(1))],
        out_specs=pl.BlockSpec((tr, width), lambda i: (i, 0)),
        out_shape=jax.ShapeDtypeStruct((rows, width), F32),
        compiler_params=_cparams(("arbitrary",)),
        name="s5_in",
    )(zs, zs, w_in)


def _s5_scan_body(x_ref, pr_ref, pi_ref, o_ref, *, nc, nsteps):
    ns = SSM_NSTATE
    x = x_ref[0]
    fre, fim = x[:, 0:ns], x[:, ns:2 * ns]
    bre, bim = x[:, 2 * ns:3 * ns], x[:, 3 * ns:4 * ns]
    row = lax.broadcasted_iota(I32, (nc, 1), 0)
    for k in range(nsteps):
        sh = 2 ** k
        far, fai = pr_ref[k:k + 1, 0:ns], pi_ref[k:k + 1, 0:ns]
        bar, bai = pr_ref[k:k + 1, ns:2 * ns], pi_ref[k:k + 1, ns:2 * ns]
        fmask = row >= sh
        sre = jnp.where(fmask, pltpu.roll(fre, sh, 0), 0.0)
        sim = jnp.where(fmask, pltpu.roll(fim, sh, 0), 0.0)
        fre, fim = fre + far * sre - fai * sim, fim + far * sim + fai * sre
        bmask = row < nc - sh
        sre = jnp.where(bmask, pltpu.roll(bre, nc - sh, 0), 0.0)
        sim = jnp.where(bmask, pltpu.roll(bim, nc - sh, 0), 0.0)
        bre, bim = bre + bar * sre - bai * sim, bim + bar * sim + bai * sre
    fmask = row >= 1
    bmask = row < nc - 1
    o_ref[0, :, 0:ns] = jnp.where(fmask, pltpu.roll(fre, 1, 0), 0.0).astype(BF16)
    o_ref[0, :, ns:2 * ns] = jnp.where(fmask, pltpu.roll(fim, 1, 0), 0.0).astype(BF16)
    o_ref[0, :, 2 * ns:3 * ns] = jnp.where(bmask, pltpu.roll(bre, nc - 1, 0), 0.0).astype(BF16)
    o_ref[0, :, 3 * ns:4 * ns] = jnp.where(bmask, pltpu.roll(bim, nc - 1, 0), 0.0).astype(BF16)


def _s5_scan(xl, pr, pi, b, nc):
    nsteps = int(math.log2(nc))
    width = 4 * SSM_NSTATE
    x = xl.reshape(b, nc, width)
    out = pl.pallas_call(
        functools.partial(_s5_scan_body, nc=nc, nsteps=nsteps),
        grid=(b,),
        in_specs=[pl.BlockSpec((1, nc, width), lambda i: (i, 0, 0)),
                  pl.BlockSpec((nsteps, 2 * SSM_NSTATE), lambda i: (0, 0)),
                  pl.BlockSpec((nsteps, 2 * SSM_NSTATE), lambda i: (0, 0))],
        out_specs=pl.BlockSpec((1, nc, width), lambda i: (i, 0, 0)),
        out_shape=jax.ShapeDtypeStruct((b, nc, width), BF16),
        compiler_params=_cparams(("arbitrary",)),
        name="s5_scan",
    )(x, pr, pi)
    return out.reshape(b * nc, width)


def _s5_out_body(za_ref, zb_ref, s_ref, m_ref, w_ref, d_ref, gw_ref, gb_ref, g_ref, o_ref, nat_a, nat_b):
    tr = s_ref.shape[0]
    hs = SSM_NSTATE // 2
    us, ys = [], []
    for h, z_ref in enumerate((za_ref, zb_ref)):
        u = _half_steps(z_ref, tr)
        st = jnp.concatenate([s_ref[:, part * SSM_NSTATE + h * hs:part * SSM_NSTATE + (h + 1) * hs]
                              for part in range(4)], axis=1)
        us.append(u)
        ys.append(_dot(st, w_ref[h]) + _dot(u.astype(BF16), m_ref[h]))
    c0 = math.sqrt(2.0 / math.pi)
    for r in range(SSM_CHUNK):
        step = slice(r * LANES, (r + 1) * LANES)
        y = jnp.concatenate([ys[0][:, step], ys[1][:, step]], axis=1)
        u = jnp.concatenate([us[0][:, step], us[1][:, step]], axis=1)
        v = y + d_ref[...] * u
        gl = 0.5 * v * (1.0 + jnp.tanh(c0 * (v + 0.044715 * (v * v * v))))
        gate = jax.nn.sigmoid(_dot(gl.astype(BF16), gw_ref[...]) + gb_ref[...])
        out = _rms(gl * gate, g_ref[...])
        nat_a[pl.ds(r, tr, stride=SSM_CHUNK), :] = out[:, :LANES]
        nat_b[pl.ds(r, tr, stride=SSM_CHUNK), :] = out[:, LANES:]
    o_ref[:, :LANES] = nat_a[...].astype(BF16)
    o_ref[:, LANES:] = nat_b[...].astype(BF16)


def _s5_out(zs, states, m_intra, w_out, d, glu_w, glu_b, g):
    rows = states.shape[0]
    tr = min(S5_ROWS, rows)
    width = 4 * SSM_NSTATE
    fix = lambda i: (0, 0)
    fix3 = lambda i: (0, 0, 0)
    once = pl.Buffered(1)
    return pl.pallas_call(
        _s5_out_body,
        grid=(rows // tr,),
        in_specs=_halves(tr * SSM_CHUNK) + [
                  pl.BlockSpec((tr, width), lambda i: (i, 0)),
                  pl.BlockSpec((2, SSM_ROW // 2, SSM_ROW // 2), fix3, pipeline_mode=once),
                  pl.BlockSpec((2, width // 2, SSM_ROW // 2), fix3, pipeline_mode=once),
                  pl.BlockSpec((1, SSM_WIDTH), fix), pl.BlockSpec((SSM_WIDTH, SSM_WIDTH), fix),
                  pl.BlockSpec((1, SSM_WIDTH), fix), pl.BlockSpec((1, SSM_WIDTH), fix)],
        out_specs=pl.BlockSpec((tr * SSM_CHUNK, SSM_WIDTH), lambda i: (i, 0)),
        out_shape=jax.ShapeDtypeStruct((rows * SSM_CHUNK, SSM_WIDTH), BF16),
        scratch_shapes=[pltpu.VMEM((tr * SSM_CHUNK, LANES), F32) for _ in range(SSM_WIDTH // LANES)],
        compiler_params=_cparams(("arbitrary",)),
        name="s5_out",
    )(zs, zs, states, m_intra, w_out, d, glu_w, glu_b, g)


def _attn_geometry(s, d):
    ln = s // d
    bq = min(128, ln)
    bk = min(bq + 2 * ATT_HALF, ln)
    return ln, bq, bk, ln // bq


def _attn_body(slope_ref, q_ref, k_ref, v_ref, o_ref, nat, qd, kd, vd, qdb, kdb, vdb, b0, b1, b2, *acc, s):
    hp = pl.program_id(1)
    s4 = s // ATT_SPLIT
    lane = lax.broadcasted_iota(I32, (1, LANES), 1)
    first = lane < ATT_HEAD_DIM
    second = jnp.logical_not(first)
    slopes = (slope_ref[2 * hp], slope_ref[2 * hp + 1])
    log2e = math.log2(math.e)
    scale = ATT_HEAD_DIM ** -0.5 * log2e

    @pl.when(pl.program_id(0) == 0)
    def _():
        for (_, d), bias in zip(DILATED_PATTERNS, (b0, b1, b2)):
            _, bq, bk, _ = _attn_geometry(s, d)
            jk = lax.broadcasted_iota(I32, (1, bk), 1)
            for o in range(ATT_OFFSETS):
                rel = jnp.abs(lax.broadcasted_iota(I32, (bq, 1), 0) + o * ATT_HALF - jk)
                dist = (d * rel).astype(F32) * log2e
                for hh in range(2):
                    bias[hp, 2 * o + hh] = jnp.where(rel <= ATT_HALF, -slopes[hh] * dist, NEG_INF)

    for src, dst_f, dst_b in ((q_ref, qd, qdb), (k_ref, kd, kdb), (v_ref, vd, vdb)):
        nat[...] = src[...].astype(F32)
        for c in range(ATT_SPLIT):
            x = nat[pl.ds(c, s4, stride=ATT_SPLIT), :]
            dst_f[c * s4:(c + 1) * s4, :] = x
            dst_b[c * s4:(c + 1) * s4, :] = x.astype(BF16)

    def pattern(p, d, refs, bias, locate, stride):
        acc_o, acc_m, acc_l = acc[3 * p], acc[3 * p + 1], acc[3 * p + 2]
        ln, bq, bk, nqb = _attn_geometry(s, d)

        def rows(c, j, size):
            start = locate(c, j)
            if stride == 1:
                return pl.ds(pl.multiple_of(start, ATT_HALF), size)
            return pl.ds(start, size, stride=stride)

        def scores(i):
            c = i // nqb
            j0 = (i % nqb) * bq
            ks = jnp.clip(j0 - ATT_HALF, 0, ln - bk)
            q = refs[0][rows(c, j0, bq), :].astype(BF16)
            k = refs[1][rows(c, ks, bk), :].astype(BF16)
            v = refs[2][rows(c, ks, bk), :].astype(BF16)
            off = (j0 - ks) // ATT_HALF
            scs = []
            for hh in range(2):
                qm = jnp.where(first if hh == 0 else second, q, jnp.zeros_like(q))
                scs.append(_dot_nt(qm, k) * scale + bias[hp, 2 * off + hh])
            return rows(c, j0, bq), v, scs

        def softmax(sc):
            m = jnp.max(sc, axis=-1, keepdims=True)
            pe = jnp.exp2(sc - m)
            return pe.astype(BF16), m, jnp.sum(pe, axis=-1, keepdims=True)

        def group(g, carry):
            staged = [scores(g * ATT_UNROLL + u) for u in range(ATT_UNROLL)]
            soft = [[softmax(sc) for sc in scs] for _, _, scs in staged]
            for (dst, v, _), ((p0, m0, l0), (p1, m1, l1)) in zip(staged, soft):
                acc_o[dst, :] = jnp.where(first, _dot(p0, v), _dot(p1, v))
                acc_m[dst, :] = jnp.where(first, m0, m1)
                acc_l[dst, :] = jnp.where(first, l0, l1)
            return carry

        lax.fori_loop(0, d * nqb // ATT_UNROLL, group, 0)

    (_, d1), (_, d2), (_, d3) = DILATED_PATTERNS
    assert d1 == 1 and d2 == ATT_SPLIT and d3 == ATT_SPLIT * ATT_SPLIT
    pattern(0, d1, (q_ref, k_ref, v_ref), b0, lambda c, j: j, 1)
    pattern(1, d2, (qdb, kdb, vdb), b1, lambda c, j: c * s4 + j, 1)
    pattern(2, d3, (qd, kd, vd), b2,
            lambda c, j: (c % ATT_SPLIT) * s4 + c // ATT_SPLIT + ATT_SPLIT * j, ATT_SPLIT)

    for c in range(ATT_SPLIT):
        part = pl.ds(c, s4, stride=ATT_SPLIT)
        blk = slice(c * s4, (c + 1) * s4)
        ms = (acc[1][part, :], acc[4][blk, :], acc[7][blk, :])
        os_ = (acc[0][part, :], acc[3][blk, :], acc[6][blk, :])
        ls = (acc[2][part, :], acc[5][blk, :], acc[8][blk, :])
        m = jnp.maximum(jnp.maximum(ms[0], ms[1]), ms[2])
        num = jnp.zeros((s4, LANES), F32)
        den = jnp.zeros((s4, LANES), F32)
        for p in range(3):
            w = jnp.exp2(ms[p] - m)
            num = num + w * os_[p]
            den = den + w * ls[p]
        nat[part, :] = num / den
    o_ref[...] = nat[...].astype(BF16)


def _attn(zqkv, slopes, b, s):
    n = b * s
    nhp = ATT_HEADS // 2
    col = lambda off: (lambda i, j, sl: (i, off + j))
    return pl.pallas_call(
        functools.partial(_attn_body, s=s),
        grid_spec=pltpu.PrefetchScalarGridSpec(
            num_scalar_prefetch=1,
            grid=(b, nhp),
            in_specs=[pl.BlockSpec((s, LANES), col(0)),
                      pl.BlockSpec((s, LANES), col(nhp)),
                      pl.BlockSpec((s, LANES), col(2 * nhp))],
            out_specs=pl.BlockSpec((s, LANES), lambda i, j, sl: (i, j)),
            scratch_shapes=([pltpu.VMEM((s, LANES), F32) for _ in range(4)]
                            + [pltpu.VMEM((s, LANES), BF16) for _ in range(3)]
                            + [pltpu.VMEM((nhp, 2 * ATT_OFFSETS) + _attn_geometry(s, d)[1:3], F32)
                               for _, d in DILATED_PATTERNS]
                            + [pltpu.VMEM((s, LANES), F32) for _ in range(9)]),
        ),
        out_shape=jax.ShapeDtypeStruct((n, ATT_WIDTH), BF16),
        compiler_params=_cparams(("arbitrary", "arbitrary")),
        name="attn",
    )(slopes, zqkv, zqkv, zqkv)


def _outproj_body(h_ref, mf_ref, ms_ref, oa_ref, ga_ref, w_ref, gf_ref, wr_ref,
                  h1_ref, m_ref, aff_ref):
    oa = _rms(oa_ref[...].astype(F32), ga_ref[...]).astype(BF16)
    o1, o2 = FN_WIDTH, FN_WIDTH + SSM_WIDTH
    acc = h_ref[...] + _dot(mf_ref[...], w_ref[0:o1, :]) + _dot(ms_ref[...], w_ref[o1:o2, :])
    acc = acc + _dot(oa, w_ref[o2:, :])
    h1_ref[...] = acc
    m = _rms(acc, gf_ref[...]).astype(BF16)
    bits = pltpu.bitcast(m.astype(F32), I32)
    m_ref[...] = lax.shift_right_logical(bits[:, :PACKED], 16) | (bits[:, PACKED:] & jnp.int32(-65536))
    lg = _dot(m, wr_ref[...])
    e = jnp.exp(lg - jnp.max(lg, axis=1, keepdims=True))
    aff_ref[...] = e / jnp.sum(e, axis=1, keepdims=True)


def _outproj(h, mf, ms, oa, ga, w, gf, wr):
    n = h.shape[0]
    tm = PROJ_ROWS
    row = lambda i: (i, 0)
    fix = lambda i: (0, 0)
    return pl.pallas_call(
        _outproj_body,
        grid=(n // tm,),
        in_specs=[pl.BlockSpec((tm, D_MODEL), row), pl.BlockSpec((tm, FN_WIDTH), row),
                  pl.BlockSpec((tm, SSM_WIDTH), row), pl.BlockSpec((tm, ATT_WIDTH), row),
                  pl.BlockSpec((1, ATT_WIDTH), fix), pl.BlockSpec((D_MODEL, D_MODEL), fix),
                  pl.BlockSpec((1, D_MODEL), fix), pl.BlockSpec((D_MODEL, N_EXPERTS), fix)],
        out_specs=[pl.BlockSpec((tm, D_MODEL), row), pl.BlockSpec((tm, PACKED), row),
                   pl.BlockSpec((tm, N_EXPERTS), row)],
        out_shape=[jax.ShapeDtypeStruct((n, D_MODEL), F32),
                   jax.ShapeDtypeStruct((n, PACKED), I32),
                   jax.ShapeDtypeStruct((n, N_EXPERTS), F32)],
        compiler_params=_cparams(("arbitrary",)),
        name="outproj",
    )(h, mf, ms, oa, ga, w, gf, wr)


def _select_body(aff_ref, rank_ref, off_ref, thr_ref, *, cap, nb):
    r128 = lax.broadcasted_iota(I32, (LANES, LANES), 0)
    c128 = lax.broadcasted_iota(I32, (LANES, LANES), 1)
    upper_incl = jnp.where(r128 <= c128, 1.0, 0.0).astype(BF16)
    ones = jnp.ones((LANES, LANES), BF16)
    rb = lax.broadcasted_iota(I32, (nb, nb), 0)
    cb = lax.broadcasted_iota(I32, (nb, nb), 1)
    lower_strict = jnp.where(cb < rb, 1.0, 0.0).astype(BF16)
    upper_strict = jnp.where(rb < cb, 1.0, 0.0).astype(BF16)
    ones8 = jnp.ones((8, LANES), BF16)

    def count(mask):
        c = jnp.sum(jnp.where(mask, 1.0, 0.0), axis=0, keepdims=True)
        return jnp.sum(c, axis=1, keepdims=True)

    bits_all = pltpu.bitcast(aff_ref[...], I32)

    def bitstep(i, t):
        cand = t | jnp.left_shift(jnp.int32(1), 30 - i)
        above = jnp.sum(jnp.where(bits_all >= cand, 1.0, 0.0), axis=1, keepdims=True)
        return jnp.where(jnp.sum(above, axis=2, keepdims=True) >= cap, cand, t)

    thr = lax.fori_loop(0, 31, bitstep, jnp.zeros((N_EXPERTS, 1, 1), I32))
    thr_ref[...] = jnp.broadcast_to(thr, thr_ref.shape)

    def prefix(mask):
        mb = jnp.where(mask, 1.0, 0.0).astype(BF16)
        incl = _dot(mb, upper_incl)
        tot = _dot(mb, ones)
        offs = _dot(lower_strict, tot.astype(BF16))
        return offs + incl - 1.0, mb

    def per_expert(e, carry):
        bits = pltpu.bitcast(aff_ref[e], I32)
        t = thr_ref[e][0:1, 0:1]
        gt = bits > t
        eq = bits == t
        need = cap - count(gt)
        eq_rank, _ = prefix(eq)
        sel = jnp.logical_or(gt, jnp.logical_and(eq, eq_rank < need))
        rank, mb = prefix(sel)
        rank_ref[e] = jnp.where(sel, rank.astype(I32), -1)
        tot_row = _dot_nt(ones8, mb)
        off_row = _dot(tot_row.astype(BF16), upper_strict)
        off_ref[pl.ds(e, 1), :] = off_row[0:1, :].astype(I32)
        return carry

    lax.fori_loop(0, N_EXPERTS, per_expert, 0)


def _select(aff_t, cap):
    n = aff_t.shape[1]
    nb = n // LANES
    a3 = aff_t.reshape(N_EXPERTS, nb, LANES)
    rank, off = pl.pallas_call(
        functools.partial(_select_body, cap=cap, nb=nb),
        grid=(1,),
        in_specs=[pl.BlockSpec((N_EXPERTS, nb, LANES), lambda i: (0, 0, 0))],
        out_specs=[pl.BlockSpec((N_EXPERTS, nb, LANES), lambda i: (0, 0, 0)),
                   pl.BlockSpec((N_EXPERTS, nb), lambda i: (0, 0))],
        out_shape=[jax.ShapeDtypeStruct((N_EXPERTS, nb, LANES), I32),
                   jax.ShapeDtypeStruct((N_EXPERTS, nb), I32)],
        scratch_shapes=[pltpu.VMEM((N_EXPERTS, 8, LANES), I32)],
        compiler_params=_cparams(("arbitrary",)),
        name="select",
    )(a3)
    return rank.reshape(N_EXPERTS, n), off


def _ffn_body(x_ref, wg_ref, wu_ref, wd_ref, o_ref):
    w = x_ref[...]
    x = jnp.concatenate([pltpu.bitcast(lax.shift_left(w, 16), F32),
                         pltpu.bitcast(w & jnp.int32(-65536), F32)], axis=1).astype(BF16)
    tf = 512
    acc = jnp.zeros(o_ref.shape, F32)
    for j in range(D_FF_EXPERT // tf):
        g = _dot(x, wg_ref[:, j * tf:(j + 1) * tf])
        u = _dot(x, wu_ref[:, j * tf:(j + 1) * tf])
        hdn = (g * jax.nn.sigmoid(g) * u).astype(BF16)
        acc = acc + _dot(hdn, wd_ref[j * tf:(j + 1) * tf, :])
    o_ref[...] = acc.astype(BF16)


def _ffn(xe, wg, wu, wd, layer):
    e, cap, _ = xe.shape
    tm = min(FFN_ROWS, cap)
    return pl.pallas_call(
        _ffn_body,
        grid=(e, cap // tm),
        in_specs=[pl.BlockSpec((None, tm, PACKED), lambda i, j: (i, j, 0)),
                  pl.BlockSpec((None, None, D_MODEL, D_FF_EXPERT), lambda i, j: (layer, i, 0, 0)),
                  pl.BlockSpec((None, None, D_MODEL, D_FF_EXPERT), lambda i, j: (layer, i, 0, 0)),
                  pl.BlockSpec((None, None, D_FF_EXPERT, D_MODEL), lambda i, j: (layer, i, 0, 0))],
        out_specs=pl.BlockSpec((None, tm, D_MODEL), lambda i, j: (i, j, 0)),
        out_shape=jax.ShapeDtypeStruct((e, cap, D_MODEL), BF16),
        compiler_params=_cparams(("arbitrary", "arbitrary")),
        name="ffn",
    )(xe, wg, wu, wd)


def _combine_body(off_ref, h_ref, rank_ref, aff_ref, p_ref, gp_ref, wg_ref, wp_ref, gfin_ref, ye_ref,
                  o_ref, stack, sems, *, cap, last):
    t = pl.program_id(0)
    nt = pl.num_programs(0)
    tile, win, nsub = COMBINE_TILE, COMBINE_WIN, COMBINE_SUB
    bpt = tile // LANES
    kdim = N_EXPERTS * win

    def base(tt, e):
        return (off_ref[e, tt * bpt] // 16) * 16

    def copy(e, start, slot, sub):
        return pltpu.make_async_copy(ye_ref.at[e, pl.ds(pl.multiple_of(start, 16), win), :],
                                     stack.at[slot, sub, pl.ds(e * win, win), :], sems.at[slot, sub, e])

    def starts(tt, r):
        want = [base(tt, e) + r * win for e in range(N_EXPERTS)]
        return want, [jnp.minimum(w, cap - win) for w in want]

    def issue(tt, r, slot, sub):
        _, got = starts(tt, r)
        for e in range(N_EXPERTS):
            copy(e, got[e], slot, sub).start()

    def wait(tt, r, slot, sub):
        _, got = starts(tt, r)
        for e in range(N_EXPERTS):
            copy(e, got[e], slot, sub).wait()

    slot = t % 2

    @pl.when(t == 0)
    def _():
        for sub in range(nsub):
            issue(sub, 0, 0, sub)

    @pl.when(t + 1 < nt)
    def _():
        for sub in range(nsub):
            issue((t + 1) * nsub + sub, 0, 1 - slot, sub)

    lane16 = lax.broadcasted_iota(I32, (1, N_EXPERTS), 1)
    er = lax.broadcasted_iota(I32, (N_EXPERTS, kdim), 0)
    ec = lax.broadcasted_iota(I32, (N_EXPERTS, kdim), 1)
    expand = jnp.where(ec // win == er, 1.0, 0.0).astype(BF16)
    lane_in = (lax.broadcasted_iota(I32, (1, kdim), 1) % win).astype(F32)

    def row_of(vals):
        r = jnp.zeros((1, N_EXPERTS), I32)
        for e in range(N_EXPERTS):
            r = jnp.where(lane16 == e, vals[e], r)
        return r

    def accumulate(sub, r):
        rows = pl.ds(sub * tile, tile)
        want, got = starts(t * nsub + sub, r)
        wrow, grow = row_of(want), row_of(got)
        rk = rank_ref[rows, :]
        ok = jnp.logical_and(rk >= wrow, rk < grow + win)
        rel = jnp.where(ok, rk - grow, -1).astype(F32).astype(BF16)
        hit = _dot(rel, expand) == lane_in
        gates = _dot(aff_ref[rows, :].astype(BF16), expand)
        w = jnp.where(hit, gates, 0.0).astype(BF16)
        return _dot(w, stack[slot, sub])

    for sub in range(nsub):
        wait(t * nsub + sub, 0, slot, sub)
    for sub in range(nsub):
        rows = pl.ds(sub * tile, tile)
        o_ref[rows, :] = h_ref[rows, :] + accumulate(sub, 0)

    for sub in range(nsub):
        tt = t * nsub + sub
        nr = jnp.int32(1)
        for e in range(N_EXPERTS):
            span = off_ref[e, (tt + 1) * bpt] - base(tt, e)
            nr = jnp.maximum(nr, (span + win - 1) // win)

        def extra(r, carry, sub=sub, tt=tt):
            issue(tt, r, slot, sub)
            wait(tt, r, slot, sub)
            o_ref[pl.ds(sub * tile, tile), :] += accumulate(sub, r)
            return carry

        lax.fori_loop(1, nr, extra, 0)

    for sub in range(nsub):
        rows = pl.ds(sub * tile, tile)
        h2 = o_ref[rows, :]
        gate = jax.nn.sigmoid(_dot(_rms(h2, gp_ref[...]).astype(BF16), wg_ref[...]))
        h3 = h2 + _dot(p_ref[rows, :].astype(BF16), wp_ref[...]) * gate
        o_ref[rows, :] = _rms(h3, gfin_ref[...]) if last else h3


def _combine(off, h1, rank_tok, aff_tok, p, layer, gp, wg, wp, gfin, ye, cap, last):
    n = h1.shape[0]
    tile = COMBINE_TILE * COMBINE_SUB
    row = lambda i, o: (i, 0)
    fix = lambda i, o: (0, 0)
    prow = lambda i, o: (layer * (n // tile) + i, 0)
    return pl.pallas_call(
        functools.partial(_combine_body, cap=cap, last=last),
        grid_spec=pltpu.PrefetchScalarGridSpec(
            num_scalar_prefetch=1,
            grid=(n // tile,),
            in_specs=[pl.BlockSpec((tile, D_MODEL), row),
                      pl.BlockSpec((tile, N_EXPERTS), row),
                      pl.BlockSpec((tile, N_EXPERTS), row),
                      pl.BlockSpec((tile, PLE_DIM), prow),
                      pl.BlockSpec((1, D_MODEL), fix),
                      pl.BlockSpec((D_MODEL, D_MODEL), fix),
                      pl.BlockSpec((PLE_DIM, D_MODEL), fix),
                      pl.BlockSpec((1, D_MODEL), fix),
                      pl.BlockSpec(memory_space=pl.ANY)],
            out_specs=pl.BlockSpec((tile, D_MODEL), row),
            scratch_shapes=[pltpu.VMEM((2, COMBINE_SUB, N_EXPERTS * COMBINE_WIN, D_MODEL), BF16),
                            pltpu.SemaphoreType.DMA((2, COMBINE_SUB, N_EXPERTS))],
        ),
        out_shape=jax.ShapeDtypeStruct((n, D_MODEL), F32),
        compiler_params=_cparams(("arbitrary",)),
        name="combine",
    )(off, h1, rank_tok, aff_tok, p, gp, wg, wp, gfin, ye)


def _gather_rows(table, rank, cap):
    n, w = table.shape
    workers_per_expert = SC_CORES * SC_SUBCORES // N_EXPERTS
    per = cap // workers_per_expert
    nch = per // GATHER_ROWS
    mesh = plsc.VectorSubcoreMesh(core_axis_name="c", subcore_axis_name="s",
                                  num_cores=SC_CORES, num_subcores=SC_SUBCORES)
    cp = dataclasses.replace(pltpu.CompilerParams(), needs_layout_passes=False)

    @functools.partial(
        pl.kernel, mesh=mesh, compiler_params=cp,
        out_type=jax.ShapeDtypeStruct((N_EXPERTS * cap, w), I32),
        scratch_types=[pltpu.VMEM((RANK_CHUNK,), I32),
                       pltpu.VMEM((nch, GATHER_ROWS), I32),
                       pltpu.VMEM((GATHER_ROWS, w), I32),
                       pltpu.SemaphoreType.DMA],
        name="sc_gather")
    def gather(table_hbm, rank_hbm, out_hbm, rbuf, idx, rows, sem):
        wid = lax.axis_index("s") * SC_CORES + lax.axis_index("c")
        e = wid // workers_per_expert
        lo = (wid % workers_per_expert) * per
        lane = lax.iota(I32, SC_LANES)

        @pl.loop(0, n // RANK_CHUNK)
        def _(c):
            pltpu.sync_copy(rank_hbm.at[e, pl.ds(c * RANK_CHUNK, RANK_CHUNK)], rbuf)

            @pl.loop(0, RANK_CHUNK // SC_LANES)
            def _(i):
                rel = rbuf[pl.ds(i * SC_LANES, SC_LANES)] - lo
                mask = jnp.logical_and(rel >= 0, rel < per)
                rel = jnp.where(mask, rel, 0)
                tok = c * RANK_CHUNK + i * SC_LANES + lane
                plsc.store_scatter(idx, [rel // GATHER_ROWS, rel % GATHER_ROWS], tok, mask=mask)

        @pl.loop(0, nch)
        def _(c):
            pltpu.async_copy(table_hbm.at[idx.at[c]], rows, sem).wait()
            pltpu.sync_copy(rows, out_hbm.at[pl.ds(e * cap + lo + c * GATHER_ROWS, GATHER_ROWS)])

    return gather(table, rank).reshape(N_EXPERTS, cap, w)


def _trunk(x, p, prm, b, s):
    n = b * s
    nc = s // SSM_CHUNK
    cap = EC_CAPACITY_FACTOR * n // N_EXPERTS
    tables = _fourier_tables(s)
    h = x.reshape(n, D_MODEL)
    depth = prm['w_in'].shape[0]
    p_rows = p.reshape(depth * n, PLE_DIM)
    for l in range(depth):
        lp = prm['layers'][l]
        zf, zs, zqkv = _inproj(h, lp['g_mix'], lp['w_in'])
        mf = _fourier(zf, tables, prm['ccb'], prm['scb'], lp['w_fnet'], lp['g_f'], b, s)
        xl = _s5_in(zs, lp['s5_w_in'])
        pr, pi = _scan_powers(lp['s5_a_chunk'], int(math.log2(nc)))
        st = _s5_scan(xl, pr, pi, b, nc)
        ms = _s5_out(zs, st, lp['s5_m_intra'], lp['s5_w_out'], lp['s5_d'], lp['s5_glu_w'], lp['s5_glu_b'],
                     lp['g_s'])
        oa = _attn(zqkv, prm['slopes'], b, s)
        h1, m, aff = _outproj(h, mf, ms, oa, lp['g_a'], lp['w_out'], lp['g_ffn'], lp['w_router'])
        rank, off = _select(aff.T, cap)
        off = jnp.concatenate([off, jnp.full((N_EXPERTS, 1), cap, I32)], axis=1)
        xe = _gather_rows(m, rank, cap)
        ye = _ffn(xe, prm['w_gate'], prm['w_up'], prm['w_down'], l)
        h = _combine(off, h1, rank.T, aff, p_rows, l, lp['g_ple'], lp['w_ple_gate'],
                     lp['w_ple_proj'], prm['g_final'], ye, cap, l == depth - 1)
    return h.reshape(b, s, D_MODEL)


def kernel(x_prompt, x_sample, p_prompt, p_sample, norm_mix, w_in, w_fnet, ssm_a_re, ssm_a_im, ssm_log_dt, ssm_b_re, ssm_b_im, ssm_c_re, ssm_c_im, ssm_d, ssm_glu_w, ssm_glu_b, norm_branch, w_out, norm_ffn, w_router, w_exp_gate, w_exp_up, w_exp_down, norm_ple, w_ple_gate, w_ple_proj, norm_final):
    depth = w_in.shape[0]
    o1, o2 = FN_WIDTH, FN_WIDTH + SSM_WIDTH
    row = lambda v: v.reshape(1, -1).astype(F32)
    cc, sc = _dft_tables(FN_HEAD_DIM)
    tile4 = lambda mtx: _block_diag(jnp.broadcast_to(mtx[None], (FN_HEADS,) + mtx.shape)).astype(BF16)
    layers = []
    for l in range(depth):
        m_intra, s5_w_in, s5_w_out, a_chunk = _s5_matrices(
            ssm_a_re[l], ssm_a_im[l], ssm_log_dt[l], ssm_b_re[l], ssm_b_im[l], ssm_c_re[l], ssm_c_im[l])
        layers.append(dict(
            g_mix=row(norm_mix[l]), w_in=w_in[l].astype(BF16),
            w_fnet=_block_diag(w_fnet[l]).astype(BF16),
            g_f=row(norm_branch[l][:o1]), g_s=row(norm_branch[l][o1:o2]), g_a=row(norm_branch[l][o2:]),
            s5_m_intra=m_intra, s5_w_in=s5_w_in, s5_w_out=s5_w_out, s5_a_chunk=a_chunk,
            s5_d=row(ssm_d[l]), s5_glu_w=ssm_glu_w[l].astype(BF16), s5_glu_b=row(ssm_glu_b[l]),
            w_out=w_out[l].astype(BF16), g_ffn=row(norm_ffn[l]),
            w_router=w_router[l].astype(BF16),
            g_ple=row(norm_ple[l]), w_ple_gate=w_ple_gate[l].astype(BF16),
            w_ple_proj=w_ple_proj[l].astype(BF16)))
    slopes = jnp.asarray([2.0 ** (-8.0 * (i + 1) / ATT_HEADS) for i in range(ATT_HEADS)], F32)
    prm = dict(w_in=w_in, layers=layers, ccb=tile4(cc), scb=tile4(sc), slopes=slopes,
               w_gate=w_exp_gate.astype(BF16), w_up=w_exp_up.astype(BF16), w_down=w_exp_down.astype(BF16),
               g_final=row(norm_final))
    bp, sp = x_prompt.shape[0], x_prompt.shape[1]
    bs, ssq = x_sample.shape[0], x_sample.shape[1]
    y_prompt = _trunk(x_prompt, p_prompt, prm, bp, sp)
    y_sample = _trunk(x_sample, p_sample, prm, bs, ssq)
    return (y_prompt, y_sample)
```

```python
import dataclasses
import functools
import math

import jax
import jax.numpy as jnp
from jax import lax
from jax.experimental import pallas as pl
from jax.experimental.pallas import tpu as pltpu
from jax.experimental.pallas import tpu_sc as plsc

D_MODEL = 1024
FN_WIDTH = 256
FN_HEADS = 4
FN_HEAD_DIM = 64
SSM_WIDTH = 256
SSM_GROUP = 16
SSM_GROUPS = 16
SSM_STATE = 64
ATT_WIDTH = 512
ATT_HEAD_DIM = 64
ATT_HEADS = 8
DILATED_PATTERNS = ((128, 1), (512, 4), (2048, 16))
IN_PROJ_WIDTH = 2048
N_EXPERTS = 16
EC_CAPACITY_FACTOR = 2
D_FF_EXPERT = 2048
PLE_DIM = 256
RMS_EPS = 1e-6
NEG_INF = -1e30

LANES = 128
SSM_CHUNK = 8
SSM_ROW = SSM_CHUNK * SSM_WIDTH
SSM_NSTATE = SSM_GROUPS * SSM_STATE
S5_ROWS = 256
ATT_HALF = 64
ATT_UNROLL = 8
ATT_SPLIT = 4
ATT_OFFSETS = 3
FFN_ROWS = 1024
COMBINE_TILE = 256
COMBINE_SUB = 2
PROJ_ROWS = 1024
FFT_RADIX = 4
COMBINE_WIN = 64
VMEM_LIMIT = 56 * 1024 * 1024
SC_CORES = 2
SC_SUBCORES = 16
SC_LANES = 16
GATHER_ROWS = 64
RANK_CHUNK = 2048
PACKED = D_MODEL // 2

F32 = jnp.float32
BF16 = jnp.bfloat16
I32 = jnp.int32


def _cparams(sem):
    return pltpu.CompilerParams(dimension_semantics=sem, vmem_limit_bytes=VMEM_LIMIT)


def _rms(x, g):
    return x * lax.rsqrt(jnp.mean(x * x, axis=-1, keepdims=True) + RMS_EPS) * g


def _dot(a, b):
    return jnp.dot(a, b, preferred_element_type=F32)


def _dot_nt(a, b):
    return lax.dot_general(a, b, (((1,), (1,)), ((), ())), preferred_element_type=F32)


def _inproj_body(h_ref, g_ref, w_ref, zf_ref, zs_ref, zqkv_ref):
    a = _rms(h_ref[...], g_ref[...]).astype(BF16)
    z = _dot(a, w_ref[...])
    zf_ref[...] = z[:, :FN_WIDTH]
    zs_ref[...] = z[:, FN_WIDTH:FN_WIDTH + SSM_WIDTH]
    zqkv_ref[...] = z[:, FN_WIDTH + SSM_WIDTH:].astype(BF16)


def _inproj(h, g, w):
    n = h.shape[0]
    tm = PROJ_ROWS
    return pl.pallas_call(
        _inproj_body,
        grid=(n // tm,),
        in_specs=[pl.BlockSpec((tm, D_MODEL), lambda i: (i, 0)),
                  pl.BlockSpec((1, D_MODEL), lambda i: (0, 0)),
                  pl.BlockSpec((D_MODEL, IN_PROJ_WIDTH), lambda i: (0, 0))],
        out_specs=[pl.BlockSpec((tm, FN_WIDTH), lambda i: (i, 0)),
                   pl.BlockSpec((tm, SSM_WIDTH), lambda i: (i, 0)),
                   pl.BlockSpec((tm, 3 * ATT_WIDTH), lambda i: (i, 0))],
        out_shape=[jax.ShapeDtypeStruct((n, FN_WIDTH), F32),
                   jax.ShapeDtypeStruct((n, SSM_WIDTH), F32),
                   jax.ShapeDtypeStruct((n, 3 * ATT_WIDTH), BF16)],
        compiler_params=_cparams(("arbitrary",)),
        name="inproj",
    )(h, g, w)


def _fourier_body(xa_ref, xb_ref, cs_ref, ss_ref, tc_ref, ts_ref, cc_ref, sc_ref, wb_ref, g_ref, o_ref, *, s):
    quarter = s // FFT_RADIX
    def phase(q):
        return jnp.concatenate([xa_ref[0, pl.ds(q, quarter, stride=FFT_RADIX), :],
                                xb_ref[0, pl.ds(q, quarter, stride=FFT_RADIX), :]], axis=1)
    x4 = jnp.concatenate([phase(q) for q in range(FFT_RADIX)], axis=1).astype(BF16)
    y = _dot(cs_ref[...], x4)
    z = _dot(ss_ref[...], x4)
    wide = lambda t: jnp.concatenate([t] * (FN_WIDTH // LANES), axis=1)
    tcs, tss = [], []
    for q in range(FFT_RADIX):
        fc = y[:, q * FN_WIDTH:(q + 1) * FN_WIDTH]
        fs = z[:, q * FN_WIDTH:(q + 1) * FN_WIDTH]
        if q == 0:
            tcs.append(fc)
            tss.append(fs)
        else:
            c, sn = wide(tc_ref[q - 1]), wide(ts_ref[q - 1])
            tcs.append(c * fc - sn * fs)
            tss.append(c * fs + sn * fc)
    (c0, c1, c2, c3), (s0, s1, s2, s3) = tcs, tss
    parts = ((c0 + c1 + c2 + c3, s0 + s1 + s2 + s3),
             (c0 - s1 - c2 + s3, s0 + c1 - s2 - c3),
             (c0 - c1 + c2 - c3, s0 - s1 + s2 - s3),
             (c0 + s1 - c2 - s3, s0 - c1 - s2 + c3))
    for part, (yy, zz) in enumerate(parts):
        f = _dot(yy.astype(BF16), cc_ref[...]) - _dot(zz.astype(BF16), sc_ref[...])
        o = _dot(f.astype(BF16), wb_ref[...])
        o_ref[0, part] = _rms(o, g_ref[...]).astype(BF16)


def _dft_tables(n):
    def exact(rows):
        k = (rows[:, None] * jnp.arange(n, dtype=I32)[None, :]) % n
        ang = k.astype(F32) * (2.0 * math.pi / n)
        return jnp.cos(ang), jnp.sin(ang)
    if n <= LANES:
        return exact(jnp.arange(n, dtype=I32))
    ca, sa = exact(LANES * jnp.arange(n // LANES, dtype=I32))
    cb, sb = exact(jnp.arange(LANES, dtype=I32))
    cos = ca[:, None, :] * cb[None] - sa[:, None, :] * sb[None]
    sin = sa[:, None, :] * cb[None] + ca[:, None, :] * sb[None]
    return cos.reshape(n, n), sin.reshape(n, n)


def _fourier(zf, tables, ccb, scb, wb, g, b, s):
    cs, ss, tc, ts = tables
    quarter = s // FFT_RADIX
    tr = min(512, quarter)
    x = zf.reshape(b, s, FN_WIDTH)
    fix = lambda i, j: (0, 0)
    out = pl.pallas_call(
        functools.partial(_fourier_body, s=s),
        grid=(quarter // tr, b),
        in_specs=[pl.BlockSpec((1, s, LANES), lambda i, j: (j, 0, 0)),
                  pl.BlockSpec((1, s, LANES), lambda i, j: (j, 0, 1)),
                  pl.BlockSpec((tr, quarter), lambda i, j: (i, 0)),
                  pl.BlockSpec((tr, quarter), lambda i, j: (i, 0)),
                  pl.BlockSpec((FFT_RADIX - 1, tr, LANES), lambda i, j: (0, i, 0)),
                  pl.BlockSpec((FFT_RADIX - 1, tr, LANES), lambda i, j: (0, i, 0)),
                  pl.BlockSpec((FN_WIDTH, FN_WIDTH), fix),
                  pl.BlockSpec((FN_WIDTH, FN_WIDTH), fix),
                  pl.BlockSpec((FN_WIDTH, FN_WIDTH), fix),
                  pl.BlockSpec((1, FN_WIDTH), fix)],
        out_specs=pl.BlockSpec((1, FFT_RADIX, tr, FN_WIDTH), lambda i, j: (j, 0, i, 0)),
        out_shape=jax.ShapeDtypeStruct((b, FFT_RADIX, quarter, FN_WIDTH), BF16),
        compiler_params=_cparams(("arbitrary", "arbitrary")),
        name="fourier",
    )(x, x, cs, ss, tc, ts, ccb, scb, wb, g)
    return out.reshape(b * s, FN_WIDTH)


def _fourier_tables(s):
    quarter = s // FFT_RADIX
    cs, ss = _dft_tables(quarter)
    kq = jnp.arange(1, FFT_RADIX, dtype=F32)[:, None] * jnp.arange(quarter, dtype=F32)[None, :]
    ang = kq * (2.0 * math.pi / s)
    wide = lambda v: jnp.broadcast_to(v[:, :, None], (FFT_RADIX - 1, quarter, LANES))
    return cs.astype(BF16), ss.astype(BF16), wide(jnp.cos(ang)), wide(jnp.sin(ang))


def _block_diag(blocks):
    h, a, bb = blocks.shape
    eye = jnp.eye(h, dtype=blocks.dtype)
    return jnp.einsum('hab,hg->hagb', blocks, eye).reshape(h * a, h * bb)


def _s5_matrices(a_re, a_im, log_dt, b_re, b_im, c_re, c_im):
    t = SSM_CHUNK
    g, p, c = SSM_GROUPS, SSM_STATE, SSM_GROUP
    lam = lax.complex(a_re.astype(F32), a_im.astype(F32))
    dt = jnp.exp(log_dt.astype(F32))[..., None]
    abar = jnp.exp(lam * dt)
    bbar = ((abar - 1.0) / lam)[..., None] * lax.complex(b_re.astype(F32), b_im.astype(F32))
    cmat = lax.complex(c_re.astype(F32), c_im.astype(F32))
    ks = jnp.arange(t + 1, dtype=F32)
    apow = jnp.exp((lam * dt)[:, None] * ks[None, :, None, None])

    kern = jnp.real(jnp.einsum('dgcp,dkgp,dgpe->dkgce', cmat, apow[:, :t], bbar))
    gh, hw, hs = g // 2, SSM_WIDTH // 2, SSM_NSTATE // 2
    lags = jnp.arange(-(t - 1), t)
    pick = lambda m: m[:, None, None, None]
    klag = (jnp.where(pick(lags >= 0), kern[0][jnp.clip(lags, 0, t - 1)], 0.0)
            + jnp.where(pick(lags <= 0), kern[1][jnp.clip(-lags, 0, t - 1)], 0.0))
    eye_h = jnp.eye(gh, dtype=F32)
    lag = jnp.arange(t)[None, :] - jnp.arange(t)[:, None]
    def _intra(h):
        blocks = jnp.einsum('lgce,gh->lgehc', klag[:, h * gh:(h + 1) * gh], eye_h).reshape(2 * t - 1, hw, hw)
        return blocks[lag + t - 1].transpose(0, 2, 1, 3).reshape(t * hw, t * hw)
    m_intra = jnp.stack([_intra(0), _intra(1)])

    wf = apow[0, t - 1 - jnp.arange(t)][:, :, :, None] * bbar[0][None]
    wb = apow[1, jnp.arange(t)][:, :, :, None] * bbar[1][None]
    same_group = ((jnp.arange(t * hw)[:, None] // c) % gh) == (jnp.arange(hs)[None, :] // p)
    def _spread(x, h):
        xh = x[:, h * gh:(h + 1) * gh].reshape(t * hw, p)
        return jnp.where(same_group, jnp.tile(xh, (1, gh)), 0.0)
    def _parts(parts, h):
        return jnp.concatenate([sign * _spread(x, h) for sign, x in parts], axis=1)
    wf, wb = jnp.swapaxes(wf, 2, 3), jnp.swapaxes(wb, 2, 3)
    in_parts = [(1.0, jnp.real(wf)), (1.0, jnp.imag(wf)), (1.0, jnp.real(wb)), (1.0, jnp.imag(wb))]
    w_in = jnp.stack([_parts(in_parts, 0), _parts(in_parts, 1)])

    qf = cmat[0][None] * apow[0, 1 + jnp.arange(t)][:, :, None, :]
    qb = cmat[1][None] * apow[1, t - jnp.arange(t)][:, :, None, :]
    out_parts = [(1.0, jnp.real(qf)), (-1.0, jnp.imag(qf)), (1.0, jnp.real(qb)), (-1.0, jnp.imag(qb))]
    w_out = jnp.stack([_parts(out_parts, 0).T, _parts(out_parts, 1).T])

    a_chunk = lam * dt * t
    return m_intra.astype(BF16), w_in.astype(BF16), w_out.astype(BF16), a_chunk


def _scan_powers(a_chunk, nsteps):
    e = jnp.exp(a_chunk[None] * (2.0 ** jnp.arange(nsteps, dtype=F32))[:, None, None, None])
    e = e.reshape(nsteps, 2 * SSM_NSTATE)
    return jnp.real(e), jnp.imag(e)


def _half_steps(z_ref, tr):
    return jnp.concatenate([z_ref[pl.ds(r, tr, stride=SSM_CHUNK), :] for r in range(SSM_CHUNK)], axis=1)


def _halves(rows):
    return [pl.BlockSpec((rows, LANES), lambda i: (i, 0)), pl.BlockSpec((rows, LANES), lambda i: (i, 1))]


def _s5_in_body(za_ref, zb_ref, w_ref, o_ref):
    tr = o_ref.shape[0]
    hs = SSM_NSTATE // 2
    for h, z_ref in enumerate((za_ref, zb_ref)):
        res = _dot(_half_steps(z_ref, tr).astype(BF16), w_ref[h])
        for part in range(4):
            lo = part * SSM_NSTATE + h * hs
            o_ref[:, lo:lo + hs] = res[:, part * hs:(part + 1) * hs]


def _s5_in(zs, w_in):
    rows = zs.shape[0] // SSM_CHUNK
    tr = min(S5_ROWS, rows)
    width = 4 * SSM_NSTATE
    return pl.pallas_call(
        _s5_in_body,
        grid=(rows // tr,),
        in_specs=_halves(tr * SSM_CHUNK) + [
            pl.BlockSpec((2, SSM_ROW // 2, width // 2), lambda i: (0, 0, 0), pipeline_mode=pl.Buffered(1))],
        out_specs=pl.BlockSpec((tr, width), lambda i: (i, 0)),
        out_shape=jax.ShapeDtypeStruct((rows, width), F32),
        compiler_params=_cparams(("arbitrary",)),
        name="s5_in",
    )(zs, zs, w_in)


def _s5_scan_body(x_ref, pr_ref, pi_ref, o_ref, *, nc, nsteps):
    ns = SSM_NSTATE
    row = lax.broadcasted_iota(I32, (nc, 1), 0)

    def direction(part, table_lo, forward):
        def shifted(v, sh):
            if forward:
                return jnp.where(row >= sh, pltpu.roll(v, sh, 0), 0.0)
            return jnp.where(row < nc - sh, pltpu.roll(v, nc - sh, 0), 0.0)

        def column(c, carry):
            off = pl.multiple_of(c * LANES, LANES)
            re_at = pl.ds(part * ns + off, LANES)
            im_at = pl.ds((part + 1) * ns + off, LANES)
            tab = pl.ds(table_lo + off, LANES)
            re, im = x_ref[0, :, re_at], x_ref[0, :, im_at]
            for k in range(nsteps):
                ar, ai = pr_ref[k:k + 1, tab], pi_ref[k:k + 1, tab]
                sre, sim = shifted(re, 2 ** k), shifted(im, 2 ** k)
                re, im = re + ar * sre - ai * sim, im + ar * sim + ai * sre
            o_ref[0, :, re_at] = shifted(re, 1).astype(BF16)
            o_ref[0, :, im_at] = shifted(im, 1).astype(BF16)
            return carry

        lax.fori_loop(0, ns // LANES, column, 0)

    direction(0, 0, True)
    direction(2, ns, False)


def _s5_scan(xl, pr, pi, b, nc):
    nsteps = int(math.log2(nc))
    width = 4 * SSM_NSTATE
    x = xl.reshape(b, nc, width)
    out = pl.pallas_call(
        functools.partial(_s5_scan_body, nc=nc, nsteps=nsteps),
        grid=(b,),
        in_specs=[pl.BlockSpec((1, nc, width), lambda i: (i, 0, 0)),
                  pl.BlockSpec((nsteps, 2 * SSM_NSTATE), lambda i: (0, 0)),
                  pl.BlockSpec((nsteps, 2 * SSM_NSTATE), lambda i: (0, 0))],
        out_specs=pl.BlockSpec((1, nc, width), lambda i: (i, 0, 0)),
        out_shape=jax.ShapeDtypeStruct((b, nc, width), BF16),
        compiler_params=_cparams(("arbitrary",)),
        name="s5_scan",
    )(x, pr, pi)
    return out.reshape(b * nc, width)


def _s5_out_body(za_ref, zb_ref, s_ref, m_ref, w_ref, d_ref, gw_ref, gb_ref, g_ref, o_ref, nat_a, nat_b):
    tr = s_ref.shape[0]
    hs = SSM_NSTATE // 2
    us, ys = [], []
    for h, z_ref in enumerate((za_ref, zb_ref)):
        u = _half_steps(z_ref, tr)
        st = jnp.concatenate([s_ref[:, part * SSM_NSTATE + h * hs:part * SSM_NSTATE + (h + 1) * hs]
                              for part in range(4)], axis=1)
        us.append(u)
        ys.append(_dot(st, w_ref[h]) + _dot(u.astype(BF16), m_ref[h]))
    c0 = math.sqrt(2.0 / math.pi)
    for r in range(SSM_CHUNK):
        step = slice(r * LANES, (r + 1) * LANES)
        y = jnp.concatenate([ys[0][:, step], ys[1][:, step]], axis=1)
        u = jnp.concatenate([us[0][:, step], us[1][:, step]], axis=1)
        v = y + d_ref[...] * u
        gl = 0.5 * v * (1.0 + jnp.tanh(c0 * (v + 0.044715 * (v * v * v))))
        gate = jax.nn.sigmoid(_dot(gl.astype(BF16), gw_ref[...]) + gb_ref[...])
        out = _rms(gl * gate, g_ref[...])
        nat_a[pl.ds(r, tr, stride=SSM_CHUNK), :] = out[:, :LANES]
        nat_b[pl.ds(r, tr, stride=SSM_CHUNK), :] = out[:, LANES:]
    o_ref[:, :LANES] = nat_a[...].astype(BF16)
    o_ref[:, LANES:] = nat_b[...].astype(BF16)


def _s5_out(zs, states, m_intra, w_out, d, glu_w, glu_b, g):
    rows = states.shape[0]
    tr = min(S5_ROWS, rows)
    width = 4 * SSM_NSTATE
    fix = lambda i: (0, 0)
    fix3 = lambda i: (0, 0, 0)
    once = pl.Buffered(1)
    return pl.pallas_call(
        _s5_out_body,
        grid=(rows // tr,),
        in_specs=_halves(tr * SSM_CHUNK) + [
                  pl.BlockSpec((tr, width), lambda i: (i, 0)),
                  pl.BlockSpec((2, SSM_ROW // 2, SSM_ROW // 2), fix3, pipeline_mode=once),
                  pl.BlockSpec((2, width // 2, SSM_ROW // 2), fix3, pipeline_mode=once),
                  pl.BlockSpec((1, SSM_WIDTH), fix), pl.BlockSpec((SSM_WIDTH, SSM_WIDTH), fix),
                  pl.BlockSpec((1, SSM_WIDTH), fix), pl.BlockSpec((1, SSM_WIDTH), fix)],
        out_specs=pl.BlockSpec((tr * SSM_CHUNK, SSM_WIDTH), lambda i: (i, 0)),
        out_shape=jax.ShapeDtypeStruct((rows * SSM_CHUNK, SSM_WIDTH), BF16),
        scratch_shapes=[pltpu.VMEM((tr * SSM_CHUNK, LANES), F32) for _ in range(SSM_WIDTH // LANES)],
        compiler_params=_cparams(("arbitrary",)),
        name="s5_out",
    )(zs, zs, states, m_intra, w_out, d, glu_w, glu_b, g)


def _attn_geometry(s, d):
    ln = s // d
    bq = min(128, ln)
    bk = min(bq + 2 * ATT_HALF, ln)
    return ln, bq, bk, ln // bq


def _attn_body(slope_ref, q_ref, k_ref, v_ref, o_ref, nat, qd, kd, vd, qdb, kdb, vdb, b0, b1, b2, *acc, s):
    hp = pl.program_id(1)
    s4 = s // ATT_SPLIT
    lane = lax.broadcasted_iota(I32, (1, LANES), 1)
    first = lane < ATT_HEAD_DIM
    second = jnp.logical_not(first)
    slopes = (slope_ref[2 * hp], slope_ref[2 * hp + 1])
    log2e = math.log2(math.e)
    scale = ATT_HEAD_DIM ** -0.5 * log2e

    @pl.when(pl.program_id(0) == 0)
    def _():
        for (_, d), bias in zip(DILATED_PATTERNS, (b0, b1, b2)):
            _, bq, bk, _ = _attn_geometry(s, d)
            jk = lax.broadcasted_iota(I32, (1, bk), 1)
            for o in range(ATT_OFFSETS):
                rel = jnp.abs(lax.broadcasted_iota(I32, (bq, 1), 0) + o * ATT_HALF - jk)
                dist = (d * rel).astype(F32) * log2e
                for hh in range(2):
                    bias[hp, 2 * o + hh] = jnp.where(rel <= ATT_HALF, -slopes[hh] * dist, NEG_INF)

    for src, dst_f, dst_b in ((q_ref, qd, qdb), (k_ref, kd, kdb), (v_ref, vd, vdb)):
        nat[...] = src[...].astype(F32)
        for c in range(ATT_SPLIT):
            x = nat[pl.ds(c, s4, stride=ATT_SPLIT), :]
            dst_f[c * s4:(c + 1) * s4, :] = x
            dst_b[c * s4:(c + 1) * s4, :] = x.astype(BF16)

    def pattern(p, d, refs, bias, locate, stride):
        acc_o, acc_m, acc_l = acc[3 * p], acc[3 * p + 1], acc[3 * p + 2]
        ln, bq, bk, nqb = _attn_geometry(s, d)

        def rows(c, j, size):
            start = locate(c, j)
            if stride == 1:
                return pl.ds(pl.multiple_of(start, ATT_HALF), size)
            return pl.ds(start, size, stride=stride)

        def scores(i):
            c = i // nqb
            j0 = (i % nqb) * bq
            ks = jnp.clip(j0 - ATT_HALF, 0, ln - bk)
            q = refs[0][rows(c, j0, bq), :].astype(BF16)
            k = refs[1][rows(c, ks, bk), :].astype(BF16)
            v = refs[2][rows(c, ks, bk), :].astype(BF16)
            off = (j0 - ks) // ATT_HALF
            scs = []
            for hh in range(2):
                qm = jnp.where(first if hh == 0 else second, q, jnp.zeros_like(q))
                scs.append(_dot_nt(qm, k) * scale + bias[hp, 2 * off + hh])
            return rows(c, j0, bq), v, scs

        def softmax(sc):
            m = jnp.max(sc, axis=-1, keepdims=True)
            pe = jnp.exp2(sc - m)
            return pe.astype(BF16), m, jnp.sum(pe, axis=-1, keepdims=True)

        def group(g, carry):
            staged = [scores(g * ATT_UNROLL + u) for u in range(ATT_UNROLL)]
            soft = [[softmax(sc) for sc in scs] for _, _, scs in staged]
            for (dst, v, _), ((p0, m0, l0), (p1, m1, l1)) in zip(staged, soft):
                acc_o[dst, :] = jnp.where(first, _dot(p0, v), _dot(p1, v))
                acc_m[dst, :] = jnp.where(first, m0, m1)
                acc_l[dst, :] = jnp.where(first, l0, l1)
            return carry

        lax.fori_loop(0, d * nqb // ATT_UNROLL, group, 0)

    (_, d1), (_, d2), (_, d3) = DILATED_PATTERNS
    assert d1 == 1 and d2 == ATT_SPLIT and d3 == ATT_SPLIT * ATT_SPLIT
    pattern(0, d1, (q_ref, k_ref, v_ref), b0, lambda c, j: j, 1)
    pattern(1, d2, (qdb, kdb, vdb), b1, lambda c, j: c * s4 + j, 1)
    pattern(2, d3, (qd, kd, vd), b2,
            lambda c, j: (c % ATT_SPLIT) * s4 + c // ATT_SPLIT + ATT_SPLIT * j, ATT_SPLIT)

    for c in range(ATT_SPLIT):
        part = pl.ds(c, s4, stride=ATT_SPLIT)
        blk = slice(c * s4, (c + 1) * s4)
        ms = (acc[1][part, :], acc[4][blk, :], acc[7][blk, :])
        os_ = (acc[0][part, :], acc[3][blk, :], acc[6][blk, :])
        ls = (acc[2][part, :], acc[5][blk, :], acc[8][blk, :])
        m = jnp.maximum(jnp.maximum(ms[0], ms[1]), ms[2])
        num = jnp.zeros((s4, LANES), F32)
        den = jnp.zeros((s4, LANES), F32)
        for p in range(3):
            w = jnp.exp2(ms[p] - m)
            num = num + w * os_[p]
            den = den + w * ls[p]
        nat[part, :] = num / den
    o_ref[...] = nat[...].astype(BF16)


def _attn(zqkv, slopes, b, s):
    n = b * s
    nhp = ATT_HEADS // 2
    col = lambda off: (lambda i, j, sl: (i, off + j))
    return pl.pallas_call(
        functools.partial(_attn_body, s=s),
        grid_spec=pltpu.PrefetchScalarGridSpec(
            num_scalar_prefetch=1,
            grid=(b, nhp),
            in_specs=[pl.BlockSpec((s, LANES), col(0)),
                      pl.BlockSpec((s, LANES), col(nhp)),
                      pl.BlockSpec((s, LANES), col(2 * nhp))],
            out_specs=pl.BlockSpec((s, LANES), lambda i, j, sl: (i, j)),
            scratch_shapes=([pltpu.VMEM((s, LANES), F32) for _ in range(4)]
                            + [pltpu.VMEM((s, LANES), BF16) for _ in range(3)]
                            + [pltpu.VMEM((nhp, 2 * ATT_OFFSETS) + _attn_geometry(s, d)[1:3], F32)
                               for _, d in DILATED_PATTERNS]
                            + [pltpu.VMEM((s, LANES), F32) for _ in range(9)]),
        ),
        out_shape=jax.ShapeDtypeStruct((n, ATT_WIDTH), BF16),
        compiler_params=_cparams(("arbitrary", "arbitrary")),
        name="attn",
    )(slopes, zqkv, zqkv, zqkv)


def _outproj_body(h_ref, mf_ref, ms_ref, oa_ref, ga_ref, w_ref, gf_ref, wr_ref,
                  h1_ref, m_ref, aff_ref):
    oa = _rms(oa_ref[...].astype(F32), ga_ref[...]).astype(BF16)
    o1, o2 = FN_WIDTH, FN_WIDTH + SSM_WIDTH
    acc = h_ref[...] + _dot(mf_ref[...], w_ref[0:o1, :]) + _dot(ms_ref[...], w_ref[o1:o2, :])
    acc = acc + _dot(oa, w_ref[o2:, :])
    h1_ref[...] = acc
    m = _rms(acc, gf_ref[...]).astype(BF16)
    bits = pltpu.bitcast(m.astype(F32), I32)
    m_ref[...] = lax.shift_right_logical(bits[:, :PACKED], 16) | (bits[:, PACKED:] & jnp.int32(-65536))
    lg = _dot(m, wr_ref[...])
    e = jnp.exp(lg - jnp.max(lg, axis=1, keepdims=True))
    aff_ref[...] = e / jnp.sum(e, axis=1, keepdims=True)


def _outproj(h, mf, ms, oa, ga, w, gf, wr):
    n = h.shape[0]
    tm = PROJ_ROWS
    row = lambda i: (i, 0)
    fix = lambda i: (0, 0)
    return pl.pallas_call(
        _outproj_body,
        grid=(n // tm,),
        in_specs=[pl.BlockSpec((tm, D_MODEL), row), pl.BlockSpec((tm, FN_WIDTH), row),
                  pl.BlockSpec((tm, SSM_WIDTH), row), pl.BlockSpec((tm, ATT_WIDTH), row),
                  pl.BlockSpec((1, ATT_WIDTH), fix), pl.BlockSpec((D_MODEL, D_MODEL), fix),
                  pl.BlockSpec((1, D_MODEL), fix), pl.BlockSpec((D_MODEL, N_EXPERTS), fix)],
        out_specs=[pl.BlockSpec((tm, D_MODEL), row), pl.BlockSpec((tm, PACKED), row),
                   pl.BlockSpec((tm, N_EXPERTS), row)],
        out_shape=[jax.ShapeDtypeStruct((n, D_MODEL), F32),
                   jax.ShapeDtypeStruct((n, PACKED), I32),
                   jax.ShapeDtypeStruct((n, N_EXPERTS), F32)],
        compiler_params=_cparams(("arbitrary",)),
        name="outproj",
    )(h, mf, ms, oa, ga, w, gf, wr)


def _select_body(aff_ref, rank_ref, off_ref, thr_ref, *, cap, nb):
    r128 = lax.broadcasted_iota(I32, (LANES, LANES), 0)
    c128 = lax.broadcasted_iota(I32, (LANES, LANES), 1)
    upper_incl = jnp.where(r128 <= c128, 1.0, 0.0).astype(BF16)
    ones = jnp.ones((LANES, LANES), BF16)
    rb = lax.broadcasted_iota(I32, (nb, nb), 0)
    cb = lax.broadcasted_iota(I32, (nb, nb), 1)
    lower_strict = jnp.where(cb < rb, 1.0, 0.0).astype(BF16)
    upper_strict = jnp.where(rb < cb, 1.0, 0.0).astype(BF16)
    ones8 = jnp.ones((8, LANES), BF16)

    def count(mask):
        c = jnp.sum(jnp.where(mask, 1.0, 0.0), axis=0, keepdims=True)
        return jnp.sum(c, axis=1, keepdims=True)

    bits_all = pltpu.bitcast(aff_ref[...], I32)

    def bitstep(i, t):
        cand = t | jnp.left_shift(jnp.int32(1), 30 - i)
        above = jnp.sum(jnp.where(bits_all >= cand, 1.0, 0.0), axis=1, keepdims=True)
        return jnp.where(jnp.sum(above, axis=2, keepdims=True) >= cap, cand, t)

    thr = lax.fori_loop(0, 31, bitstep, jnp.zeros((N_EXPERTS, 1, 1), I32))
    thr_ref[...] = jnp.broadcast_to(thr, thr_ref.shape)

    def prefix(mask):
        mb = jnp.where(mask, 1.0, 0.0).astype(BF16)
        incl = _dot(mb, upper_incl)
        tot = _dot(mb, ones)
        offs = _dot(lower_strict, tot.astype(BF16))
        return offs + incl - 1.0, mb

    def per_expert(e, carry):
        bits = pltpu.bitcast(aff_ref[e], I32)
        t = thr_ref[e][0:1, 0:1]
        gt = bits > t
        eq = bits == t
        need = cap - count(gt)
        eq_rank, _ = prefix(eq)
        sel = jnp.logical_or(gt, jnp.logical_and(eq, eq_rank < need))
        rank, mb = prefix(sel)
        rank_ref[e] = jnp.where(sel, rank.astype(I32), -1)
        tot_row = _dot_nt(ones8, mb)
        off_row = _dot(tot_row.astype(BF16), upper_strict)
        off_ref[pl.ds(e, 1), :] = off_row[0:1, :].astype(I32)
        return carry

    lax.fori_loop(0, N_EXPERTS, per_expert, 0)


def _select(aff_t, cap):
    n = aff_t.shape[1]
    nb = n // LANES
    a3 = aff_t.reshape(N_EXPERTS, nb, LANES)
    rank, off = pl.pallas_call(
        functools.partial(_select_body, cap=cap, nb=nb),
        grid=(1,),
        in_specs=[pl.BlockSpec((N_EXPERTS, nb, LANES), lambda i: (0, 0, 0))],
        out_specs=[pl.BlockSpec((N_EXPERTS, nb, LANES), lambda i: (0, 0, 0)),
                   pl.BlockSpec((N_EXPERTS, nb), lambda i: (0, 0))],
        out_shape=[jax.ShapeDtypeStruct((N_EXPERTS, nb, LANES), I32),
                   jax.ShapeDtypeStruct((N_EXPERTS, nb), I32)],
        scratch_shapes=[pltpu.VMEM((N_EXPERTS, 8, LANES), I32)],
        compiler_params=_cparams(("arbitrary",)),
        name="select",
    )(a3)
    return rank.reshape(N_EXPERTS, n), off


def _ffn_body(x_ref, wg_ref, wu_ref, wd_ref, o_ref):
    w = x_ref[...]
    x = jnp.concatenate([pltpu.bitcast(lax.shift_left(w, 16), F32),
                         pltpu.bitcast(w & jnp.int32(-65536), F32)], axis=1).astype(BF16)
    tf = 512
    acc = jnp.zeros(o_ref.shape, F32)
    for j in range(D_FF_EXPERT // tf):
        g = _dot(x, wg_ref[:, j * tf:(j + 1) * tf])
        u = _dot(x, wu_ref[:, j * tf:(j + 1) * tf])
        hdn = (g * jax.nn.sigmoid(g) * u).astype(BF16)
        acc = acc + _dot(hdn, wd_ref[j * tf:(j + 1) * tf, :])
    o_ref[...] = acc.astype(BF16)


def _ffn(xe, wg, wu, wd, layer):
    e, cap, _ = xe.shape
    tm = min(FFN_ROWS, cap)
    return pl.pallas_call(
        _ffn_body,
        grid=(e, cap // tm),
        in_specs=[pl.BlockSpec((None, tm, PACKED), lambda i, j: (i, j, 0)),
                  pl.BlockSpec((None, None, D_MODEL, D_FF_EXPERT), lambda i, j: (layer, i, 0, 0)),
                  pl.BlockSpec((None, None, D_MODEL, D_FF_EXPERT), lambda i, j: (layer, i, 0, 0)),
                  pl.BlockSpec((None, None, D_FF_EXPERT, D_MODEL), lambda i, j: (layer, i, 0, 0))],
        out_specs=pl.BlockSpec((None, tm, D_MODEL), lambda i, j: (i, j, 0)),
        out_shape=jax.ShapeDtypeStruct((e, cap, D_MODEL), BF16),
        compiler_params=_cparams(("arbitrary", "arbitrary")),
        name="ffn",
    )(xe, wg, wu, wd)


def _combine_body(off_ref, h_ref, rank_ref, aff_ref, p_ref, gp_ref, wg_ref, wp_ref, gfin_ref, ye_ref,
                  o_ref, stack, sems, *, cap, last):
    t = pl.program_id(0)
    nt = pl.num_programs(0)
    tile, win, nsub = COMBINE_TILE, COMBINE_WIN, COMBINE_SUB
    bpt = tile // LANES
    kdim = N_EXPERTS * win

    def base(tt, e):
        return (off_ref[e, tt * bpt] // 16) * 16

    def copy(e, start, slot, sub):
        return pltpu.make_async_copy(ye_ref.at[e, pl.ds(pl.multiple_of(start, 16), win), :],
                                     stack.at[slot, sub, pl.ds(e * win, win), :], sems.at[slot, sub, e])

    def starts(tt, r):
        want = [base(tt, e) + r * win for e in range(N_EXPERTS)]
        return want, [jnp.minimum(w, cap - win) for w in want]

    def issue(tt, r, slot, sub):
        _, got = starts(tt, r)
        for e in range(N_EXPERTS):
            copy(e, got[e], slot, sub).start()

    def wait(tt, r, slot, sub):
        _, got = starts(tt, r)
        for e in range(N_EXPERTS):
            copy(e, got[e], slot, sub).wait()

    slot = t % 2

    @pl.when(t == 0)
    def _():
        for sub in range(nsub):
            issue(sub, 0, 0, sub)

    @pl.when(t + 1 < nt)
    def _():
        for sub in range(nsub):
            issue((t + 1) * nsub + sub, 0, 1 - slot, sub)

    lane16 = lax.broadcasted_iota(I32, (1, N_EXPERTS), 1)
    er = lax.broadcasted_iota(I32, (N_EXPERTS, kdim), 0)
    ec = lax.broadcasted_iota(I32, (N_EXPERTS, kdim), 1)
    expand = jnp.where(ec // win == er, 1.0, 0.0).astype(BF16)
    lane_in = (lax.broadcasted_iota(I32, (1, kdim), 1) % win).astype(F32)

    def row_of(vals):
        r = jnp.zeros((1, N_EXPERTS), I32)
        for e in range(N_EXPERTS):
            r = jnp.where(lane16 == e, vals[e], r)
        return r

    def accumulate(sub, r):
        rows = pl.ds(sub * tile, tile)
        want, got = starts(t * nsub + sub, r)
        wrow, grow = row_of(want), row_of(got)
        rk = rank_ref[rows, :]
        ok = jnp.logical_and(rk >= wrow, rk < grow + win)
        rel = jnp.where(ok, rk - grow, -1).astype(F32).astype(BF16)
        hit = _dot(rel, expand) == lane_in
        gates = _dot(aff_ref[rows, :].astype(BF16), expand)
        w = jnp.where(hit, gates, 0.0).astype(BF16)
        return _dot(w, stack[slot, sub])

    for sub in range(nsub):
        wait(t * nsub + sub, 0, slot, sub)
    for sub in range(nsub):
        rows = pl.ds(sub * tile, tile)
        o_ref[rows, :] = h_ref[rows, :] + accumulate(sub, 0)

    for sub in range(nsub):
        tt = t * nsub + sub
        nr = jnp.int32(1)
        for e in range(N_EXPERTS):
            span = off_ref[e, (tt + 1) * bpt] - base(tt, e)
            nr = jnp.maximum(nr, (span + win - 1) // win)

        def extra(r, carry, sub=sub, tt=tt):
            issue(tt, r, slot, sub)
            wait(tt, r, slot, sub)
            o_ref[pl.ds(sub * tile, tile), :] += accumulate(sub, r)
            return carry

        lax.fori_loop(1, nr, extra, 0)

    for sub in range(nsub):
        rows = pl.ds(sub * tile, tile)
        h2 = o_ref[rows, :]
        gate = jax.nn.sigmoid(_dot(_rms(h2, gp_ref[...]).astype(BF16), wg_ref[...]))
        h3 = h2 + _dot(p_ref[rows, :].astype(BF16), wp_ref[...]) * gate
        o_ref[rows, :] = _rms(h3, gfin_ref[...]) if last else h3


def _combine(off, h1, rank_tok, aff_tok, p, layer, gp, wg, wp, gfin, ye, cap, last):
    n = h1.shape[0]
    tile = COMBINE_TILE * COMBINE_SUB
    row = lambda i, o: (i, 0)
    fix = lambda i, o: (0, 0)
    prow = lambda i, o: (layer * (n // tile) + i, 0)
    return pl.pallas_call(
        functools.partial(_combine_body, cap=cap, last=last),
        grid_spec=pltpu.PrefetchScalarGridSpec(
            num_scalar_prefetch=1,
            grid=(n // tile,),
            in_specs=[pl.BlockSpec((tile, D_MODEL), row),
                      pl.BlockSpec((tile, N_EXPERTS), row),
                      pl.BlockSpec((tile, N_EXPERTS), row),
                      pl.BlockSpec((tile, PLE_DIM), prow),
                      pl.BlockSpec((1, D_MODEL), fix),
                      pl.BlockSpec((D_MODEL, D_MODEL), fix),
                      pl.BlockSpec((PLE_DIM, D_MODEL), fix),
                      pl.BlockSpec((1, D_MODEL), fix),
                      pl.BlockSpec(memory_space=pl.ANY)],
            out_specs=pl.BlockSpec((tile, D_MODEL), row),
            scratch_shapes=[pltpu.VMEM((2, COMBINE_SUB, N_EXPERTS * COMBINE_WIN, D_MODEL), BF16),
                            pltpu.SemaphoreType.DMA((2, COMBINE_SUB, N_EXPERTS))],
        ),
        out_shape=jax.ShapeDtypeStruct((n, D_MODEL), F32),
        compiler_params=_cparams(("arbitrary",)),
        name="combine",
    )(off, h1, rank_tok, aff_tok, p, gp, wg, wp, gfin, ye)


def _gather_rows(table, rank, cap):
    n, w = table.shape
    workers_per_expert = SC_CORES * SC_SUBCORES // N_EXPERTS
    per = cap // workers_per_expert
    nch = per // GATHER_ROWS
    mesh = plsc.VectorSubcoreMesh(core_axis_name="c", subcore_axis_name="s",
                                  num_cores=SC_CORES, num_subcores=SC_SUBCORES)
    cp = dataclasses.replace(pltpu.CompilerParams(), needs_layout_passes=False)

    @functools.partial(
        pl.kernel, mesh=mesh, compiler_params=cp,
        out_type=jax.ShapeDtypeStruct((N_EXPERTS * cap, w), I32),
        scratch_types=[pltpu.VMEM((RANK_CHUNK,), I32),
                       pltpu.VMEM((nch, GATHER_ROWS), I32),
                       pltpu.VMEM((GATHER_ROWS, w), I32),
                       pltpu.SemaphoreType.DMA],
        name="sc_gather")
    def gather(table_hbm, rank_hbm, out_hbm, rbuf, idx, rows, sem):
        wid = lax.axis_index("s") * SC_CORES + lax.axis_index("c")
        e = wid // workers_per_expert
        lo = (wid % workers_per_expert) * per
        lane = lax.iota(I32, SC_LANES)

        @pl.loop(0, n // RANK_CHUNK)
        def _(c):
            pltpu.sync_copy(rank_hbm.at[e, pl.ds(c * RANK_CHUNK, RANK_CHUNK)], rbuf)

            @pl.loop(0, RANK_CHUNK // SC_LANES)
            def _(i):
                rel = rbuf[pl.ds(i * SC_LANES, SC_LANES)] - lo
                mask = jnp.logical_and(rel >= 0, rel < per)
                rel = jnp.where(mask, rel, 0)
                tok = c * RANK_CHUNK + i * SC_LANES + lane
                plsc.store_scatter(idx, [rel // GATHER_ROWS, rel % GATHER_ROWS], tok, mask=mask)

        @pl.loop(0, nch)
        def _(c):
            pltpu.async_copy(table_hbm.at[idx.at[c]], rows, sem).wait()
            pltpu.sync_copy(rows, out_hbm.at[pl.ds(e * cap + lo + c * GATHER_ROWS, GATHER_ROWS)])

    return gather(table, rank).reshape(N_EXPERTS, cap, w)


def _trunk(x, p, prm, b, s):
    n = b * s
    nc = s // SSM_CHUNK
    cap = EC_CAPACITY_FACTOR * n // N_EXPERTS
    tables = _fourier_tables(s)
    h = x.reshape(n, D_MODEL)
    depth = prm['w_in'].shape[0]
    p_rows = p.reshape(depth * n, PLE_DIM)
    for l in range(depth):
        lp = prm['layers'][l]
        zf, zs, zqkv = _inproj(h, lp['g_mix'], lp['w_in'])
        mf = _fourier(zf, tables, prm['ccb'], prm['scb'], lp['w_fnet'], lp['g_f'], b, s)
        xl = _s5_in(zs, lp['s5_w_in'])
        pr, pi = _scan_powers(lp['s5_a_chunk'], int(math.log2(nc)))
        st = _s5_scan(xl, pr, pi, b, nc)
        ms = _s5_out(zs, st, lp['s5_m_intra'], lp['s5_w_out'], lp['s5_d'], lp['s5_glu_w'], lp['s5_glu_b'],
                     lp['g_s'])
        oa = _attn(zqkv, prm['slopes'], b, s)
        h1, m, aff = _outproj(h, mf, ms, oa, lp['g_a'], lp['w_out'], lp['g_ffn'], lp['w_router'])
        rank, off = _select(aff.T, cap)
        off = jnp.concatenate([off, jnp.full((N_EXPERTS, 1), cap, I32)], axis=1)
        xe = _gather_rows(m, rank, cap)
        ye = _ffn(xe, prm['w_gate'], prm['w_up'], prm['w_down'], l)
        h = _combine(off, h1, rank.T, aff, p_rows, l, lp['g_ple'], lp['w_ple_gate'],
                     lp['w_ple_proj'], prm['g_final'], ye, cap, l == depth - 1)
    return h.reshape(b, s, D_MODEL)


def kernel(x_prompt, x_sample, p_prompt, p_sample, norm_mix, w_in, w_fnet, ssm_a_re, ssm_a_im, ssm_log_dt, ssm_b_re, ssm_b_im, ssm_c_re, ssm_c_im, ssm_d, ssm_glu_w, ssm_glu_b, norm_branch, w_out, norm_ffn, w_router, w_exp_gate, w_exp_up, w_exp_down, norm_ple, w_ple_gate, w_ple_proj, norm_final):
    depth = w_in.shape[0]
    o1, o2 = FN_WIDTH, FN_WIDTH + SSM_WIDTH
    row = lambda v: v.reshape(1, -1).astype(F32)
    cc, sc = _dft_tables(FN_HEAD_DIM)
    tile4 = lambda mtx: _block_diag(jnp.broadcast_to(mtx[None], (FN_HEADS,) + mtx.shape)).astype(BF16)
    layers = []
    for l in range(depth):
        m_intra, s5_w_in, s5_w_out, a_chunk = _s5_matrices(
            ssm_a_re[l], ssm_a_im[l], ssm_log_dt[l], ssm_b_re[l], ssm_b_im[l], ssm_c_re[l], ssm_c_im[l])
        layers.append(dict(
            g_mix=row(norm_mix[l]), w_in=w_in[l].astype(BF16),
            w_fnet=_block_diag(w_fnet[l]).astype(BF16),
            g_f=row(norm_branch[l][:o1]), g_s=row(norm_branch[l][o1:o2]), g_a=row(norm_branch[l][o2:]),
            s5_m_intra=m_intra, s5_w_in=s5_w_in, s5_w_out=s5_w_out, s5_a_chunk=a_chunk,
            s5_d=row(ssm_d[l]), s5_glu_w=ssm_glu_w[l].astype(BF16), s5_glu_b=row(ssm_glu_b[l]),
            w_out=w_out[l].astype(BF16), g_ffn=row(norm_ffn[l]),
            w_router=w_router[l].astype(BF16),
            g_ple=row(norm_ple[l]), w_ple_gate=w_ple_gate[l].astype(BF16),
            w_ple_proj=w_ple_proj[l].astype(BF16)))
    slopes = jnp.asarray([2.0 ** (-8.0 * (i + 1) / ATT_HEADS) for i in range(ATT_HEADS)], F32)
    prm = dict(w_in=w_in, layers=layers, ccb=tile4(cc), scb=tile4(sc), slopes=slopes,
               w_gate=w_exp_gate.astype(BF16), w_up=w_exp_up.astype(BF16), w_down=w_exp_down.astype(BF16),
               g_final=row(norm_final))
    bp, sp = x_prompt.shape[0], x_prompt.shape[1]
    bs, ssq = x_sample.shape[0], x_sample.shape[1]
    y_prompt = _trunk(x_prompt, p_prompt, prm, bp, sp)
    y_sample = _trunk(x_sample, p_sample, prm, bs, ssq)
    return (y_prompt, y_sample)
```

```python
import dataclasses
import functools
import math

import jax
import jax.numpy as jnp
from jax import lax
from jax.experimental import pallas as pl
from jax.experimental.pallas import tpu as pltpu
from jax.experimental.pallas import tpu_sc as plsc

D_MODEL = 1024
FN_WIDTH = 256
FN_HEADS = 4
FN_HEAD_DIM = 64
SSM_WIDTH = 256
SSM_GROUP = 16
SSM_GROUPS = 16
SSM_STATE = 64
ATT_WIDTH = 512
ATT_HEAD_DIM = 64
ATT_HEADS = 8
DILATED_PATTERNS = ((128, 1), (512, 4), (2048, 16))
IN_PROJ_WIDTH = 2048
N_EXPERTS = 16
EC_CAPACITY_FACTOR = 2
D_FF_EXPERT = 2048
PLE_DIM = 256
RMS_EPS = 1e-6
NEG_INF = -1e30

LANES = 128
SUBLANES = 8
BF16_ROWS = 16
FOURIER_ROWS = 512
FFN_COLS = 512
SSM_CHUNK = 8
SSM_ROW = SSM_CHUNK * SSM_WIDTH
SSM_NSTATE = SSM_GROUPS * SSM_STATE
S5_ROWS = 256
ATT_HALF = 64
ATT_UNROLL = 8
ATT_SPLIT = 4
ATT_OFFSETS = 3
FFN_ROWS = 1024
COMBINE_TILE = 256
COMBINE_SUB = 4
PROJ_ROWS = 1024
FFT_RADIX = 4
COMBINE_WIN = 64
VMEM_LIMIT = 56 * 1024 * 1024
SC_CORES = 2
SC_SUBCORES = 16
SC_LANES = 16
GATHER_ROWS = 64
RANK_CHUNK = 2048
PACKED = D_MODEL // 2

F32 = jnp.float32
BF16 = jnp.bfloat16
I32 = jnp.int32


def _cparams(sem):
    return pltpu.CompilerParams(dimension_semantics=sem, vmem_limit_bytes=VMEM_LIMIT)


def _rms(x, g):
    return x * lax.rsqrt(jnp.mean(x * x, axis=-1, keepdims=True) + RMS_EPS) * g


def _dot(a, b):
    return jnp.dot(a, b, preferred_element_type=F32)


def _dot_nt(a, b):
    return lax.dot_general(a, b, (((1,), (1,)), ((), ())), preferred_element_type=F32)


def _inproj_body(h_ref, g_ref, w_ref, zf_ref, zs_ref, zqkv_ref):
    a = _rms(h_ref[...], g_ref[...]).astype(BF16)
    z = _dot(a, w_ref[...])
    zf_ref[...] = z[:, :FN_WIDTH]
    zs_ref[...] = z[:, FN_WIDTH:FN_WIDTH + SSM_WIDTH]
    zqkv_ref[...] = z[:, FN_WIDTH + SSM_WIDTH:].astype(BF16)


def _inproj(h, g, w):
    n = h.shape[0]
    tm = PROJ_ROWS
    return pl.pallas_call(
        _inproj_body,
        grid=(n // tm,),
        in_specs=[pl.BlockSpec((tm, D_MODEL), lambda i: (i, 0)),
                  pl.BlockSpec((1, D_MODEL), lambda i: (0, 0)),
                  pl.BlockSpec((D_MODEL, IN_PROJ_WIDTH), lambda i: (0, 0))],
        out_specs=[pl.BlockSpec((tm, FN_WIDTH), lambda i: (i, 0)),
                   pl.BlockSpec((tm, SSM_WIDTH), lambda i: (i, 0)),
                   pl.BlockSpec((tm, 3 * ATT_WIDTH), lambda i: (i, 0))],
        out_shape=[jax.ShapeDtypeStruct((n, FN_WIDTH), F32),
                   jax.ShapeDtypeStruct((n, SSM_WIDTH), F32),
                   jax.ShapeDtypeStruct((n, 3 * ATT_WIDTH), BF16)],
        compiler_params=_cparams(("arbitrary",)),
        name="inproj",
    )(h, g, w)


def _fourier_body(xa_ref, xb_ref, cs_ref, ss_ref, tc_ref, ts_ref, cc_ref, sc_ref, wb_ref, g_ref, o_ref, *, s):
    quarter = s // FFT_RADIX
    def phase(q):
        return jnp.concatenate([xa_ref[0, pl.ds(q, quarter, stride=FFT_RADIX), :],
                                xb_ref[0, pl.ds(q, quarter, stride=FFT_RADIX), :]], axis=1)
    x4 = jnp.concatenate([phase(q) for q in range(FFT_RADIX)], axis=1).astype(BF16)
    y = _dot(cs_ref[...], x4)
    z = _dot(ss_ref[...], x4)
    wide = lambda t: jnp.concatenate([t] * (FN_WIDTH // LANES), axis=1)
    tcs, tss = [], []
    for q in range(FFT_RADIX):
        fc = y[:, q * FN_WIDTH:(q + 1) * FN_WIDTH]
        fs = z[:, q * FN_WIDTH:(q + 1) * FN_WIDTH]
        if q == 0:
            tcs.append(fc)
            tss.append(fs)
        else:
            c, sn = wide(tc_ref[q - 1]), wide(ts_ref[q - 1])
            tcs.append(c * fc - sn * fs)
            tss.append(c * fs + sn * fc)
    (c0, c1, c2, c3), (s0, s1, s2, s3) = tcs, tss
    parts = ((c0 + c1 + c2 + c3, s0 + s1 + s2 + s3),
             (c0 - s1 - c2 + s3, s0 + c1 - s2 - c3),
             (c0 - c1 + c2 - c3, s0 - s1 + s2 - s3),
             (c0 + s1 - c2 - s3, s0 - c1 - s2 + c3))
    for part, (yy, zz) in enumerate(parts):
        f = _dot(yy.astype(BF16), cc_ref[...]) - _dot(zz.astype(BF16), sc_ref[...])
        o = _dot(f.astype(BF16), wb_ref[...])
        o_ref[0, part] = _rms(o, g_ref[...]).astype(BF16)


def _dft_tables(n):
    def exact(rows):
        k = (rows[:, None] * jnp.arange(n, dtype=I32)[None, :]) % n
        ang = k.astype(F32) * (2.0 * math.pi / n)
        return jnp.cos(ang), jnp.sin(ang)
    if n <= LANES:
        return exact(jnp.arange(n, dtype=I32))
    ca, sa = exact(LANES * jnp.arange(n // LANES, dtype=I32))
    cb, sb = exact(jnp.arange(LANES, dtype=I32))
    cos = ca[:, None, :] * cb[None] - sa[:, None, :] * sb[None]
    sin = sa[:, None, :] * cb[None] + ca[:, None, :] * sb[None]
    return cos.reshape(n, n), sin.reshape(n, n)


def _fourier(zf, tables, ccb, scb, wb, g, b, s):
    cs, ss, tc, ts = tables
    quarter = s // FFT_RADIX
    tr = min(FOURIER_ROWS, quarter)
    x = zf.reshape(b, s, FN_WIDTH)
    fix = lambda i, j: (0, 0)
    out = pl.pallas_call(
        functools.partial(_fourier_body, s=s),
        grid=(quarter // tr, b),
        in_specs=[pl.BlockSpec((1, s, LANES), lambda i, j: (j, 0, 0)),
                  pl.BlockSpec((1, s, LANES), lambda i, j: (j, 0, 1)),
                  pl.BlockSpec((tr, quarter), lambda i, j: (i, 0)),
                  pl.BlockSpec((tr, quarter), lambda i, j: (i, 0)),
                  pl.BlockSpec((FFT_RADIX - 1, tr, LANES), lambda i, j: (0, i, 0)),
                  pl.BlockSpec((FFT_RADIX - 1, tr, LANES), lambda i, j: (0, i, 0)),
                  pl.BlockSpec((FN_WIDTH, FN_WIDTH), fix),
                  pl.BlockSpec((FN_WIDTH, FN_WIDTH), fix),
                  pl.BlockSpec((FN_WIDTH, FN_WIDTH), fix),
                  pl.BlockSpec((1, FN_WIDTH), fix)],
        out_specs=pl.BlockSpec((1, FFT_RADIX, tr, FN_WIDTH), lambda i, j: (j, 0, i, 0)),
        out_shape=jax.ShapeDtypeStruct((b, FFT_RADIX, quarter, FN_WIDTH), BF16),
        compiler_params=_cparams(("arbitrary", "arbitrary")),
        name="fourier",
    )(x, x, cs, ss, tc, ts, ccb, scb, wb, g)
    return out.reshape(b * s, FN_WIDTH)


def _fourier_tables(s):
    quarter = s // FFT_RADIX
    cs, ss = _dft_tables(quarter)
    kq = jnp.arange(1, FFT_RADIX, dtype=F32)[:, None] * jnp.arange(quarter, dtype=F32)[None, :]
    ang = kq * (2.0 * math.pi / s)
    wide = lambda v: jnp.broadcast_to(v[:, :, None], (FFT_RADIX - 1, quarter, LANES))
    return cs.astype(BF16), ss.astype(BF16), wide(jnp.cos(ang)), wide(jnp.sin(ang))


def _block_diag(blocks):
    h, a, bb = blocks.shape
    eye = jnp.eye(h, dtype=blocks.dtype)
    return jnp.einsum('hab,hg->hagb', blocks, eye).reshape(h * a, h * bb)


def _s5_matrices(a_re, a_im, log_dt, b_re, b_im, c_re, c_im):
    t = SSM_CHUNK
    g, p, c = SSM_GROUPS, SSM_STATE, SSM_GROUP
    lam = lax.complex(a_re.astype(F32), a_im.astype(F32))
    dt = jnp.exp(log_dt.astype(F32))[..., None]
    abar = jnp.exp(lam * dt)
    bbar = ((abar - 1.0) / lam)[..., None] * lax.complex(b_re.astype(F32), b_im.astype(F32))
    cmat = lax.complex(c_re.astype(F32), c_im.astype(F32))
    ks = jnp.arange(t + 1, dtype=F32)
    apow = jnp.exp((lam * dt)[:, None] * ks[None, :, None, None])

    kern = jnp.real(jnp.einsum('dgcp,dkgp,dgpe->dkgce', cmat, apow[:, :t], bbar))
    gh, hw, hs = g // 2, SSM_WIDTH // 2, SSM_NSTATE // 2
    lags = jnp.arange(-(t - 1), t)
    pick = lambda m: m[:, None, None, None]
    klag = (jnp.where(pick(lags >= 0), kern[0][jnp.clip(lags, 0, t - 1)], 0.0)
            + jnp.where(pick(lags <= 0), kern[1][jnp.clip(-lags, 0, t - 1)], 0.0))
    eye_h = jnp.eye(gh, dtype=F32)
    lag = jnp.arange(t)[None, :] - jnp.arange(t)[:, None]
    def _intra(h):
        blocks = jnp.einsum('lgce,gh->lgehc', klag[:, h * gh:(h + 1) * gh], eye_h).reshape(2 * t - 1, hw, hw)
        return blocks[lag + t - 1].transpose(0, 2, 1, 3).reshape(t * hw, t * hw)
    m_intra = jnp.stack([_intra(0), _intra(1)])

    wf = apow[0, t - 1 - jnp.arange(t)][:, :, :, None] * bbar[0][None]
    wb = apow[1, jnp.arange(t)][:, :, :, None] * bbar[1][None]
    same_group = ((jnp.arange(t * hw)[:, None] // c) % gh) == (jnp.arange(hs)[None, :] // p)
    def _spread(x, h):
        xh = x[:, h * gh:(h + 1) * gh].reshape(t * hw, p)
        return jnp.where(same_group, jnp.tile(xh, (1, gh)), 0.0)
    def _parts(parts, h):
        return jnp.concatenate([sign * _spread(x, h) for sign, x in parts], axis=1)
    wf, wb = jnp.swapaxes(wf, 2, 3), jnp.swapaxes(wb, 2, 3)
    in_parts = [(1.0, jnp.real(wf)), (1.0, jnp.imag(wf)), (1.0, jnp.real(wb)), (1.0, jnp.imag(wb))]
    w_in = jnp.stack([_parts(in_parts, 0), _parts(in_parts, 1)])

    qf = cmat[0][None] * apow[0, 1 + jnp.arange(t)][:, :, None, :]
    qb = cmat[1][None] * apow[1, t - jnp.arange(t)][:, :, None, :]
    out_parts = [(1.0, jnp.real(qf)), (-1.0, jnp.imag(qf)), (1.0, jnp.real(qb)), (-1.0, jnp.imag(qb))]
    w_out = jnp.stack([_parts(out_parts, 0).T, _parts(out_parts, 1).T])

    a_chunk = lam * dt * t
    return m_intra.astype(BF16), w_in.astype(BF16), w_out.astype(BF16), a_chunk


def _scan_powers(a_chunk, nsteps):
    e = jnp.exp(a_chunk[None] * (2.0 ** jnp.arange(nsteps, dtype=F32))[:, None, None, None])
    e = e.reshape(nsteps, 2 * SSM_NSTATE)
    return jnp.real(e), jnp.imag(e)


def _half_steps(z_ref, tr):
    return jnp.concatenate([z_ref[pl.ds(r, tr, stride=SSM_CHUNK), :] for r in range(SSM_CHUNK)], axis=1)


def _halves(rows):
    return [pl.BlockSpec((rows, LANES), lambda i: (i, 0)), pl.BlockSpec((rows, LANES), lambda i: (i, 1))]


def _s5_in_body(za_ref, zb_ref, w_ref, o_ref):
    tr = o_ref.shape[0]
    hs = SSM_NSTATE // 2
    for h, z_ref in enumerate((za_ref, zb_ref)):
        res = _dot(_half_steps(z_ref, tr).astype(BF16), w_ref[h])
        for part in range(4):
            lo = part * SSM_NSTATE + h * hs
            o_ref[:, lo:lo + hs] = res[:, part * hs:(part + 1) * hs]


def _s5_in(zs, w_in):
    rows = zs.shape[0] // SSM_CHUNK
    tr = min(S5_ROWS, rows)
    width = 4 * SSM_NSTATE
    return pl.pallas_call(
        _s5_in_body,
        grid=(rows // tr,),
        in_specs=_halves(tr * SSM_CHUNK) + [
            pl.BlockSpec((2, SSM_ROW // 2, width // 2), lambda i: (0, 0, 0), pipeline_mode=pl.Buffered(1))],
        out_specs=pl.BlockSpec((tr, width), lambda i: (i, 0)),
        out_shape=jax.ShapeDtypeStruct((rows, width), F32),
        compiler_params=_cparams(("arbitrary",)),
        name="s5_in",
    )(zs, zs, w_in)


def _s5_scan_body(x_ref, pr_ref, pi_ref, o_ref, *, nc, nsteps):
    ns = SSM_NSTATE
    row = lax.broadcasted_iota(I32, (nc, 1), 0)

    def direction(part, table_lo, forward):
        def shifted(v, sh):
            if forward:
                return jnp.where(row >= sh, pltpu.roll(v, sh, 0), 0.0)
            return jnp.where(row < nc - sh, pltpu.roll(v, nc - sh, 0), 0.0)

        def column(c, carry):
            off = pl.multiple_of(c * LANES, LANES)
            re_at = pl.ds(part * ns + off, LANES)
            im_at = pl.ds((part + 1) * ns + off, LANES)
            tab = pl.ds(table_lo + off, LANES)
            re, im = x_ref[0, :, re_at], x_ref[0, :, im_at]
            for k in range(nsteps):
                ar, ai = pr_ref[k:k + 1, tab], pi_ref[k:k + 1, tab]
                sre, sim = shifted(re, 2 ** k), shifted(im, 2 ** k)
                re, im = re + ar * sre - ai * sim, im + ar * sim + ai * sre
            o_ref[0, :, re_at] = shifted(re, 1).astype(BF16)
            o_ref[0, :, im_at] = shifted(im, 1).astype(BF16)
            return carry

        lax.fori_loop(0, ns // LANES, column, 0)

    direction(0, 0, True)
    direction(2, ns, False)


def _s5_scan(xl, pr, pi, b, nc):
    nsteps = int(math.log2(nc))
    width = 4 * SSM_NSTATE
    x = xl.reshape(b, nc, width)
    out = pl.pallas_call(
        functools.partial(_s5_scan_body, nc=nc, nsteps=nsteps),
        grid=(b,),
        in_specs=[pl.BlockSpec((1, nc, width), lambda i: (i, 0, 0)),
                  pl.BlockSpec((nsteps, 2 * SSM_NSTATE), lambda i: (0, 0)),
                  pl.BlockSpec((nsteps, 2 * SSM_NSTATE), lambda i: (0, 0))],
        out_specs=pl.BlockSpec((1, nc, width), lambda i: (i, 0, 0)),
        out_shape=jax.ShapeDtypeStruct((b, nc, width), BF16),
        compiler_params=_cparams(("arbitrary",)),
        name="s5_scan",
    )(x, pr, pi)
    return out.reshape(b * nc, width)


def _s5_out_body(za_ref, zb_ref, s_ref, m_ref, w_ref, d_ref, gw_ref, gb_ref, g_ref, o_ref, nat_a, nat_b):
    tr = s_ref.shape[0]
    hs = SSM_NSTATE // 2
    us, ys = [], []
    for h, z_ref in enumerate((za_ref, zb_ref)):
        u = _half_steps(z_ref, tr)
        st = jnp.concatenate([s_ref[:, part * SSM_NSTATE + h * hs:part * SSM_NSTATE + (h + 1) * hs]
                              for part in range(4)], axis=1)
        us.append(u)
        ys.append(_dot(st, w_ref[h]) + _dot(u.astype(BF16), m_ref[h]))
    c0 = math.sqrt(2.0 / math.pi)
    for r in range(SSM_CHUNK):
        step = slice(r * LANES, (r + 1) * LANES)
        y = jnp.concatenate([ys[0][:, step], ys[1][:, step]], axis=1)
        u = jnp.concatenate([us[0][:, step], us[1][:, step]], axis=1)
        v = y + d_ref[...] * u
        gl = 0.5 * v * (1.0 + jnp.tanh(c0 * (v + 0.044715 * (v * v * v))))
        gate = jax.nn.sigmoid(_dot(gl.astype(BF16), gw_ref[...]) + gb_ref[...])
        out = _rms(gl * gate, g_ref[...])
        nat_a[pl.ds(r, tr, stride=SSM_CHUNK), :] = out[:, :LANES]
        nat_b[pl.ds(r, tr, stride=SSM_CHUNK), :] = out[:, LANES:]
    o_ref[:, :LANES] = nat_a[...].astype(BF16)
    o_ref[:, LANES:] = nat_b[...].astype(BF16)


def _s5_out(zs, states, m_intra, w_out, d, glu_w, glu_b, g):
    rows = states.shape[0]
    tr = min(S5_ROWS, rows)
    width = 4 * SSM_NSTATE
    fix = lambda i: (0, 0)
    fix3 = lambda i: (0, 0, 0)
    once = pl.Buffered(1)
    return pl.pallas_call(
        _s5_out_body,
        grid=(rows // tr,),
        in_specs=_halves(tr * SSM_CHUNK) + [
                  pl.BlockSpec((tr, width), lambda i: (i, 0)),
                  pl.BlockSpec((2, SSM_ROW // 2, SSM_ROW // 2), fix3, pipeline_mode=once),
                  pl.BlockSpec((2, width // 2, SSM_ROW // 2), fix3, pipeline_mode=once),
                  pl.BlockSpec((1, SSM_WIDTH), fix), pl.BlockSpec((SSM_WIDTH, SSM_WIDTH), fix),
                  pl.BlockSpec((1, SSM_WIDTH), fix), pl.BlockSpec((1, SSM_WIDTH), fix)],
        out_specs=pl.BlockSpec((tr * SSM_CHUNK, SSM_WIDTH), lambda i: (i, 0)),
        out_shape=jax.ShapeDtypeStruct((rows * SSM_CHUNK, SSM_WIDTH), BF16),
        scratch_shapes=[pltpu.VMEM((tr * SSM_CHUNK, LANES), F32) for _ in range(SSM_WIDTH // LANES)],
        compiler_params=_cparams(("arbitrary",)),
        name="s5_out",
    )(zs, zs, states, m_intra, w_out, d, glu_w, glu_b, g)


def _attn_geometry(s, d):
    ln = s // d
    bq = min(128, ln)
    bk = min(bq + 2 * ATT_HALF, ln)
    return ln, bq, bk, ln // bq


def _attn_body(slope_ref, q_ref, k_ref, v_ref, o_ref, nat, qd, kd, vd, qdb, kdb, vdb, b0, b1, b2, *acc, s):
    hp = pl.program_id(1)
    s4 = s // ATT_SPLIT
    lane = lax.broadcasted_iota(I32, (1, LANES), 1)
    first = lane < ATT_HEAD_DIM
    second = jnp.logical_not(first)
    slopes = (slope_ref[2 * hp], slope_ref[2 * hp + 1])
    log2e = math.log2(math.e)
    scale = ATT_HEAD_DIM ** -0.5 * log2e

    @pl.when(pl.program_id(0) == 0)
    def _():
        for (_, d), bias in zip(DILATED_PATTERNS, (b0, b1, b2)):
            _, bq, bk, _ = _attn_geometry(s, d)
            jk = lax.broadcasted_iota(I32, (1, bk), 1)
            for o in range(ATT_OFFSETS):
                rel = jnp.abs(lax.broadcasted_iota(I32, (bq, 1), 0) + o * ATT_HALF - jk)
                dist = (d * rel).astype(F32) * log2e
                for hh in range(2):
                    bias[hp, 2 * o + hh] = jnp.where(rel <= ATT_HALF, -slopes[hh] * dist, NEG_INF)

    for src, dst_f, dst_b in ((q_ref, qd, qdb), (k_ref, kd, kdb), (v_ref, vd, vdb)):
        nat[...] = src[...].astype(F32)
        for c in range(ATT_SPLIT):
            x = nat[pl.ds(c, s4, stride=ATT_SPLIT), :]
            dst_f[c * s4:(c + 1) * s4, :] = x
            dst_b[c * s4:(c + 1) * s4, :] = x.astype(BF16)

    def pattern(p, d, refs, bias, locate, stride):
        acc_o, acc_m, acc_l = acc[3 * p], acc[3 * p + 1], acc[3 * p + 2]
        ln, bq, bk, nqb = _attn_geometry(s, d)

        def rows(c, j, size):
            start = locate(c, j)
            if stride == 1:
                return pl.ds(pl.multiple_of(start, ATT_HALF), size)
            return pl.ds(start, size, stride=stride)

        def scores(i):
            c = i // nqb
            j0 = (i % nqb) * bq
            ks = jnp.clip(j0 - ATT_HALF, 0, ln - bk)
            q = refs[0][rows(c, j0, bq), :].astype(BF16)
            k = refs[1][rows(c, ks, bk), :].astype(BF16)
            v = refs[2][rows(c, ks, bk), :].astype(BF16)
            off = (j0 - ks) // ATT_HALF
            scs = []
            for hh in range(2):
                qm = jnp.where(first if hh == 0 else second, q, jnp.zeros_like(q))
                scs.append(_dot_nt(qm, k) * scale + bias[hp, 2 * off + hh])
            return rows(c, j0, bq), v, scs

        def softmax(sc):
            m = jnp.max(sc, axis=-1, keepdims=True)
            pe = jnp.exp2(sc - m)
            return pe.astype(BF16), m, jnp.sum(pe, axis=-1, keepdims=True)

        def group(g, carry):
            staged = [scores(g * ATT_UNROLL + u) for u in range(ATT_UNROLL)]
            soft = [[softmax(sc) for sc in scs] for _, _, scs in staged]
            for (dst, v, _), ((p0, m0, l0), (p1, m1, l1)) in zip(staged, soft):
                acc_o[dst, :] = jnp.where(first, _dot(p0, v), _dot(p1, v))
                acc_m[dst, :] = jnp.where(first, m0, m1)
                acc_l[dst, :] = jnp.where(first, l0, l1)
            return carry

        lax.fori_loop(0, d * nqb // ATT_UNROLL, group, 0)

    (_, d1), (_, d2), (_, d3) = DILATED_PATTERNS
    assert d1 == 1 and d2 == ATT_SPLIT and d3 == ATT_SPLIT * ATT_SPLIT
    pattern(0, d1, (q_ref, k_ref, v_ref), b0, lambda c, j: j, 1)
    pattern(1, d2, (qdb, kdb, vdb), b1, lambda c, j: c * s4 + j, 1)
    pattern(2, d3, (qd, kd, vd), b2,
            lambda c, j: (c % ATT_SPLIT) * s4 + c // ATT_SPLIT + ATT_SPLIT * j, ATT_SPLIT)

    for c in range(ATT_SPLIT):
        part = pl.ds(c, s4, stride=ATT_SPLIT)
        blk = slice(c * s4, (c + 1) * s4)
        ms = (acc[1][part, :], acc[4][blk, :], acc[7][blk, :])
        os_ = (acc[0][part, :], acc[3][blk, :], acc[6][blk, :])
        ls = (acc[2][part, :], acc[5][blk, :], acc[8][blk, :])
        m = jnp.maximum(jnp.maximum(ms[0], ms[1]), ms[2])
        num = jnp.zeros((s4, LANES), F32)
        den = jnp.zeros((s4, LANES), F32)
        for p in range(3):
            w = jnp.exp2(ms[p] - m)
            num = num + w * os_[p]
            den = den + w * ls[p]
        nat[part, :] = num / den
    o_ref[...] = nat[...].astype(BF16)


def _attn(zqkv, slopes, b, s):
    n = b * s
    nhp = ATT_HEADS // 2
    col = lambda off: (lambda i, j, sl: (i, off + j))
    return pl.pallas_call(
        functools.partial(_attn_body, s=s),
        grid_spec=pltpu.PrefetchScalarGridSpec(
            num_scalar_prefetch=1,
            grid=(b, nhp),
            in_specs=[pl.BlockSpec((s, LANES), col(0)),
                      pl.BlockSpec((s, LANES), col(nhp)),
                      pl.BlockSpec((s, LANES), col(2 * nhp))],
            out_specs=pl.BlockSpec((s, LANES), lambda i, j, sl: (i, j)),
            scratch_shapes=([pltpu.VMEM((s, LANES), F32) for _ in range(4)]
                            + [pltpu.VMEM((s, LANES), BF16) for _ in range(3)]
                            + [pltpu.VMEM((nhp, 2 * ATT_OFFSETS) + _attn_geometry(s, d)[1:3], F32)
                               for _, d in DILATED_PATTERNS]
                            + [pltpu.VMEM((s, LANES), F32) for _ in range(9)]),
        ),
        out_shape=jax.ShapeDtypeStruct((n, ATT_WIDTH), BF16),
        compiler_params=_cparams(("arbitrary", "arbitrary")),
        name="attn",
    )(slopes, zqkv, zqkv, zqkv)


def _outproj_body(h_ref, mf_ref, ms_ref, oa_ref, ga_ref, w_ref, gf_ref, wr_ref,
                  h1_ref, m_ref, aff_ref):
    oa = _rms(oa_ref[...].astype(F32), ga_ref[...]).astype(BF16)
    o1, o2 = FN_WIDTH, FN_WIDTH + SSM_WIDTH
    acc = h_ref[...] + _dot(mf_ref[...], w_ref[0:o1, :]) + _dot(ms_ref[...], w_ref[o1:o2, :])
    acc = acc + _dot(oa, w_ref[o2:, :])
    h1_ref[...] = acc
    m = _rms(acc, gf_ref[...]).astype(BF16)
    bits = pltpu.bitcast(m.astype(F32), I32)
    m_ref[...] = lax.shift_right_logical(bits[:, :PACKED], 16) | (bits[:, PACKED:] & jnp.int32(-65536))
    lg = _dot(m, wr_ref[...])
    e = jnp.exp(lg - jnp.max(lg, axis=1, keepdims=True))
    aff_ref[...] = e / jnp.sum(e, axis=1, keepdims=True)


def _outproj(h, mf, ms, oa, ga, w, gf, wr):
    n = h.shape[0]
    tm = PROJ_ROWS
    row = lambda i: (i, 0)
    fix = lambda i: (0, 0)
    return pl.pallas_call(
        _outproj_body,
        grid=(n // tm,),
        in_specs=[pl.BlockSpec((tm, D_MODEL), row), pl.BlockSpec((tm, FN_WIDTH), row),
                  pl.BlockSpec((tm, SSM_WIDTH), row), pl.BlockSpec((tm, ATT_WIDTH), row),
                  pl.BlockSpec((1, ATT_WIDTH), fix), pl.BlockSpec((D_MODEL, D_MODEL), fix),
                  pl.BlockSpec((1, D_MODEL), fix), pl.BlockSpec((D_MODEL, N_EXPERTS), fix)],
        out_specs=[pl.BlockSpec((tm, D_MODEL), row), pl.BlockSpec((tm, PACKED), row),
                   pl.BlockSpec((tm, N_EXPERTS), row)],
        out_shape=[jax.ShapeDtypeStruct((n, D_MODEL), F32),
                   jax.ShapeDtypeStruct((n, PACKED), I32),
                   jax.ShapeDtypeStruct((n, N_EXPERTS), F32)],
        compiler_params=_cparams(("arbitrary",)),
        name="outproj",
    )(h, mf, ms, oa, ga, w, gf, wr)


def _select_body(aff_ref, rank_ref, off_ref, thr_ref, *, cap, nb):
    r128 = lax.broadcasted_iota(I32, (LANES, LANES), 0)
    c128 = lax.broadcasted_iota(I32, (LANES, LANES), 1)
    upper_incl = jnp.where(r128 <= c128, 1.0, 0.0).astype(BF16)
    ones = jnp.ones((LANES, LANES), BF16)
    rb = lax.broadcasted_iota(I32, (nb, nb), 0)
    cb = lax.broadcasted_iota(I32, (nb, nb), 1)
    lower_strict = jnp.where(cb < rb, 1.0, 0.0).astype(BF16)
    upper_strict = jnp.where(rb < cb, 1.0, 0.0).astype(BF16)
    ones8 = jnp.ones((8, LANES), BF16)

    def count(mask):
        c = jnp.sum(jnp.where(mask, 1.0, 0.0), axis=0, keepdims=True)
        return jnp.sum(c, axis=1, keepdims=True)

    bits_all = pltpu.bitcast(aff_ref[...], I32)

    def bitstep(i, t):
        cand = t | jnp.left_shift(jnp.int32(1), 30 - i)
        above = jnp.sum(jnp.where(bits_all >= cand, 1.0, 0.0), axis=1, keepdims=True)
        return jnp.where(jnp.sum(above, axis=2, keepdims=True) >= cap, cand, t)

    thr = lax.fori_loop(0, 31, bitstep, jnp.zeros((N_EXPERTS, 1, 1), I32))
    thr_ref[...] = jnp.broadcast_to(thr, thr_ref.shape)

    def prefix(mask):
        mb = jnp.where(mask, 1.0, 0.0).astype(BF16)
        incl = _dot(mb, upper_incl)
        tot = _dot(mb, ones)
        offs = _dot(lower_strict, tot.astype(BF16))
        return offs + incl - 1.0, mb

    def per_expert(e, carry):
        bits = pltpu.bitcast(aff_ref[e], I32)
        t = thr_ref[e][0:1, 0:1]
        gt = bits > t
        eq = bits == t
        need = cap - count(gt)
        eq_rank, _ = prefix(eq)
        sel = jnp.logical_or(gt, jnp.logical_and(eq, eq_rank < need))
        rank, mb = prefix(sel)
        rank_ref[e] = jnp.where(sel, rank.astype(I32), -1)
        tot_row = _dot_nt(ones8, mb)
        off_row = _dot(tot_row.astype(BF16), upper_strict)
        off_ref[pl.ds(e, 1), :] = off_row[0:1, :].astype(I32)
        return carry

    lax.fori_loop(0, N_EXPERTS, per_expert, 0)


def _select(aff_t, cap):
    n = aff_t.shape[1]
    nb = n // LANES
    a3 = aff_t.reshape(N_EXPERTS, nb, LANES)
    rank, off = pl.pallas_call(
        functools.partial(_select_body, cap=cap, nb=nb),
        grid=(1,),
        in_specs=[pl.BlockSpec((N_EXPERTS, nb, LANES), lambda i: (0, 0, 0))],
        out_specs=[pl.BlockSpec((N_EXPERTS, nb, LANES), lambda i: (0, 0, 0)),
                   pl.BlockSpec((N_EXPERTS, nb), lambda i: (0, 0))],
        out_shape=[jax.ShapeDtypeStruct((N_EXPERTS, nb, LANES), I32),
                   jax.ShapeDtypeStruct((N_EXPERTS, nb), I32)],
        scratch_shapes=[pltpu.VMEM((N_EXPERTS, SUBLANES, LANES), I32)],
        compiler_params=_cparams(("arbitrary",)),
        name="select",
    )(a3)
    return rank.reshape(N_EXPERTS, n), off


def _ffn_body(x_ref, wg_ref, wu_ref, wd_ref, o_ref):
    w = x_ref[...]
    x = jnp.concatenate([pltpu.bitcast(lax.shift_left(w, 16), F32),
                         pltpu.bitcast(w & jnp.int32(-65536), F32)], axis=1).astype(BF16)
    tf = FFN_COLS
    acc = jnp.zeros(o_ref.shape, F32)
    for j in range(D_FF_EXPERT // tf):
        g = _dot(x, wg_ref[:, j * tf:(j + 1) * tf])
        u = _dot(x, wu_ref[:, j * tf:(j + 1) * tf])
        hdn = (g * jax.nn.sigmoid(g) * u).astype(BF16)
        acc = acc + _dot(hdn, wd_ref[j * tf:(j + 1) * tf, :])
    o_ref[...] = acc.astype(BF16)


def _ffn(xe, wg, wu, wd, layer):
    e, cap, _ = xe.shape
    tm = min(FFN_ROWS, cap)
    return pl.pallas_call(
        _ffn_body,
        grid=(e, cap // tm),
        in_specs=[pl.BlockSpec((None, tm, PACKED), lambda i, j: (i, j, 0)),
                  pl.BlockSpec((None, None, D_MODEL, D_FF_EXPERT), lambda i, j: (layer, i, 0, 0)),
                  pl.BlockSpec((None, None, D_MODEL, D_FF_EXPERT), lambda i, j: (layer, i, 0, 0)),
                  pl.BlockSpec((None, None, D_FF_EXPERT, D_MODEL), lambda i, j: (layer, i, 0, 0))],
        out_specs=pl.BlockSpec((None, tm, D_MODEL), lambda i, j: (i, j, 0)),
        out_shape=jax.ShapeDtypeStruct((e, cap, D_MODEL), BF16),
        compiler_params=_cparams(("arbitrary", "arbitrary")),
        name="ffn",
    )(xe, wg, wu, wd)


def _combine_body(off_ref, h_ref, rank_ref, aff_ref, p_ref, gp_ref, wg_ref, wp_ref, gfin_ref, ye_ref,
                  o_ref, stack, sems, *, cap, last):
    t = pl.program_id(0)
    nt = pl.num_programs(0)
    tile, win, nsub = COMBINE_TILE, COMBINE_WIN, COMBINE_SUB
    bpt = tile // LANES
    kdim = N_EXPERTS * win

    def base(tt, e):
        return (off_ref[e, tt * bpt] // BF16_ROWS) * BF16_ROWS

    def copy(e, start, slot, sub):
        return pltpu.make_async_copy(ye_ref.at[e, pl.ds(pl.multiple_of(start, BF16_ROWS), win), :],
                                     stack.at[slot, sub, pl.ds(e * win, win), :], sems.at[slot, sub, e])

    def starts(tt, r):
        want = [base(tt, e) + r * win for e in range(N_EXPERTS)]
        return want, [jnp.minimum(w, cap - win) for w in want]

    def issue(tt, r, slot, sub):
        _, got = starts(tt, r)
        for e in range(N_EXPERTS):
            copy(e, got[e], slot, sub).start()

    def wait(tt, r, slot, sub):
        _, got = starts(tt, r)
        for e in range(N_EXPERTS):
            copy(e, got[e], slot, sub).wait()

    slot = t % 2

    @pl.when(t == 0)
    def _():
        for sub in range(nsub):
            issue(sub, 0, 0, sub)

    @pl.when(t + 1 < nt)
    def _():
        for sub in range(nsub):
            issue((t + 1) * nsub + sub, 0, 1 - slot, sub)

    lane16 = lax.broadcasted_iota(I32, (1, N_EXPERTS), 1)
    er = lax.broadcasted_iota(I32, (N_EXPERTS, kdim), 0)
    ec = lax.broadcasted_iota(I32, (N_EXPERTS, kdim), 1)
    expand = jnp.where(ec // win == er, 1.0, 0.0).astype(BF16)
    lane_in = (lax.broadcasted_iota(I32, (1, kdim), 1) % win).astype(F32)

    def row_of(vals):
        r = jnp.zeros((1, N_EXPERTS), I32)
        for e in range(N_EXPERTS):
            r = jnp.where(lane16 == e, vals[e], r)
        return r

    def accumulate(sub, r):
        rows = pl.ds(sub * tile, tile)
        want, got = starts(t * nsub + sub, r)
        wrow, grow = row_of(want), row_of(got)
        rk = rank_ref[rows, :]
        ok = jnp.logical_and(rk >= wrow, rk < grow + win)
        rel = jnp.where(ok, rk - grow, -1).astype(F32).astype(BF16)
        hit = _dot(rel, expand) == lane_in
        gates = _dot(aff_ref[rows, :].astype(BF16), expand)
        w = jnp.where(hit, gates, 0.0).astype(BF16)
        return _dot(w, stack[slot, sub])

    for sub in range(nsub):
        wait(t * nsub + sub, 0, slot, sub)
    for sub in range(nsub):
        rows = pl.ds(sub * tile, tile)
        o_ref[rows, :] = h_ref[rows, :] + accumulate(sub, 0)

    for sub in range(nsub):
        tt = t * nsub + sub
        nr = jnp.int32(1)
        for e in range(N_EXPERTS):
            span = off_ref[e, (tt + 1) * bpt] - base(tt, e)
            nr = jnp.maximum(nr, (span + win - 1) // win)

        def extra(r, carry, sub=sub, tt=tt):
            issue(tt, r, slot, sub)
            wait(tt, r, slot, sub)
            o_ref[pl.ds(sub * tile, tile), :] += accumulate(sub, r)
            return carry

        lax.fori_loop(1, nr, extra, 0)

    for sub in range(nsub):
        rows = pl.ds(sub * tile, tile)
        h2 = o_ref[rows, :]
        gate = jax.nn.sigmoid(_dot(_rms(h2, gp_ref[...]).astype(BF16), wg_ref[...]))
        h3 = h2 + _dot(p_ref[rows, :].astype(BF16), wp_ref[...]) * gate
        o_ref[rows, :] = _rms(h3, gfin_ref[...]) if last else h3


def _combine(off, h1, rank_tok, aff_tok, p, layer, gp, wg, wp, gfin, ye, cap, last):
    n = h1.shape[0]
    tile = COMBINE_TILE * COMBINE_SUB
    row = lambda i, o: (i, 0)
    fix = lambda i, o: (0, 0)
    prow = lambda i, o: (layer * (n // tile) + i, 0)
    return pl.pallas_call(
        functools.partial(_combine_body, cap=cap, last=last),
        grid_spec=pltpu.PrefetchScalarGridSpec(
            num_scalar_prefetch=1,
            grid=(n // tile,),
            in_specs=[pl.BlockSpec((tile, D_MODEL), row),
                      pl.BlockSpec((tile, N_EXPERTS), row),
                      pl.BlockSpec((tile, N_EXPERTS), row),
                      pl.BlockSpec((tile, PLE_DIM), prow),
                      pl.BlockSpec((1, D_MODEL), fix),
                      pl.BlockSpec((D_MODEL, D_MODEL), fix),
                      pl.BlockSpec((PLE_DIM, D_MODEL), fix),
                      pl.BlockSpec((1, D_MODEL), fix),
                      pl.BlockSpec(memory_space=pl.ANY)],
            out_specs=pl.BlockSpec((tile, D_MODEL), row),
            scratch_shapes=[pltpu.VMEM((2, COMBINE_SUB, N_EXPERTS * COMBINE_WIN, D_MODEL), BF16),
                            pltpu.SemaphoreType.DMA((2, COMBINE_SUB, N_EXPERTS))],
        ),
        out_shape=jax.ShapeDtypeStruct((n, D_MODEL), F32),
        compiler_params=_cparams(("arbitrary",)),
        name="combine",
    )(off, h1, rank_tok, aff_tok, p, gp, wg, wp, gfin, ye)


def _gather_rows(table, rank, cap):
    n, w = table.shape
    workers_per_expert = SC_CORES * SC_SUBCORES // N_EXPERTS
    per = cap // workers_per_expert
    nch = per // GATHER_ROWS
    mesh = plsc.VectorSubcoreMesh(core_axis_name="c", subcore_axis_name="s",
                                  num_cores=SC_CORES, num_subcores=SC_SUBCORES)
    cp = dataclasses.replace(pltpu.CompilerParams(), needs_layout_passes=False)

    @functools.partial(
        pl.kernel, mesh=mesh, compiler_params=cp,
        out_type=jax.ShapeDtypeStruct((N_EXPERTS * cap, w), I32),
        scratch_types=[pltpu.VMEM((RANK_CHUNK,), I32),
                       pltpu.VMEM((nch, GATHER_ROWS), I32),
                       pltpu.VMEM((GATHER_ROWS, w), I32),
                       pltpu.SemaphoreType.DMA],
        name="sc_gather")
    def gather(table_hbm, rank_hbm, out_hbm, rbuf, idx, rows, sem):
        wid = lax.axis_index("s") * SC_CORES + lax.axis_index("c")
        e = wid // workers_per_expert
        lo = (wid % workers_per_expert) * per
        lane = lax.iota(I32, SC_LANES)

        @pl.loop(0, n // RANK_CHUNK)
        def _(c):
            pltpu.sync_copy(rank_hbm.at[e, pl.ds(c * RANK_CHUNK, RANK_CHUNK)], rbuf)

            @pl.loop(0, RANK_CHUNK // SC_LANES)
            def _(i):
                rel = rbuf[pl.ds(i * SC_LANES, SC_LANES)] - lo
                mask = jnp.logical_and(rel >= 0, rel < per)
                rel = jnp.where(mask, rel, 0)
                tok = c * RANK_CHUNK + i * SC_LANES + lane
                plsc.store_scatter(idx, [rel // GATHER_ROWS, rel % GATHER_ROWS], tok, mask=mask)

        @pl.loop(0, nch)
        def _(c):
            pltpu.async_copy(table_hbm.at[idx.at[c]], rows, sem).wait()
            pltpu.sync_copy(rows, out_hbm.at[pl.ds(e * cap + lo + c * GATHER_ROWS, GATHER_ROWS)])

    return gather(table, rank).reshape(N_EXPERTS, cap, w)


def _trunk(x, p, prm, b, s):
    n = b * s
    nc = s // SSM_CHUNK
    cap = EC_CAPACITY_FACTOR * n // N_EXPERTS
    tables = _fourier_tables(s)
    h = x.reshape(n, D_MODEL)
    depth = prm['w_in'].shape[0]
    p_rows = p.reshape(depth * n, PLE_DIM)
    for l in range(depth):
        lp = prm['layers'][l]
        zf, zs, zqkv = _inproj(h, lp['g_mix'], lp['w_in'])
        mf = _fourier(zf, tables, prm['ccb'], prm['scb'], lp['w_fnet'], lp['g_f'], b, s)
        xl = _s5_in(zs, lp['s5_w_in'])
        pr, pi = _scan_powers(lp['s5_a_chunk'], int(math.log2(nc)))
        st = _s5_scan(xl, pr, pi, b, nc)
        ms = _s5_out(zs, st, lp['s5_m_intra'], lp['s5_w_out'], lp['s5_d'], lp['s5_glu_w'], lp['s5_glu_b'],
                     lp['g_s'])
        oa = _attn(zqkv, prm['slopes'], b, s)
        h1, m, aff = _outproj(h, mf, ms, oa, lp['g_a'], lp['w_out'], lp['g_ffn'], lp['w_router'])
        rank, off = _select(aff.T, cap)
        off = jnp.concatenate([off, jnp.full((N_EXPERTS, 1), cap, I32)], axis=1)
        xe = _gather_rows(m, rank, cap)
        ye = _ffn(xe, prm['w_gate'], prm['w_up'], prm['w_down'], l)
        h = _combine(off, h1, rank.T, aff, p_rows, l, lp['g_ple'], lp['w_ple_gate'],
                     lp['w_ple_proj'], prm['g_final'], ye, cap, l == depth - 1)
    return h.reshape(b, s, D_MODEL)


def kernel(x_prompt, x_sample, p_prompt, p_sample, norm_mix, w_in, w_fnet, ssm_a_re, ssm_a_im, ssm_log_dt, ssm_b_re, ssm_b_im, ssm_c_re, ssm_c_im, ssm_d, ssm_glu_w, ssm_glu_b, norm_branch, w_out, norm_ffn, w_router, w_exp_gate, w_exp_up, w_exp_down, norm_ple, w_ple_gate, w_ple_proj, norm_final):
    depth = w_in.shape[0]
    o1, o2 = FN_WIDTH, FN_WIDTH + SSM_WIDTH
    row = lambda v: v.reshape(1, -1).astype(F32)
    cc, sc = _dft_tables(FN_HEAD_DIM)
    tile4 = lambda mtx: _block_diag(jnp.broadcast_to(mtx[None], (FN_HEADS,) + mtx.shape)).astype(BF16)
    layers = []
    for l in range(depth):
        m_intra, s5_w_in, s5_w_out, a_chunk = _s5_matrices(
            ssm_a_re[l], ssm_a_im[l], ssm_log_dt[l], ssm_b_re[l], ssm_b_im[l], ssm_c_re[l], ssm_c_im[l])
        layers.append(dict(
            g_mix=row(norm_mix[l]), w_in=w_in[l].astype(BF16),
            w_fnet=_block_diag(w_fnet[l]).astype(BF16),
            g_f=row(norm_branch[l][:o1]), g_s=row(norm_branch[l][o1:o2]), g_a=row(norm_branch[l][o2:]),
            s5_m_intra=m_intra, s5_w_in=s5_w_in, s5_w_out=s5_w_out, s5_a_chunk=a_chunk,
            s5_d=row(ssm_d[l]), s5_glu_w=ssm_glu_w[l].astype(BF16), s5_glu_b=row(ssm_glu_b[l]),
            w_out=w_out[l].astype(BF16), g_ffn=row(norm_ffn[l]),
            w_router=w_router[l].astype(BF16),
            g_ple=row(norm_ple[l]), w_ple_gate=w_ple_gate[l].astype(BF16),
            w_ple_proj=w_ple_proj[l].astype(BF16)))
    slopes = jnp.asarray([2.0 ** (-8.0 * (i + 1) / ATT_HEADS) for i in range(ATT_HEADS)], F32)
    prm = dict(w_in=w_in, layers=layers, ccb=tile4(cc), scb=tile4(sc), slopes=slopes,
               w_gate=w_exp_gate.astype(BF16), w_up=w_exp_up.astype(BF16), w_down=w_exp_down.astype(BF16),
               g_final=row(norm_final))
    bp, sp = x_prompt.shape[0], x_prompt.shape[1]
    bs, ssq = x_sample.shape[0], x_sample.shape[1]
    y_prompt = _trunk(x_prompt, p_prompt, prm, bp, sp)
    y_sample = _trunk(x_sample, p_sample, prm, bs, ssq)
    return (y_prompt, y_sample)
```

```python
import dataclasses
import functools
import math

import jax
import jax.numpy as jnp
from jax import lax
from jax.experimental import pallas as pl
from jax.experimental.pallas import tpu as pltpu
from jax.experimental.pallas import tpu_sc as plsc

D_MODEL = 1024
FN_WIDTH = 256
FN_HEADS = 4
FN_HEAD_DIM = 64
SSM_WIDTH = 256
SSM_GROUP = 16
SSM_GROUPS = 16
SSM_STATE = 64
ATT_WIDTH = 512
ATT_HEAD_DIM = 64
ATT_HEADS = 8
DILATED_PATTERNS = ((128, 1), (512, 4), (2048, 16))
IN_PROJ_WIDTH = 2048
N_EXPERTS = 16
EC_CAPACITY_FACTOR = 2
D_FF_EXPERT = 2048
PLE_DIM = 256
RMS_EPS = 1e-6
NEG_INF = -1e30

LANES = 128
SUBLANES = 8
BF16_ROWS = 16
FOURIER_ROWS = 512
FFN_COLS = 512
SSM_CHUNK = 8
SSM_ROW = SSM_CHUNK * SSM_WIDTH
SSM_NSTATE = SSM_GROUPS * SSM_STATE
S5_ROWS = 256
ATT_HALF = 64
ATT_UNROLL = 8
ATT_SPLIT = 4
ATT_OFFSETS = 3
FFN_ROWS = 1024
COMBINE_TILE = 256
COMBINE_SUB = 4
PROJ_ROWS = 1024
FFT_RADIX = 4
COMBINE_WIN = 64
VMEM_LIMIT = 56 * 1024 * 1024
SC_CORES = 2
SC_SUBCORES = 16
SC_LANES = 16
GATHER_ROWS = 64
RANK_CHUNK = 2048
PACKED = D_MODEL // 2

F32 = jnp.float32
BF16 = jnp.bfloat16
I32 = jnp.int32


def _cparams(sem):
    return pltpu.CompilerParams(dimension_semantics=sem, vmem_limit_bytes=VMEM_LIMIT)


def _rms(x, g):
    return x * lax.rsqrt(jnp.mean(x * x, axis=-1, keepdims=True) + RMS_EPS) * g


def _dot(a, b):
    return jnp.dot(a, b, preferred_element_type=F32)


def _dot_nt(a, b):
    return lax.dot_general(a, b, (((1,), (1,)), ((), ())), preferred_element_type=F32)


def _inproj_body(h_ref, g_ref, w_ref, zf_ref, zs_ref, zqkv_ref):
    a = _rms(h_ref[...], g_ref[...]).astype(BF16)
    z = _dot(a, w_ref[...])
    zf_ref[...] = z[:, :FN_WIDTH]
    zs_ref[...] = z[:, FN_WIDTH:FN_WIDTH + SSM_WIDTH]
    zqkv_ref[...] = z[:, FN_WIDTH + SSM_WIDTH:].astype(BF16)


def _inproj(h, g, w):
    n = h.shape[0]
    tm = PROJ_ROWS
    return pl.pallas_call(
        _inproj_body,
        grid=(n // tm,),
        in_specs=[pl.BlockSpec((tm, D_MODEL), lambda i: (i, 0)),
                  pl.BlockSpec((1, D_MODEL), lambda i: (0, 0)),
                  pl.BlockSpec((D_MODEL, IN_PROJ_WIDTH), lambda i: (0, 0))],
        out_specs=[pl.BlockSpec((tm, FN_WIDTH), lambda i: (i, 0)),
                   pl.BlockSpec((tm, SSM_WIDTH), lambda i: (i, 0)),
                   pl.BlockSpec((tm, 3 * ATT_WIDTH), lambda i: (i, 0))],
        out_shape=[jax.ShapeDtypeStruct((n, FN_WIDTH), F32),
                   jax.ShapeDtypeStruct((n, SSM_WIDTH), F32),
                   jax.ShapeDtypeStruct((n, 3 * ATT_WIDTH), BF16)],
        compiler_params=_cparams(("arbitrary",)),
        name="inproj",
    )(h, g, w)


def _fourier_body(xa_ref, xb_ref, cs_ref, ss_ref, tc_ref, ts_ref, cc_ref, sc_ref, wb_ref, g_ref, o_ref, *, s):
    quarter = s // FFT_RADIX
    def phase(q):
        return jnp.concatenate([xa_ref[0, pl.ds(q, quarter, stride=FFT_RADIX), :],
                                xb_ref[0, pl.ds(q, quarter, stride=FFT_RADIX), :]], axis=1)
    x4 = jnp.concatenate([phase(q) for q in range(FFT_RADIX)], axis=1).astype(BF16)
    y = _dot(cs_ref[...], x4)
    z = _dot(ss_ref[...], x4)
    wide = lambda t: jnp.concatenate([t] * (FN_WIDTH // LANES), axis=1)
    tcs, tss = [], []
    for q in range(FFT_RADIX):
        fc = y[:, q * FN_WIDTH:(q + 1) * FN_WIDTH]
        fs = z[:, q * FN_WIDTH:(q + 1) * FN_WIDTH]
        if q == 0:
            tcs.append(fc)
            tss.append(fs)
        else:
            c, sn = wide(tc_ref[q - 1]), wide(ts_ref[q - 1])
            tcs.append(c * fc - sn * fs)
            tss.append(c * fs + sn * fc)
    (c0, c1, c2, c3), (s0, s1, s2, s3) = tcs, tss
    parts = ((c0 + c1 + c2 + c3, s0 + s1 + s2 + s3),
             (c0 - s1 - c2 + s3, s0 + c1 - s2 - c3),
             (c0 - c1 + c2 - c3, s0 - s1 + s2 - s3),
             (c0 + s1 - c2 - s3, s0 - c1 - s2 + c3))
    for part, (yy, zz) in enumerate(parts):
        f = _dot(yy.astype(BF16), cc_ref[...]) - _dot(zz.astype(BF16), sc_ref[...])
        o = _dot(f.astype(BF16), wb_ref[...])
        o_ref[0, part] = _rms(o, g_ref[...]).astype(BF16)


def _dft_tables(n):
    def exact(rows):
        k = (rows[:, None] * jnp.arange(n, dtype=I32)[None, :]) % n
        ang = k.astype(F32) * (2.0 * math.pi / n)
        return jnp.cos(ang), jnp.sin(ang)
    if n <= LANES:
        return exact(jnp.arange(n, dtype=I32))
    ca, sa = exact(LANES * jnp.arange(n // LANES, dtype=I32))
    cb, sb = exact(jnp.arange(LANES, dtype=I32))
    cos = ca[:, None, :] * cb[None] - sa[:, None, :] * sb[None]
    sin = sa[:, None, :] * cb[None] + ca[:, None, :] * sb[None]
    return cos.reshape(n, n), sin.reshape(n, n)


def _fourier(zf, tables, ccb, scb, wb, g, b, s):
    cs, ss, tc, ts = tables
    quarter = s // FFT_RADIX
    tr = min(FOURIER_ROWS, quarter)
    x = zf.reshape(b, s, FN_WIDTH)
    fix = lambda i, j: (0, 0)
    out = pl.pallas_call(
        functools.partial(_fourier_body, s=s),
        grid=(quarter // tr, b),
        in_specs=[pl.BlockSpec((1, s, LANES), lambda i, j: (j, 0, 0)),
                  pl.BlockSpec((1, s, LANES), lambda i, j: (j, 0, 1)),
                  pl.BlockSpec((tr, quarter), lambda i, j: (i, 0)),
                  pl.BlockSpec((tr, quarter), lambda i, j: (i, 0)),
                  pl.BlockSpec((FFT_RADIX - 1, tr, LANES), lambda i, j: (0, i, 0)),
                  pl.BlockSpec((FFT_RADIX - 1, tr, LANES), lambda i, j: (0, i, 0)),
                  pl.BlockSpec((FN_WIDTH, FN_WIDTH), fix),
                  pl.BlockSpec((FN_WIDTH, FN_WIDTH), fix),
                  pl.BlockSpec((FN_WIDTH, FN_WIDTH), fix),
                  pl.BlockSpec((1, FN_WIDTH), fix)],
        out_specs=pl.BlockSpec((1, FFT_RADIX, tr, FN_WIDTH), lambda i, j: (j, 0, i, 0)),
        out_shape=jax.ShapeDtypeStruct((b, FFT_RADIX, quarter, FN_WIDTH), BF16),
        compiler_params=_cparams(("arbitrary", "arbitrary")),
        name="fourier",
    )(x, x, cs, ss, tc, ts, ccb, scb, wb, g)
    return out.reshape(b * s, FN_WIDTH)


def _fourier_tables(s):
    quarter = s // FFT_RADIX
    cs, ss = _dft_tables(quarter)
    kq = jnp.arange(1, FFT_RADIX, dtype=F32)[:, None] * jnp.arange(quarter, dtype=F32)[None, :]
    ang = kq * (2.0 * math.pi / s)
    wide = lambda v: jnp.broadcast_to(v[:, :, None], (FFT_RADIX - 1, quarter, LANES))
    return cs.astype(BF16), ss.astype(BF16), wide(jnp.cos(ang)), wide(jnp.sin(ang))


def _block_diag(blocks):
    h, a, bb = blocks.shape
    eye = jnp.eye(h, dtype=blocks.dtype)
    return jnp.einsum('hab,hg->hagb', blocks, eye).reshape(h * a, h * bb)


def _s5_matrices(a_re, a_im, log_dt, b_re, b_im, c_re, c_im):
    t = SSM_CHUNK
    g, p, c = SSM_GROUPS, SSM_STATE, SSM_GROUP
    lam = lax.complex(a_re.astype(F32), a_im.astype(F32))
    dt = jnp.exp(log_dt.astype(F32))[..., None]
    abar = jnp.exp(lam * dt)
    bbar = ((abar - 1.0) / lam)[..., None] * lax.complex(b_re.astype(F32), b_im.astype(F32))
    cmat = lax.complex(c_re.astype(F32), c_im.astype(F32))
    ks = jnp.arange(t + 1, dtype=F32)
    apow = jnp.exp((lam * dt)[:, None] * ks[None, :, None, None])

    kern = jnp.real(jnp.einsum('dgcp,dkgp,dgpe->dkgce', cmat, apow[:, :t], bbar))
    gh, hw, hs = g // 2, SSM_WIDTH // 2, SSM_NSTATE // 2
    lags = jnp.arange(-(t - 1), t)
    pick = lambda m: m[:, None, None, None]
    klag = (jnp.where(pick(lags >= 0), kern[0][jnp.clip(lags, 0, t - 1)], 0.0)
            + jnp.where(pick(lags <= 0), kern[1][jnp.clip(-lags, 0, t - 1)], 0.0))
    eye_h = jnp.eye(gh, dtype=F32)
    lag = jnp.arange(t)[None, :] - jnp.arange(t)[:, None]
    def _intra(h):
        blocks = jnp.einsum('lgce,gh->lgehc', klag[:, h * gh:(h + 1) * gh], eye_h).reshape(2 * t - 1, hw, hw)
        return blocks[lag + t - 1].transpose(0, 2, 1, 3).reshape(t * hw, t * hw)
    m_intra = jnp.stack([_intra(0), _intra(1)])

    wf = apow[0, t - 1 - jnp.arange(t)][:, :, :, None] * bbar[0][None]
    wb = apow[1, jnp.arange(t)][:, :, :, None] * bbar[1][None]
    same_group = ((jnp.arange(t * hw)[:, None] // c) % gh) == (jnp.arange(hs)[None, :] // p)
    def _spread(x, h):
        xh = x[:, h * gh:(h + 1) * gh].reshape(t * hw, p)
        return jnp.where(same_group, jnp.tile(xh, (1, gh)), 0.0)
    def _parts(parts, h):
        return jnp.concatenate([sign * _spread(x, h) for sign, x in parts], axis=1)
    wf, wb = jnp.swapaxes(wf, 2, 3), jnp.swapaxes(wb, 2, 3)
    in_parts = [(1.0, jnp.real(wf)), (1.0, jnp.imag(wf)), (1.0, jnp.real(wb)), (1.0, jnp.imag(wb))]
    w_in = jnp.stack([_parts(in_parts, 0), _parts(in_parts, 1)])

    qf = cmat[0][None] * apow[0, 1 + jnp.arange(t)][:, :, None, :]
    qb = cmat[1][None] * apow[1, t - jnp.arange(t)][:, :, None, :]
    out_parts = [(1.0, jnp.real(qf)), (-1.0, jnp.imag(qf)), (1.0, jnp.real(qb)), (-1.0, jnp.imag(qb))]
    w_out = jnp.stack([_parts(out_parts, 0).T, _parts(out_parts, 1).T])

    a_chunk = lam * dt * t
    return m_intra.astype(BF16), w_in.astype(BF16), w_out.astype(BF16), a_chunk


def _scan_powers(a_chunk, nsteps):
    e = jnp.exp(a_chunk[None] * (2.0 ** jnp.arange(nsteps, dtype=F32))[:, None, None, None])
    e = e.reshape(nsteps, 2 * SSM_NSTATE)
    return jnp.real(e), jnp.imag(e)


def _half_steps(z_ref, tr):
    return jnp.concatenate([z_ref[pl.ds(r, tr, stride=SSM_CHUNK), :] for r in range(SSM_CHUNK)], axis=1)


def _halves(rows):
    return [pl.BlockSpec((rows, LANES), lambda i: (i, 0)), pl.BlockSpec((rows, LANES), lambda i: (i, 1))]


def _s5_in_body(za_ref, zb_ref, w_ref, o_ref):
    tr = o_ref.shape[0]
    hs = SSM_NSTATE // 2
    for h, z_ref in enumerate((za_ref, zb_ref)):
        res = _dot(_half_steps(z_ref, tr).astype(BF16), w_ref[h])
        for part in range(4):
            lo = part * SSM_NSTATE + h * hs
            o_ref[:, lo:lo + hs] = res[:, part * hs:(part + 1) * hs]


def _s5_in(zs, w_in):
    rows = zs.shape[0] // SSM_CHUNK
    tr = min(S5_ROWS, rows)
    width = 4 * SSM_NSTATE
    return pl.pallas_call(
        _s5_in_body,
        grid=(rows // tr,),
        in_specs=_halves(tr * SSM_CHUNK) + [
            pl.BlockSpec((2, SSM_ROW // 2, width // 2), lambda i: (0, 0, 0), pipeline_mode=pl.Buffered(1))],
        out_specs=pl.BlockSpec((tr, width), lambda i: (i, 0)),
        out_shape=jax.ShapeDtypeStruct((rows, width), F32),
        compiler_params=_cparams(("arbitrary",)),
        name="s5_in",
    )(zs, zs, w_in)


def _s5_scan_body(x_ref, pr_ref, pi_ref, o_ref, *, nc, nsteps):
    ns = SSM_NSTATE
    row = lax.broadcasted_iota(I32, (nc, 1), 0)

    def direction(part, table_lo, forward):
        def shifted(v, sh):
            if forward:
                return jnp.where(row >= sh, pltpu.roll(v, sh, 0), 0.0)
            return jnp.where(row < nc - sh, pltpu.roll(v, nc - sh, 0), 0.0)

        def column(c, carry):
            off = pl.multiple_of(c * LANES, LANES)
            re_at = pl.ds(part * ns + off, LANES)
            im_at = pl.ds((part + 1) * ns + off, LANES)
            tab = pl.ds(table_lo + off, LANES)
            re, im = x_ref[0, :, re_at], x_ref[0, :, im_at]
            for k in range(nsteps):
                ar, ai = pr_ref[k:k + 1, tab], pi_ref[k:k + 1, tab]
                sre, sim = shifted(re, 2 ** k), shifted(im, 2 ** k)
                re, im = re + ar * sre - ai * sim, im + ar * sim + ai * sre
            o_ref[0, :, re_at] = shifted(re, 1).astype(BF16)
            o_ref[0, :, im_at] = shifted(im, 1).astype(BF16)
            return carry

        lax.fori_loop(0, ns // LANES, column, 0)

    direction(0, 0, True)
    direction(2, ns, False)


def _s5_scan(xl, pr, pi, b, nc):
    nsteps = int(math.log2(nc))
    width = 4 * SSM_NSTATE
    x = xl.reshape(b, nc, width)
    out = pl.pallas_call(
        functools.partial(_s5_scan_body, nc=nc, nsteps=nsteps),
        grid=(b,),
        in_specs=[pl.BlockSpec((1, nc, width), lambda i: (i, 0, 0)),
                  pl.BlockSpec((nsteps, 2 * SSM_NSTATE), lambda i: (0, 0)),
                  pl.BlockSpec((nsteps, 2 * SSM_NSTATE), lambda i: (0, 0))],
        out_specs=pl.BlockSpec((1, nc, width), lambda i: (i, 0, 0)),
        out_shape=jax.ShapeDtypeStruct((b, nc, width), BF16),
        compiler_params=_cparams(("arbitrary",)),
        name="s5_scan",
    )(x, pr, pi)
    return out.reshape(b * nc, width)


def _s5_out_body(za_ref, zb_ref, s_ref, m_ref, w_ref, d_ref, gw_ref, gb_ref, g_ref, o_ref, nat_a, nat_b):
    tr = s_ref.shape[0]
    hs = SSM_NSTATE // 2
    us, ys = [], []
    for h, z_ref in enumerate((za_ref, zb_ref)):
        u = _half_steps(z_ref, tr)
        st = jnp.concatenate([s_ref[:, part * SSM_NSTATE + h * hs:part * SSM_NSTATE + (h + 1) * hs]
                              for part in range(4)], axis=1)
        us.append(u)
        ys.append(_dot(st, w_ref[h]) + _dot(u.astype(BF16), m_ref[h]))
    c0 = math.sqrt(2.0 / math.pi)
    for r in range(SSM_CHUNK):
        step = slice(r * LANES, (r + 1) * LANES)
        y = jnp.concatenate([ys[0][:, step], ys[1][:, step]], axis=1)
        u = jnp.concatenate([us[0][:, step], us[1][:, step]], axis=1)
        v = y + d_ref[...] * u
        gl = 0.5 * v * (1.0 + jnp.tanh(c0 * (v + 0.044715 * (v * v * v))))
        gate = jax.nn.sigmoid(_dot(gl.astype(BF16), gw_ref[...]) + gb_ref[...])
        out = _rms(gl * gate, g_ref[...])
        nat_a[pl.ds(r, tr, stride=SSM_CHUNK), :] = out[:, :LANES]
        nat_b[pl.ds(r, tr, stride=SSM_CHUNK), :] = out[:, LANES:]
    o_ref[:, :LANES] = nat_a[...].astype(BF16)
    o_ref[:, LANES:] = nat_b[...].astype(BF16)


def _s5_out(zs, states, m_intra, w_out, d, glu_w, glu_b, g):
    rows = states.shape[0]
    tr = min(S5_ROWS, rows)
    width = 4 * SSM_NSTATE
    fix = lambda i: (0, 0)
    fix3 = lambda i: (0, 0, 0)
    once = pl.Buffered(1)
    return pl.pallas_call(
        _s5_out_body,
        grid=(rows // tr,),
        in_specs=_halves(tr * SSM_CHUNK) + [
                  pl.BlockSpec((tr, width), lambda i: (i, 0)),
                  pl.BlockSpec((2, SSM_ROW // 2, SSM_ROW // 2), fix3, pipeline_mode=once),
                  pl.BlockSpec((2, width // 2, SSM_ROW // 2), fix3, pipeline_mode=once),
                  pl.BlockSpec((1, SSM_WIDTH), fix), pl.BlockSpec((SSM_WIDTH, SSM_WIDTH), fix),
                  pl.BlockSpec((1, SSM_WIDTH), fix), pl.BlockSpec((1, SSM_WIDTH), fix)],
        out_specs=pl.BlockSpec((tr * SSM_CHUNK, SSM_WIDTH), lambda i: (i, 0)),
        out_shape=jax.ShapeDtypeStruct((rows * SSM_CHUNK, SSM_WIDTH), BF16),
        scratch_shapes=[pltpu.VMEM((tr * SSM_CHUNK, LANES), F32) for _ in range(SSM_WIDTH // LANES)],
        compiler_params=_cparams(("arbitrary",)),
        name="s5_out",
    )(zs, zs, states, m_intra, w_out, d, glu_w, glu_b, g)


def _attn_geometry(s, d):
    ln = s // d
    bq = min(128, ln)
    bk = min(bq + 2 * ATT_HALF, ln)
    return ln, bq, bk, ln // bq


def _attn_body(slope_ref, q_ref, k_ref, v_ref, o_ref, nat, qd, kd, vd, qdb, kdb, vdb, b0, b1, b2, *acc, s):
    hp = pl.program_id(1)
    s4 = s // ATT_SPLIT
    lane = lax.broadcasted_iota(I32, (1, LANES), 1)
    first = lane < ATT_HEAD_DIM
    second = jnp.logical_not(first)
    slopes = (slope_ref[2 * hp], slope_ref[2 * hp + 1])
    log2e = math.log2(math.e)
    scale = ATT_HEAD_DIM ** -0.5 * log2e

    @pl.when(pl.program_id(0) == 0)
    def _():
        for (_, d), bias in zip(DILATED_PATTERNS, (b0, b1, b2)):
            _, bq, bk, _ = _attn_geometry(s, d)
            jk = lax.broadcasted_iota(I32, (1, bk), 1)
            for o in range(ATT_OFFSETS):
                rel = jnp.abs(lax.broadcasted_iota(I32, (bq, 1), 0) + o * ATT_HALF - jk)
                dist = (d * rel).astype(F32) * log2e
                for hh in range(2):
                    bias[hp, 2 * o + hh] = jnp.where(rel <= ATT_HALF, -slopes[hh] * dist, NEG_INF)

    for src, dst_f, dst_b in ((q_ref, qd, qdb), (k_ref, kd, kdb), (v_ref, vd, vdb)):
        nat[...] = src[...].astype(F32)
        for c in range(ATT_SPLIT):
            x = nat[pl.ds(c, s4, stride=ATT_SPLIT), :]
            dst_f[c * s4:(c + 1) * s4, :] = x
            dst_b[c * s4:(c + 1) * s4, :] = x.astype(BF16)

    def pattern(p, d, refs, bias, locate, stride):
        acc_o, acc_m, acc_l = acc[3 * p], acc[3 * p + 1], acc[3 * p + 2]
        ln, bq, bk, nqb = _attn_geometry(s, d)

        def rows(c, j, size):
            start = locate(c, j)
            if stride == 1:
                return pl.ds(pl.multiple_of(start, ATT_HALF), size)
            return pl.ds(start, size, stride=stride)

        def scores(i):
            c = i // nqb
            j0 = (i % nqb) * bq
            ks = jnp.clip(j0 - ATT_HALF, 0, ln - bk)
            q = refs[0][rows(c, j0, bq), :].astype(BF16)
            k = refs[1][rows(c, ks, bk), :].astype(BF16)
            v = refs[2][rows(c, ks, bk), :].astype(BF16)
            off = (j0 - ks) // ATT_HALF
            scs = []
            for hh in range(2):
                qm = jnp.where(first if hh == 0 else second, q, jnp.zeros_like(q))
                scs.append(_dot_nt(qm, k) * scale + bias[hp, 2 * off + hh])
            return rows(c, j0, bq), v, scs

        def softmax(sc):
            m = jnp.max(sc, axis=-1, keepdims=True)
            pe = jnp.exp2(sc - m)
            return pe.astype(BF16), m, jnp.sum(pe, axis=-1, keepdims=True)

        def group(g, carry):
            staged = [scores(g * ATT_UNROLL + u) for u in range(ATT_UNROLL)]
            soft = [[softmax(sc) for sc in scs] for _, _, scs in staged]
            for (dst, v, _), ((p0, m0, l0), (p1, m1, l1)) in zip(staged, soft):
                acc_o[dst, :] = jnp.where(first, _dot(p0, v), _dot(p1, v))
                acc_m[dst, :] = jnp.where(first, m0, m1)
                acc_l[dst, :] = jnp.where(first, l0, l1)
            return carry

        lax.fori_loop(0, d * nqb // ATT_UNROLL, group, 0)

    (_, d1), (_, d2), (_, d3) = DILATED_PATTERNS
    assert d1 == 1 and d2 == ATT_SPLIT and d3 == ATT_SPLIT * ATT_SPLIT
    pattern(0, d1, (q_ref, k_ref, v_ref), b0, lambda c, j: j, 1)
    pattern(1, d2, (qdb, kdb, vdb), b1, lambda c, j: c * s4 + j, 1)
    pattern(2, d3, (qd, kd, vd), b2,
            lambda c, j: (c % ATT_SPLIT) * s4 + c // ATT_SPLIT + ATT_SPLIT * j, ATT_SPLIT)

    for c in range(ATT_SPLIT):
        part = pl.ds(c, s4, stride=ATT_SPLIT)
        blk = slice(c * s4, (c + 1) * s4)
        ms = (acc[1][part, :], acc[4][blk, :], acc[7][blk, :])
        os_ = (acc[0][part, :], acc[3][blk, :], acc[6][blk, :])
        ls = (acc[2][part, :], acc[5][blk, :], acc[8][blk, :])
        m = jnp.maximum(jnp.maximum(ms[0], ms[1]), ms[2])
        num = jnp.zeros((s4, LANES), F32)
        den = jnp.zeros((s4, LANES), F32)
        for p in range(3):
            w = jnp.exp2(ms[p] - m)
            num = num + w * os_[p]
            den = den + w * ls[p]
        nat[part, :] = num / den
    o_ref[...] = nat[...].astype(BF16)


def _attn(zqkv, slopes, b, s):
    n = b * s
    nhp = ATT_HEADS // 2
    col = lambda off: (lambda i, j, sl: (i, off + j))
    return pl.pallas_call(
        functools.partial(_attn_body, s=s),
        grid_spec=pltpu.PrefetchScalarGridSpec(
            num_scalar_prefetch=1,
            grid=(b, nhp),
            in_specs=[pl.BlockSpec((s, LANES), col(0)),
                      pl.BlockSpec((s, LANES), col(nhp)),
                      pl.BlockSpec((s, LANES), col(2 * nhp))],
            out_specs=pl.BlockSpec((s, LANES), lambda i, j, sl: (i, j)),
            scratch_shapes=([pltpu.VMEM((s, LANES), F32) for _ in range(4)]
                            + [pltpu.VMEM((s, LANES), BF16) for _ in range(3)]
                            + [pltpu.VMEM((nhp, 2 * ATT_OFFSETS) + _attn_geometry(s, d)[1:3], F32)
                               for _, d in DILATED_PATTERNS]
                            + [pltpu.VMEM((s, LANES), F32) for _ in range(9)]),
        ),
        out_shape=jax.ShapeDtypeStruct((n, ATT_WIDTH), BF16),
        compiler_params=_cparams(("arbitrary", "arbitrary")),
        name="attn",
    )(slopes, zqkv, zqkv, zqkv)


def _outproj_body(h_ref, mf_ref, ms_ref, oa_ref, ga_ref, w_ref, gf_ref, wr_ref,
                  h1_ref, m_ref, aff_ref):
    oa = _rms(oa_ref[...].astype(F32), ga_ref[...]).astype(BF16)
    o1, o2 = FN_WIDTH, FN_WIDTH + SSM_WIDTH
    acc = h_ref[...] + _dot(mf_ref[...], w_ref[0:o1, :]) + _dot(ms_ref[...], w_ref[o1:o2, :])
    acc = acc + _dot(oa, w_ref[o2:, :])
    h1_ref[...] = acc
    m = _rms(acc, gf_ref[...]).astype(BF16)
    bits = pltpu.bitcast(m.astype(F32), I32)
    m_ref[...] = lax.shift_right_logical(bits[:, :PACKED], 16) | (bits[:, PACKED:] & jnp.int32(-65536))
    lg = _dot(m, wr_ref[...])
    e = jnp.exp(lg - jnp.max(lg, axis=1, keepdims=True))
    aff_ref[...] = e / jnp.sum(e, axis=1, keepdims=True)


def _outproj(h, mf, ms, oa, ga, w, gf, wr):
    n = h.shape[0]
    tm = PROJ_ROWS
    row = lambda i: (i, 0)
    fix = lambda i: (0, 0)
    return pl.pallas_call(
        _outproj_body,
        grid=(n // tm,),
        in_specs=[pl.BlockSpec((tm, D_MODEL), row), pl.BlockSpec((tm, FN_WIDTH), row),
                  pl.BlockSpec((tm, SSM_WIDTH), row), pl.BlockSpec((tm, ATT_WIDTH), row),
                  pl.BlockSpec((1, ATT_WIDTH), fix), pl.BlockSpec((D_MODEL, D_MODEL), fix),
                  pl.BlockSpec((1, D_MODEL), fix), pl.BlockSpec((D_MODEL, N_EXPERTS), fix)],
        out_specs=[pl.BlockSpec((tm, D_MODEL), row), pl.BlockSpec((tm, PACKED), row),
                   pl.BlockSpec((tm, N_EXPERTS), row)],
        out_shape=[jax.ShapeDtypeStruct((n, D_MODEL), F32),
                   jax.ShapeDtypeStruct((n, PACKED), I32),
                   jax.ShapeDtypeStruct((n, N_EXPERTS), F32)],
        compiler_params=_cparams(("arbitrary",)),
        name="outproj",
    )(h, mf, ms, oa, ga, w, gf, wr)


def _select_body(aff_ref, rank_ref, off_ref, thr_ref, *, cap, nb):
    r128 = lax.broadcasted_iota(I32, (LANES, LANES), 0)
    c128 = lax.broadcasted_iota(I32, (LANES, LANES), 1)
    upper_incl = jnp.where(r128 <= c128, 1.0, 0.0).astype(BF16)
    ones = jnp.ones((LANES, LANES), BF16)
    rb = lax.broadcasted_iota(I32, (nb, nb), 0)
    cb = lax.broadcasted_iota(I32, (nb, nb), 1)
    lower_strict = jnp.where(cb < rb, 1.0, 0.0).astype(BF16)
    upper_strict = jnp.where(rb < cb, 1.0, 0.0).astype(BF16)
    ones8 = jnp.ones((8, LANES), BF16)

    def count(mask):
        c = jnp.sum(jnp.where(mask, 1.0, 0.0), axis=0, keepdims=True)
        return jnp.sum(c, axis=1, keepdims=True)

    bits_all = pltpu.bitcast(aff_ref[...], I32)

    def bitstep(i, t):
        cand = t | jnp.left_shift(jnp.int32(1), 30 - i)
        above = jnp.sum(jnp.where(bits_all >= cand, 1.0, 0.0), axis=1, keepdims=True)
        return jnp.where(jnp.sum(above, axis=2, keepdims=True) >= cap, cand, t)

    thr = lax.fori_loop(0, 31, bitstep, jnp.zeros((N_EXPERTS, 1, 1), I32))
    thr_ref[...] = jnp.broadcast_to(thr, thr_ref.shape)

    def prefix(mask):
        mb = jnp.where(mask, 1.0, 0.0).astype(BF16)
        incl = _dot(mb, upper_incl)
        tot = _dot(mb, ones)
        offs = _dot(lower_strict, tot.astype(BF16))
        return offs + incl - 1.0, mb

    def per_expert(e, carry):
        bits = pltpu.bitcast(aff_ref[e], I32)
        t = thr_ref[e][0:1, 0:1]
        gt = bits > t
        eq = bits == t
        need = cap - count(gt)
        eq_rank, _ = prefix(eq)
        sel = jnp.logical_or(gt, jnp.logical_and(eq, eq_rank < need))
        rank, mb = prefix(sel)
        rank_ref[e] = jnp.where(sel, rank.astype(I32), -1)
        tot_row = _dot_nt(ones8, mb)
        off_row = _dot(tot_row.astype(BF16), upper_strict)
        off_ref[pl.ds(e, 1), :] = off_row[0:1, :].astype(I32)
        return carry

    lax.fori_loop(0, N_EXPERTS, per_expert, 0)


def _select(aff_t, cap):
    n = aff_t.shape[1]
    nb = n // LANES
    a3 = aff_t.reshape(N_EXPERTS, nb, LANES)
    rank, off = pl.pallas_call(
        functools.partial(_select_body, cap=cap, nb=nb),
        grid=(1,),
        in_specs=[pl.BlockSpec((N_EXPERTS, nb, LANES), lambda i: (0, 0, 0))],
        out_specs=[pl.BlockSpec((N_EXPERTS, nb, LANES), lambda i: (0, 0, 0)),
                   pl.BlockSpec((N_EXPERTS, nb), lambda i: (0, 0))],
        out_shape=[jax.ShapeDtypeStruct((N_EXPERTS, nb, LANES), I32),
                   jax.ShapeDtypeStruct((N_EXPERTS, nb), I32)],
        scratch_shapes=[pltpu.VMEM((N_EXPERTS, SUBLANES, LANES), I32)],
        compiler_params=_cparams(("arbitrary",)),
        name="select",
    )(a3)
    return rank.reshape(N_EXPERTS, n), off


def _ffn_body(x_ref, wg_ref, wu_ref, wd_ref, o_ref):
    w = x_ref[...]
    x = jnp.concatenate([pltpu.bitcast(lax.shift_left(w, 16), F32),
                         pltpu.bitcast(w & jnp.int32(-65536), F32)], axis=1).astype(BF16)
    tf = FFN_COLS
    acc = jnp.zeros(o_ref.shape, F32)
    for j in range(D_FF_EXPERT // tf):
        g = _dot(x, wg_ref[:, j * tf:(j + 1) * tf])
        u = _dot(x, wu_ref[:, j * tf:(j + 1) * tf])
        hdn = (g * jax.nn.sigmoid(g) * u).astype(BF16)
        acc = acc + _dot(hdn, wd_ref[j * tf:(j + 1) * tf, :])
    o_ref[...] = acc.astype(BF16)


def _ffn(xe, wg, wu, wd, layer):
    e, cap, _ = xe.shape
    tm = min(FFN_ROWS, cap)
    return pl.pallas_call(
        _ffn_body,
        grid=(e, cap // tm),
        in_specs=[pl.BlockSpec((None, tm, PACKED), lambda i, j: (i, j, 0)),
                  pl.BlockSpec((None, None, D_MODEL, D_FF_EXPERT), lambda i, j: (layer, i, 0, 0)),
                  pl.BlockSpec((None, None, D_MODEL, D_FF_EXPERT), lambda i, j: (layer, i, 0, 0)),
                  pl.BlockSpec((None, None, D_FF_EXPERT, D_MODEL), lambda i, j: (layer, i, 0, 0))],
        out_specs=pl.BlockSpec((None, tm, D_MODEL), lambda i, j: (i, j, 0)),
        out_shape=jax.ShapeDtypeStruct((e, cap, D_MODEL), BF16),
        compiler_params=_cparams(("arbitrary", "arbitrary")),
        name="ffn",
    )(xe, wg, wu, wd)


def _combine_body(off_ref, h_ref, rank_ref, aff_ref, p_ref, gp_ref, wg_ref, wp_ref, gfin_ref, ye_ref,
                  o_ref, stack, sems, *, cap, last):
    t = pl.program_id(0)
    nt = pl.num_programs(0)
    tile, win, nsub = COMBINE_TILE, COMBINE_WIN, COMBINE_SUB
    bpt = tile // LANES
    kdim = N_EXPERTS * win

    def base(tt, e):
        return (off_ref[e, tt * bpt] // BF16_ROWS) * BF16_ROWS

    def copy(e, start, slot, sub):
        return pltpu.make_async_copy(ye_ref.at[e, pl.ds(pl.multiple_of(start, BF16_ROWS), win), :],
                                     stack.at[slot, sub, pl.ds(e * win, win), :], sems.at[slot, sub, e])

    def starts(tt, r):
        want = [base(tt, e) + r * win for e in range(N_EXPERTS)]
        return want, [jnp.minimum(w, cap - win) for w in want]

    def issue(tt, r, slot, sub):
        _, got = starts(tt, r)
        for e in range(N_EXPERTS):
            copy(e, got[e], slot, sub).start()

    def wait(tt, r, slot, sub):
        _, got = starts(tt, r)
        for e in range(N_EXPERTS):
            copy(e, got[e], slot, sub).wait()

    slot = t % 2

    @pl.when(t == 0)
    def _():
        for sub in range(nsub):
            issue(sub, 0, 0, sub)

    @pl.when(t + 1 < nt)
    def _():
        for sub in range(nsub):
            issue((t + 1) * nsub + sub, 0, 1 - slot, sub)

    lane16 = lax.broadcasted_iota(I32, (1, N_EXPERTS), 1)
    er = lax.broadcasted_iota(I32, (N_EXPERTS, kdim), 0)
    ec = lax.broadcasted_iota(I32, (N_EXPERTS, kdim), 1)
    expand = jnp.where(ec // win == er, 1.0, 0.0).astype(BF16)
    lane_in = (lax.broadcasted_iota(I32, (1, kdim), 1) % win).astype(F32)

    def row_of(vals):
        r = jnp.zeros((1, N_EXPERTS), I32)
        for e in range(N_EXPERTS):
            r = jnp.where(lane16 == e, vals[e], r)
        return r

    def accumulate(sub, r):
        rows = pl.ds(sub * tile, tile)
        want, got = starts(t * nsub + sub, r)
        wrow, grow = row_of(want), row_of(got)
        rk = rank_ref[rows, :]
        ok = jnp.logical_and(rk >= wrow, rk < grow + win)
        rel = jnp.where(ok, rk - grow, -1).astype(F32).astype(BF16)
        hit = _dot(rel, expand) == lane_in
        gates = _dot(aff_ref[rows, :].astype(BF16), expand)
        w = jnp.where(hit, gates, 0.0).astype(BF16)
        return _dot(w, stack[slot, sub])

    for sub in range(nsub):
        wait(t * nsub + sub, 0, slot, sub)
    for sub in range(nsub):
        rows = pl.ds(sub * tile, tile)
        o_ref[rows, :] = h_ref[rows, :] + accumulate(sub, 0)

    rounds = []
    for sub in range(nsub):
        tt = t * nsub + sub
        nr = jnp.int32(1)
        for e in range(N_EXPERTS):
            span = off_ref[e, (tt + 1) * bpt] - base(tt, e)
            nr = jnp.maximum(nr, (span + win - 1) // win)
        rounds.append(nr)

    def extra(r, carry):
        for sub in range(nsub):
            @pl.when(r < rounds[sub])
            def _(sub=sub):
                issue(t * nsub + sub, r, slot, sub)
        for sub in range(nsub):
            @pl.when(r < rounds[sub])
            def _(sub=sub):
                wait(t * nsub + sub, r, slot, sub)
                o_ref[pl.ds(sub * tile, tile), :] += accumulate(sub, r)
        return carry

    lax.fori_loop(1, functools.reduce(jnp.maximum, rounds), extra, 0)

    for sub in range(nsub):
        rows = pl.ds(sub * tile, tile)
        h2 = o_ref[rows, :]
        gate = jax.nn.sigmoid(_dot(_rms(h2, gp_ref[...]).astype(BF16), wg_ref[...]))
        h3 = h2 + _dot(p_ref[rows, :].astype(BF16), wp_ref[...]) * gate
        o_ref[rows, :] = _rms(h3, gfin_ref[...]) if last else h3


def _combine(off, h1, rank_tok, aff_tok, p, layer, gp, wg, wp, gfin, ye, cap, last):
    n = h1.shape[0]
    tile = COMBINE_TILE * COMBINE_SUB
    row = lambda i, o: (i, 0)
    fix = lambda i, o: (0, 0)
    prow = lambda i, o: (layer * (n // tile) + i, 0)
    return pl.pallas_call(
        functools.partial(_combine_body, cap=cap, last=last),
        grid_spec=pltpu.PrefetchScalarGridSpec(
            num_scalar_prefetch=1,
            grid=(n // tile,),
            in_specs=[pl.BlockSpec((tile, D_MODEL), row),
                      pl.BlockSpec((tile, N_EXPERTS), row),
                      pl.BlockSpec((tile, N_EXPERTS), row),
                      pl.BlockSpec((tile, PLE_DIM), prow),
                      pl.BlockSpec((1, D_MODEL), fix),
                      pl.BlockSpec((D_MODEL, D_MODEL), fix),
                      pl.BlockSpec((PLE_DIM, D_MODEL), fix),
                      pl.BlockSpec((1, D_MODEL), fix),
                      pl.BlockSpec(memory_space=pl.ANY)],
            out_specs=pl.BlockSpec((tile, D_MODEL), row),
            scratch_shapes=[pltpu.VMEM((2, COMBINE_SUB, N_EXPERTS * COMBINE_WIN, D_MODEL), BF16),
                            pltpu.SemaphoreType.DMA((2, COMBINE_SUB, N_EXPERTS))],
        ),
        out_shape=jax.ShapeDtypeStruct((n, D_MODEL), F32),
        compiler_params=_cparams(("arbitrary",)),
        name="combine",
    )(off, h1, rank_tok, aff_tok, p, gp, wg, wp, gfin, ye)


def _gather_rows(table, rank, cap):
    n, w = table.shape
    workers_per_expert = SC_CORES * SC_SUBCORES // N_EXPERTS
    per = cap // workers_per_expert
    nch = per // GATHER_ROWS
    mesh = plsc.VectorSubcoreMesh(core_axis_name="c", subcore_axis_name="s",
                                  num_cores=SC_CORES, num_subcores=SC_SUBCORES)
    cp = dataclasses.replace(pltpu.CompilerParams(), needs_layout_passes=False)

    @functools.partial(
        pl.kernel, mesh=mesh, compiler_params=cp,
        out_type=jax.ShapeDtypeStruct((N_EXPERTS * cap, w), I32),
        scratch_types=[pltpu.VMEM((RANK_CHUNK,), I32),
                       pltpu.VMEM((nch, GATHER_ROWS), I32),
                       pltpu.VMEM((GATHER_ROWS, w), I32),
                       pltpu.SemaphoreType.DMA],
        name="sc_gather")
    def gather(table_hbm, rank_hbm, out_hbm, rbuf, idx, rows, sem):
        wid = lax.axis_index("s") * SC_CORES + lax.axis_index("c")
        e = wid // workers_per_expert
        lo = (wid % workers_per_expert) * per
        lane = lax.iota(I32, SC_LANES)

        @pl.loop(0, n // RANK_CHUNK)
        def _(c):
            pltpu.sync_copy(rank_hbm.at[e, pl.ds(c * RANK_CHUNK, RANK_CHUNK)], rbuf)

            @pl.loop(0, RANK_CHUNK // SC_LANES)
            def _(i):
                rel = rbuf[pl.ds(i * SC_LANES, SC_LANES)] - lo
                mask = jnp.logical_and(rel >= 0, rel < per)
                rel = jnp.where(mask, rel, 0)
                tok = c * RANK_CHUNK + i * SC_LANES + lane
                plsc.store_scatter(idx, [rel // GATHER_ROWS, rel % GATHER_ROWS], tok, mask=mask)

        @pl.loop(0, nch)
        def _(c):
            pltpu.async_copy(table_hbm.at[idx.at[c]], rows, sem).wait()
            pltpu.sync_copy(rows, out_hbm.at[pl.ds(e * cap + lo + c * GATHER_ROWS, GATHER_ROWS)])

    return gather(table, rank).reshape(N_EXPERTS, cap, w)


def _trunk(x, p, prm, b, s):
    n = b * s
    nc = s // SSM_CHUNK
    cap = EC_CAPACITY_FACTOR * n // N_EXPERTS
    tables = _fourier_tables(s)
    h = x.reshape(n, D_MODEL)
    depth = prm['w_in'].shape[0]
    p_rows = p.reshape(depth * n, PLE_DIM)
    for l in range(depth):
        lp = prm['layers'][l]
        zf, zs, zqkv = _inproj(h, lp['g_mix'], lp['w_in'])
        mf = _fourier(zf, tables, prm['ccb'], prm['scb'], lp['w_fnet'], lp['g_f'], b, s)
        xl = _s5_in(zs, lp['s5_w_in'])
        pr, pi = _scan_powers(lp['s5_a_chunk'], int(math.log2(nc)))
        st = _s5_scan(xl, pr, pi, b, nc)
        ms = _s5_out(zs, st, lp['s5_m_intra'], lp['s5_w_out'], lp['s5_d'], lp['s5_glu_w'], lp['s5_glu_b'],
                     lp['g_s'])
        oa = _attn(zqkv, prm['slopes'], b, s)
        h1, m, aff = _outproj(h, mf, ms, oa, lp['g_a'], lp['w_out'], lp['g_ffn'], lp['w_router'])
        rank, off = _select(aff.T, cap)
        off = jnp.concatenate([off, jnp.full((N_EXPERTS, 1), cap, I32)], axis=1)
        xe = _gather_rows(m, rank, cap)
        ye = _ffn(xe, prm['w_gate'], prm['w_up'], prm['w_down'], l)
        h = _combine(off, h1, rank.T, aff, p_rows, l, lp['g_ple'], lp['w_ple_gate'],
                     lp['w_ple_proj'], prm['g_final'], ye, cap, l == depth - 1)
    return h.reshape(b, s, D_MODEL)


def kernel(x_prompt, x_sample, p_prompt, p_sample, norm_mix, w_in, w_fnet, ssm_a_re, ssm_a_im, ssm_log_dt, ssm_b_re, ssm_b_im, ssm_c_re, ssm_c_im, ssm_d, ssm_glu_w, ssm_glu_b, norm_branch, w_out, norm_ffn, w_router, w_exp_gate, w_exp_up, w_exp_down, norm_ple, w_ple_gate, w_ple_proj, norm_final):
    depth = w_in.shape[0]
    o1, o2 = FN_WIDTH, FN_WIDTH + SSM_WIDTH
    row = lambda v: v.reshape(1, -1).astype(F32)
    cc, sc = _dft_tables(FN_HEAD_DIM)
    tile4 = lambda mtx: _block_diag(jnp.broadcast_to(mtx[None], (FN_HEADS,) + mtx.shape)).astype(BF16)
    layers = []
    for l in range(depth):
        m_intra, s5_w_in, s5_w_out, a_chunk = _s5_matrices(
            ssm_a_re[l], ssm_a_im[l], ssm_log_dt[l], ssm_b_re[l], ssm_b_im[l], ssm_c_re[l], ssm_c_im[l])
        layers.append(dict(
            g_mix=row(norm_mix[l]), w_in=w_in[l].astype(BF16),
            w_fnet=_block_diag(w_fnet[l]).astype(BF16),
            g_f=row(norm_branch[l][:o1]), g_s=row(norm_branch[l][o1:o2]), g_a=row(norm_branch[l][o2:]),
            s5_m_intra=m_intra, s5_w_in=s5_w_in, s5_w_out=s5_w_out, s5_a_chunk=a_chunk,
            s5_d=row(ssm_d[l]), s5_glu_w=ssm_glu_w[l].astype(BF16), s5_glu_b=row(ssm_glu_b[l]),
            w_out=w_out[l].astype(BF16), g_ffn=row(norm_ffn[l]),
            w_router=w_router[l].astype(BF16),
            g_ple=row(norm_ple[l]), w_ple_gate=w_ple_gate[l].astype(BF16),
            w_ple_proj=w_ple_proj[l].astype(BF16)))
    slopes = jnp.asarray([2.0 ** (-8.0 * (i + 1) / ATT_HEADS) for i in range(ATT_HEADS)], F32)
    prm = dict(w_in=w_in, layers=layers, ccb=tile4(cc), scb=tile4(sc), slopes=slopes,
               w_gate=w_exp_gate.astype(BF16), w_up=w_exp_up.astype(BF16), w_down=w_exp_down.astype(BF16),
               g_final=row(norm_final))
    bp, sp = x_prompt.shape[0], x_prompt.shape[1]
    bs, ssq = x_sample.shape[0], x_sample.shape[1]
    y_prompt = _trunk(x_prompt, p_prompt, prm, bp, sp)
    y_sample = _trunk(x_sample, p_sample, prm, bs, ssq)
    return (y_prompt, y_sample)
```

```python
import dataclasses
import functools
import math

import jax
import jax.numpy as jnp
from jax import lax
from jax.experimental import pallas as pl
from jax.experimental.pallas import tpu as pltpu
from jax.experimental.pallas import tpu_sc as plsc

D_MODEL = 1024
FN_WIDTH = 256
FN_HEADS = 4
FN_HEAD_DIM = 64
SSM_WIDTH = 256
SSM_GROUP = 16
SSM_GROUPS = 16
SSM_STATE = 64
ATT_WIDTH = 512
ATT_HEAD_DIM = 64
ATT_HEADS = 8
DILATED_PATTERNS = ((128, 1), (512, 4), (2048, 16))
IN_PROJ_WIDTH = 2048
N_EXPERTS = 16
EC_CAPACITY_FACTOR = 2
D_FF_EXPERT = 2048
PLE_DIM = 256
RMS_EPS = 1e-6
NEG_INF = -1e30

LANES = 128
SUBLANES = 8
BF16_ROWS = 16
FOURIER_ROWS = 512
FFN_COLS = 512
SSM_CHUNK = 8
SSM_ROW = SSM_CHUNK * SSM_WIDTH
SSM_NSTATE = SSM_GROUPS * SSM_STATE
S5_ROWS = 256
ATT_HALF = 64
ATT_UNROLL = 8
ATT_SPLIT = 4
ATT_OFFSETS = 3
FFN_ROWS = 1024
COMBINE_TILE = 256
COMBINE_SUB = 4
PROJ_ROWS = 1024
FFT_RADIX = 4
COMBINE_WIN = 64
VMEM_LIMIT = 56 * 1024 * 1024
SC_CORES = 2
SC_SUBCORES = 16
SC_LANES = 16
GATHER_ROWS = 64
RANK_CHUNK = 2048
PACKED = D_MODEL // 2

F32 = jnp.float32
BF16 = jnp.bfloat16
I32 = jnp.int32


def _cparams(sem):
    return pltpu.CompilerParams(dimension_semantics=sem, vmem_limit_bytes=VMEM_LIMIT)


def _rms(x, g):
    return x * lax.rsqrt(jnp.mean(x * x, axis=-1, keepdims=True) + RMS_EPS) * g


def _dot(a, b):
    return jnp.dot(a, b, preferred_element_type=F32)


def _dot_nt(a, b):
    return lax.dot_general(a, b, (((1,), (1,)), ((), ())), preferred_element_type=F32)


def _inproj_body(h_ref, g_ref, w_ref, zf_ref, zs_ref, zqkv_ref):
    a = _rms(h_ref[...], g_ref[...]).astype(BF16)
    z = _dot(a, w_ref[...])
    zf_ref[...] = z[:, :FN_WIDTH]
    zs_ref[...] = z[:, FN_WIDTH:FN_WIDTH + SSM_WIDTH]
    zqkv_ref[...] = z[:, FN_WIDTH + SSM_WIDTH:].astype(BF16)


def _inproj(h, g, w):
    n = h.shape[0]
    tm = PROJ_ROWS
    return pl.pallas_call(
        _inproj_body,
        grid=(n // tm,),
        in_specs=[pl.BlockSpec((tm, D_MODEL), lambda i: (i, 0)),
                  pl.BlockSpec((1, D_MODEL), lambda i: (0, 0)),
                  pl.BlockSpec((D_MODEL, IN_PROJ_WIDTH), lambda i: (0, 0))],
        out_specs=[pl.BlockSpec((tm, FN_WIDTH), lambda i: (i, 0)),
                   pl.BlockSpec((tm, SSM_WIDTH), lambda i: (i, 0)),
                   pl.BlockSpec((tm, 3 * ATT_WIDTH), lambda i: (i, 0))],
        out_shape=[jax.ShapeDtypeStruct((n, FN_WIDTH), F32),
                   jax.ShapeDtypeStruct((n, SSM_WIDTH), F32),
                   jax.ShapeDtypeStruct((n, 3 * ATT_WIDTH), BF16)],
        compiler_params=_cparams(("arbitrary",)),
        name="inproj",
    )(h, g, w)


def _fourier_body(xa_ref, xb_ref, cs_ref, ss_ref, tc_ref, ts_ref, cc_ref, sc_ref, wb_ref, g_ref, o_ref, *, s):
    quarter = s // FFT_RADIX
    def phase(q):
        return jnp.concatenate([xa_ref[0, pl.ds(q, quarter, stride=FFT_RADIX), :],
                                xb_ref[0, pl.ds(q, quarter, stride=FFT_RADIX), :]], axis=1)
    x4 = jnp.concatenate([phase(q) for q in range(FFT_RADIX)], axis=1).astype(BF16)
    y = _dot(cs_ref[...], x4)
    z = _dot(ss_ref[...], x4)
    wide = lambda t: jnp.concatenate([t] * (FN_WIDTH // LANES), axis=1)
    tcs, tss = [], []
    for q in range(FFT_RADIX):
        fc = y[:, q * FN_WIDTH:(q + 1) * FN_WIDTH]
        fs = z[:, q * FN_WIDTH:(q + 1) * FN_WIDTH]
        if q == 0:
            tcs.append(fc)
            tss.append(fs)
        else:
            c, sn = wide(tc_ref[q - 1]), wide(ts_ref[q - 1])
            tcs.append(c * fc - sn * fs)
            tss.append(c * fs + sn * fc)
    (c0, c1, c2, c3), (s0, s1, s2, s3) = tcs, tss
    parts = ((c0 + c1 + c2 + c3, s0 + s1 + s2 + s3),
             (c0 - s1 - c2 + s3, s0 + c1 - s2 - c3),
             (c0 - c1 + c2 - c3, s0 - s1 + s2 - s3),
             (c0 + s1 - c2 - s3, s0 - c1 - s2 + c3))
    for part, (yy, zz) in enumerate(parts):
        f = _dot(yy.astype(BF16), cc_ref[...]) - _dot(zz.astype(BF16), sc_ref[...])
        o = _dot(f.astype(BF16), wb_ref[...])
        o_ref[0, part] = _rms(o, g_ref[...]).astype(BF16)


def _dft_tables(n):
    def exact(rows):
        k = (rows[:, None] * jnp.arange(n, dtype=I32)[None, :]) % n
        ang = k.astype(F32) * (2.0 * math.pi / n)
        return jnp.cos(ang), jnp.sin(ang)
    if n <= LANES:
        return exact(jnp.arange(n, dtype=I32))
    ca, sa = exact(LANES * jnp.arange(n // LANES, dtype=I32))
    cb, sb = exact(jnp.arange(LANES, dtype=I32))
    cos = ca[:, None, :] * cb[None] - sa[:, None, :] * sb[None]
    sin = sa[:, None, :] * cb[None] + ca[:, None, :] * sb[None]
    return cos.reshape(n, n), sin.reshape(n, n)


def _fourier(zf, tables, ccb, scb, wb, g, b, s):
    cs, ss, tc, ts = tables
    quarter = s // FFT_RADIX
    tr = min(FOURIER_ROWS, quarter)
    x = zf.reshape(b, s, FN_WIDTH)
    fix = lambda i, j: (0, 0)
    out = pl.pallas_call(
        functools.partial(_fourier_body, s=s),
        grid=(quarter // tr, b),
        in_specs=[pl.BlockSpec((1, s, LANES), lambda i, j: (j, 0, 0)),
                  pl.BlockSpec((1, s, LANES), lambda i, j: (j, 0, 1)),
                  pl.BlockSpec((tr, quarter), lambda i, j: (i, 0)),
                  pl.BlockSpec((tr, quarter), lambda i, j: (i, 0)),
                  pl.BlockSpec((FFT_RADIX - 1, tr, LANES), lambda i, j: (0, i, 0)),
                  pl.BlockSpec((FFT_RADIX - 1, tr, LANES), lambda i, j: (0, i, 0)),
                  pl.BlockSpec((FN_WIDTH, FN_WIDTH), fix),
                  pl.BlockSpec((FN_WIDTH, FN_WIDTH), fix),
                  pl.BlockSpec((FN_WIDTH, FN_WIDTH), fix),
                  pl.BlockSpec((1, FN_WIDTH), fix)],
        out_specs=pl.BlockSpec((1, FFT_RADIX, tr, FN_WIDTH), lambda i, j: (j, 0, i, 0)),
        out_shape=jax.ShapeDtypeStruct((b, FFT_RADIX, quarter, FN_WIDTH), BF16),
        compiler_params=_cparams(("arbitrary", "arbitrary")),
        name="fourier",
    )(x, x, cs, ss, tc, ts, ccb, scb, wb, g)
    return out.reshape(b * s, FN_WIDTH)


def _fourier_tables(s):
    quarter = s // FFT_RADIX
    cs, ss = _dft_tables(quarter)
    kq = jnp.arange(1, FFT_RADIX, dtype=F32)[:, None] * jnp.arange(quarter, dtype=F32)[None, :]
    ang = kq * (2.0 * math.pi / s)
    wide = lambda v: jnp.broadcast_to(v[:, :, None], (FFT_RADIX - 1, quarter, LANES))
    return cs.astype(BF16), ss.astype(BF16), wide(jnp.cos(ang)), wide(jnp.sin(ang))


def _block_diag(blocks):
    h, a, bb = blocks.shape
    eye = jnp.eye(h, dtype=blocks.dtype)
    return jnp.einsum('hab,hg->hagb', blocks, eye).reshape(h * a, h * bb)


def _s5_matrices(a_re, a_im, log_dt, b_re, b_im, c_re, c_im):
    t = SSM_CHUNK
    g, p, c = SSM_GROUPS, SSM_STATE, SSM_GROUP
    lam = lax.complex(a_re.astype(F32), a_im.astype(F32))
    dt = jnp.exp(log_dt.astype(F32))[..., None]
    abar = jnp.exp(lam * dt)
    bbar = ((abar - 1.0) / lam)[..., None] * lax.complex(b_re.astype(F32), b_im.astype(F32))
    cmat = lax.complex(c_re.astype(F32), c_im.astype(F32))
    ks = jnp.arange(t + 1, dtype=F32)
    apow = jnp.exp((lam * dt)[:, None] * ks[None, :, None, None])

    kern = jnp.real(jnp.einsum('dgcp,dkgp,dgpe->dkgce', cmat, apow[:, :t], bbar))
    gh, hw, hs = g // 2, SSM_WIDTH // 2, SSM_NSTATE // 2
    lags = jnp.arange(-(t - 1), t)
    pick = lambda m: m[:, None, None, None]
    klag = (jnp.where(pick(lags >= 0), kern[0][jnp.clip(lags, 0, t - 1)], 0.0)
            + jnp.where(pick(lags <= 0), kern[1][jnp.clip(-lags, 0, t - 1)], 0.0))
    eye_h = jnp.eye(gh, dtype=F32)
    lag = jnp.arange(t)[None, :] - jnp.arange(t)[:, None]
    def _intra(h):
        blocks = jnp.einsum('lgce,gh->lgehc', klag[:, h * gh:(h + 1) * gh], eye_h).reshape(2 * t - 1, hw, hw)
        return blocks[lag + t - 1].transpose(0, 2, 1, 3).reshape(t * hw, t * hw)
    m_intra = jnp.stack([_intra(0), _intra(1)])

    wf = apow[0, t - 1 - jnp.arange(t)][:, :, :, None] * bbar[0][None]
    wb = apow[1, jnp.arange(t)][:, :, :, None] * bbar[1][None]
    same_group = ((jnp.arange(t * hw)[:, None] // c) % gh) == (jnp.arange(hs)[None, :] // p)
    def _spread(x, h):
        xh = x[:, h * gh:(h + 1) * gh].reshape(t * hw, p)
        return jnp.where(same_group, jnp.tile(xh, (1, gh)), 0.0)
    def _parts(parts, h):
        return jnp.concatenate([sign * _spread(x, h) for sign, x in parts], axis=1)
    wf, wb = jnp.swapaxes(wf, 2, 3), jnp.swapaxes(wb, 2, 3)
    in_parts = [(1.0, jnp.real(wf)), (1.0, jnp.imag(wf)), (1.0, jnp.real(wb)), (1.0, jnp.imag(wb))]
    w_in = jnp.stack([_parts(in_parts, 0), _parts(in_parts, 1)])

    qf = cmat[0][None] * apow[0, 1 + jnp.arange(t)][:, :, None, :]
    qb = cmat[1][None] * apow[1, t - jnp.arange(t)][:, :, None, :]
    out_parts = [(1.0, jnp.real(qf)), (-1.0, jnp.imag(qf)), (1.0, jnp.real(qb)), (-1.0, jnp.imag(qb))]
    w_out = jnp.stack([_parts(out_parts, 0).T, _parts(out_parts, 1).T])

    a_chunk = lam * dt * t
    return m_intra.astype(BF16), w_in.astype(BF16), w_out.astype(BF16), a_chunk


def _scan_powers(a_chunk, nsteps):
    e = jnp.exp(a_chunk[None] * (2.0 ** jnp.arange(nsteps, dtype=F32))[:, None, None, None])
    e = e.reshape(nsteps, 2 * SSM_NSTATE)
    return jnp.real(e), jnp.imag(e)


def _half_steps(z_ref, tr):
    return jnp.concatenate([z_ref[pl.ds(r, tr, stride=SSM_CHUNK), :] for r in range(SSM_CHUNK)], axis=1)


def _halves(rows):
    return [pl.BlockSpec((rows, LANES), lambda i: (i, 0)), pl.BlockSpec((rows, LANES), lambda i: (i, 1))]


def _s5_in_body(za_ref, zb_ref, w_ref, o_ref):
    tr = o_ref.shape[0]
    hs = SSM_NSTATE // 2
    for h, z_ref in enumerate((za_ref, zb_ref)):
        res = _dot(_half_steps(z_ref, tr).astype(BF16), w_ref[h])
        for part in range(4):
            lo = part * SSM_NSTATE + h * hs
            o_ref[:, lo:lo + hs] = res[:, part * hs:(part + 1) * hs]


def _s5_in(zs, w_in):
    rows = zs.shape[0] // SSM_CHUNK
    tr = min(S5_ROWS, rows)
    width = 4 * SSM_NSTATE
    return pl.pallas_call(
        _s5_in_body,
        grid=(rows // tr,),
        in_specs=_halves(tr * SSM_CHUNK) + [
            pl.BlockSpec((2, SSM_ROW // 2, width // 2), lambda i: (0, 0, 0), pipeline_mode=pl.Buffered(1))],
        out_specs=pl.BlockSpec((tr, width), lambda i: (i, 0)),
        out_shape=jax.ShapeDtypeStruct((rows, width), F32),
        compiler_params=_cparams(("arbitrary",)),
        name="s5_in",
    )(zs, zs, w_in)


def _s5_scan_body(x_ref, pr_ref, pi_ref, o_ref, *, nc, nsteps):
    ns = SSM_NSTATE
    row = lax.broadcasted_iota(I32, (nc, 1), 0)

    def direction(part, table_lo, forward):
        def shifted(v, sh):
            if forward:
                return jnp.where(row >= sh, pltpu.roll(v, sh, 0), 0.0)
            return jnp.where(row < nc - sh, pltpu.roll(v, nc - sh, 0), 0.0)

        def column(c, carry):
            off = pl.multiple_of(c * LANES, LANES)
            re_at = pl.ds(part * ns + off, LANES)
            im_at = pl.ds((part + 1) * ns + off, LANES)
            tab = pl.ds(table_lo + off, LANES)
            re, im = x_ref[0, :, re_at], x_ref[0, :, im_at]
            for k in range(nsteps):
                ar, ai = pr_ref[k:k + 1, tab], pi_ref[k:k + 1, tab]
                sre, sim = shifted(re, 2 ** k), shifted(im, 2 ** k)
                re, im = re + ar * sre - ai * sim, im + ar * sim + ai * sre
            o_ref[0, :, re_at] = shifted(re, 1).astype(BF16)
            o_ref[0, :, im_at] = shifted(im, 1).astype(BF16)
            return carry

        lax.fori_loop(0, ns // LANES, column, 0)

    direction(0, 0, True)
    direction(2, ns, False)


def _s5_scan(xl, pr, pi, b, nc):
    nsteps = int(math.log2(nc))
    width = 4 * SSM_NSTATE
    x = xl.reshape(b, nc, width)
    out = pl.pallas_call(
        functools.partial(_s5_scan_body, nc=nc, nsteps=nsteps),
        grid=(b,),
        in_specs=[pl.BlockSpec((1, nc, width), lambda i: (i, 0, 0)),
                  pl.BlockSpec((nsteps, 2 * SSM_NSTATE), lambda i: (0, 0)),
                  pl.BlockSpec((nsteps, 2 * SSM_NSTATE), lambda i: (0, 0))],
        out_specs=pl.BlockSpec((1, nc, width), lambda i: (i, 0, 0)),
        out_shape=jax.ShapeDtypeStruct((b, nc, width), BF16),
        compiler_params=_cparams(("arbitrary",)),
        name="s5_scan",
    )(x, pr, pi)
    return out.reshape(b * nc, width)


def _s5_out_body(za_ref, zb_ref, s_ref, m_ref, w_ref, d_ref, gw_ref, gb_ref, g_ref, o_ref, nat_a, nat_b):
    tr = s_ref.shape[0]
    hs = SSM_NSTATE // 2
    us, ys = [], []
    for h, z_ref in enumerate((za_ref, zb_ref)):
        u = _half_steps(z_ref, tr)
        st = jnp.concatenate([s_ref[:, part * SSM_NSTATE + h * hs:part * SSM_NSTATE + (h + 1) * hs]
                              for part in range(4)], axis=1)
        us.append(u)
        ys.append(_dot(st, w_ref[h]) + _dot(u.astype(BF16), m_ref[h]))
    c0 = math.sqrt(2.0 / math.pi)
    for r in range(SSM_CHUNK):
        step = slice(r * LANES, (r + 1) * LANES)
        y = jnp.concatenate([ys[0][:, step], ys[1][:, step]], axis=1)
        u = jnp.concatenate([us[0][:, step], us[1][:, step]], axis=1)
        v = y + d_ref[...] * u
        gl = 0.5 * v * (1.0 + jnp.tanh(c0 * (v + 0.044715 * (v * v * v))))
        gate = jax.nn.sigmoid(_dot(gl.astype(BF16), gw_ref[...]) + gb_ref[...])
        out = _rms(gl * gate, g_ref[...])
        nat_a[pl.ds(r, tr, stride=SSM_CHUNK), :] = out[:, :LANES]
        nat_b[pl.ds(r, tr, stride=SSM_CHUNK), :] = out[:, LANES:]
    o_ref[:, :LANES] = nat_a[...].astype(BF16)
    o_ref[:, LANES:] = nat_b[...].astype(BF16)


def _s5_out(zs, states, m_intra, w_out, d, glu_w, glu_b, g):
    rows = states.shape[0]
    tr = min(S5_ROWS, rows)
    width = 4 * SSM_NSTATE
    fix = lambda i: (0, 0)
    fix3 = lambda i: (0, 0, 0)
    once = pl.Buffered(1)
    return pl.pallas_call(
        _s5_out_body,
        grid=(rows // tr,),
        in_specs=_halves(tr * SSM_CHUNK) + [
                  pl.BlockSpec((tr, width), lambda i: (i, 0)),
                  pl.BlockSpec((2, SSM_ROW // 2, SSM_ROW // 2), fix3, pipeline_mode=once),
                  pl.BlockSpec((2, width // 2, SSM_ROW // 2), fix3, pipeline_mode=once),
                  pl.BlockSpec((1, SSM_WIDTH), fix), pl.BlockSpec((SSM_WIDTH, SSM_WIDTH), fix),
                  pl.BlockSpec((1, SSM_WIDTH), fix), pl.BlockSpec((1, SSM_WIDTH), fix)],
        out_specs=pl.BlockSpec((tr * SSM_CHUNK, SSM_WIDTH), lambda i: (i, 0)),
        out_shape=jax.ShapeDtypeStruct((rows * SSM_CHUNK, SSM_WIDTH), BF16),
        scratch_shapes=[pltpu.VMEM((tr * SSM_CHUNK, LANES), F32) for _ in range(SSM_WIDTH // LANES)],
        compiler_params=_cparams(("arbitrary",)),
        name="s5_out",
    )(zs, zs, states, m_intra, w_out, d, glu_w, glu_b, g)


def _attn_geometry(s, d):
    ln = s // d
    bq = min(128, ln)
    bk = min(bq + 2 * ATT_HALF, ln)
    return ln, bq, bk, ln // bq


def _attn_body(slope_ref, q_ref, k_ref, v_ref, o_ref, nat, qd, kd, vd, qdb, kdb, vdb, b0, b1, b2, *acc, s):
    hp = pl.program_id(1)
    s4 = s // ATT_SPLIT
    lane = lax.broadcasted_iota(I32, (1, LANES), 1)
    first = lane < ATT_HEAD_DIM
    second = jnp.logical_not(first)
    slopes = (slope_ref[2 * hp], slope_ref[2 * hp + 1])
    log2e = math.log2(math.e)
    scale = ATT_HEAD_DIM ** -0.5 * log2e

    @pl.when(pl.program_id(0) == 0)
    def _():
        for (_, d), bias in zip(DILATED_PATTERNS, (b0, b1, b2)):
            _, bq, bk, _ = _attn_geometry(s, d)
            jk = lax.broadcasted_iota(I32, (1, bk), 1)
            for o in range(ATT_OFFSETS):
                rel = jnp.abs(lax.broadcasted_iota(I32, (bq, 1), 0) + o * ATT_HALF - jk)
                dist = (d * rel).astype(F32) * log2e
                for hh in range(2):
                    bias[hp, 2 * o + hh] = jnp.where(rel <= ATT_HALF, -slopes[hh] * dist, NEG_INF)

    for src, dst_f, dst_b in ((q_ref, qd, qdb), (k_ref, kd, kdb), (v_ref, vd, vdb)):
        nat[...] = src[...].astype(F32)
        for c in range(ATT_SPLIT):
            x = nat[pl.ds(c, s4, stride=ATT_SPLIT), :]
            dst_f[c * s4:(c + 1) * s4, :] = x
            dst_b[c * s4:(c + 1) * s4, :] = x.astype(BF16)

    def pattern(p, d, refs, bias, locate, stride):
        acc_o, acc_m, acc_l = acc[3 * p], acc[3 * p + 1], acc[3 * p + 2]
        ln, bq, bk, nqb = _attn_geometry(s, d)

        def rows(c, j, size):
            start = locate(c, j)
            if stride == 1:
                return pl.ds(pl.multiple_of(start, ATT_HALF), size)
            return pl.ds(start, size, stride=stride)

        def scores(i):
            c = i // nqb
            j0 = (i % nqb) * bq
            ks = jnp.clip(j0 - ATT_HALF, 0, ln - bk)
            q = refs[0][rows(c, j0, bq), :].astype(BF16)
            k = refs[1][rows(c, ks, bk), :].astype(BF16)
            v = refs[2][rows(c, ks, bk), :].astype(BF16)
            off = (j0 - ks) // ATT_HALF
            scs = []
            for hh in range(2):
                qm = jnp.where(first if hh == 0 else second, q, jnp.zeros_like(q))
                scs.append(_dot_nt(qm, k) * scale + bias[hp, 2 * off + hh])
            return rows(c, j0, bq), v, scs

        def softmax(sc):
            m = jnp.max(sc, axis=-1, keepdims=True)
            pe = jnp.exp2(sc - m)
            return pe.astype(BF16), m, jnp.sum(pe, axis=-1, keepdims=True)

        def group(g, carry):
            staged = [scores(g * ATT_UNROLL + u) for u in range(ATT_UNROLL)]
            soft = [[softmax(sc) for sc in scs] for _, _, scs in staged]
            for (dst, v, _), ((p0, m0, l0), (p1, m1, l1)) in zip(staged, soft):
                acc_o[dst, :] = jnp.where(first, _dot(p0, v), _dot(p1, v))
                acc_m[dst, :] = jnp.where(first, m0, m1)
                acc_l[dst, :] = jnp.where(first, l0, l1)
            return carry

        lax.fori_loop(0, d * nqb // ATT_UNROLL, group, 0)

    (_, d1), (_, d2), (_, d3) = DILATED_PATTERNS
    assert d1 == 1 and d2 == ATT_SPLIT and d3 == ATT_SPLIT * ATT_SPLIT
    pattern(0, d1, (q_ref, k_ref, v_ref), b0, lambda c, j: j, 1)
    pattern(1, d2, (qdb, kdb, vdb), b1, lambda c, j: c * s4 + j, 1)
    pattern(2, d3, (qd, kd, vd), b2,
            lambda c, j: (c % ATT_SPLIT) * s4 + c // ATT_SPLIT + ATT_SPLIT * j, ATT_SPLIT)

    for c in range(ATT_SPLIT):
        part = pl.ds(c, s4, stride=ATT_SPLIT)
        blk = slice(c * s4, (c + 1) * s4)
        ms = (acc[1][part, :], acc[4][blk, :], acc[7][blk, :])
        os_ = (acc[0][part, :], acc[3][blk, :], acc[6][blk, :])
        ls = (acc[2][part, :], acc[5][blk, :], acc[8][blk, :])
        m = jnp.maximum(jnp.maximum(ms[0], ms[1]), ms[2])
        num = jnp.zeros((s4, LANES), F32)
        den = jnp.zeros((s4, LANES), F32)
        for p in range(3):
            w = jnp.exp2(ms[p] - m)
            num = num + w * os_[p]
            den = den + w * ls[p]
        nat[part, :] = num / den
    o_ref[...] = nat[...].astype(BF16)


def _attn(zqkv, slopes, b, s):
    n = b * s
    nhp = ATT_HEADS // 2
    col = lambda off: (lambda i, j, sl: (i, off + j))
    return pl.pallas_call(
        functools.partial(_attn_body, s=s),
        grid_spec=pltpu.PrefetchScalarGridSpec(
            num_scalar_prefetch=1,
            grid=(b, nhp),
            in_specs=[pl.BlockSpec((s, LANES), col(0)),
                      pl.BlockSpec((s, LANES), col(nhp)),
                      pl.BlockSpec((s, LANES), col(2 * nhp))],
            out_specs=pl.BlockSpec((s, LANES), lambda i, j, sl: (i, j)),
            scratch_shapes=([pltpu.VMEM((s, LANES), F32) for _ in range(4)]
                            + [pltpu.VMEM((s, LANES), BF16) for _ in range(3)]
                            + [pltpu.VMEM((nhp, 2 * ATT_OFFSETS) + _attn_geometry(s, d)[1:3], F32)
                               for _, d in DILATED_PATTERNS]
                            + [pltpu.VMEM((s, LANES), F32) for _ in range(9)]),
        ),
        out_shape=jax.ShapeDtypeStruct((n, ATT_WIDTH), BF16),
        compiler_params=_cparams(("arbitrary", "arbitrary")),
        name="attn",
    )(slopes, zqkv, zqkv, zqkv)


def _outproj_body(h_ref, mf_ref, ms_ref, oa_ref, ga_ref, w_ref, gf_ref, wr_ref,
                  h1_ref, m_ref, aff_ref):
    oa = _rms(oa_ref[...].astype(F32), ga_ref[...]).astype(BF16)
    o1, o2 = FN_WIDTH, FN_WIDTH + SSM_WIDTH
    acc = h_ref[...] + _dot(mf_ref[...], w_ref[0:o1, :]) + _dot(ms_ref[...], w_ref[o1:o2, :])
    acc = acc + _dot(oa, w_ref[o2:, :])
    h1_ref[...] = acc
    m = _rms(acc, gf_ref[...]).astype(BF16)
    bits = pltpu.bitcast(m.astype(F32), I32)
    m_ref[...] = lax.shift_right_logical(bits[:, :PACKED], 16) | (bits[:, PACKED:] & jnp.int32(-65536))
    lg = _dot(m, wr_ref[...])
    e = jnp.exp(lg - jnp.max(lg, axis=1, keepdims=True))
    aff_ref[...] = e / jnp.sum(e, axis=1, keepdims=True)


def _outproj(h, mf, ms, oa, ga, w, gf, wr):
    n = h.shape[0]
    tm = PROJ_ROWS
    row = lambda i: (i, 0)
    fix = lambda i: (0, 0)
    return pl.pallas_call(
        _outproj_body,
        grid=(n // tm,),
        in_specs=[pl.BlockSpec((tm, D_MODEL), row), pl.BlockSpec((tm, FN_WIDTH), row),
                  pl.BlockSpec((tm, SSM_WIDTH), row), pl.BlockSpec((tm, ATT_WIDTH), row),
                  pl.BlockSpec((1, ATT_WIDTH), fix), pl.BlockSpec((D_MODEL, D_MODEL), fix),
                  pl.BlockSpec((1, D_MODEL), fix), pl.BlockSpec((D_MODEL, N_EXPERTS), fix)],
        out_specs=[pl.BlockSpec((tm, D_MODEL), row), pl.BlockSpec((tm, PACKED), row),
                   pl.BlockSpec((tm, N_EXPERTS), row)],
        out_shape=[jax.ShapeDtypeStruct((n, D_MODEL), F32),
                   jax.ShapeDtypeStruct((n, PACKED), I32),
                   jax.ShapeDtypeStruct((n, N_EXPERTS), F32)],
        compiler_params=_cparams(("arbitrary",)),
        name="outproj",
    )(h, mf, ms, oa, ga, w, gf, wr)


def _select_body(aff_ref, rank_ref, off_ref, thr_ref, *, cap, nb):
    r128 = lax.broadcasted_iota(I32, (LANES, LANES), 0)
    c128 = lax.broadcasted_iota(I32, (LANES, LANES), 1)
    upper_incl = jnp.where(r128 <= c128, 1.0, 0.0).astype(BF16)
    ones = jnp.ones((LANES, LANES), BF16)
    rb = lax.broadcasted_iota(I32, (nb, nb), 0)
    cb = lax.broadcasted_iota(I32, (nb, nb), 1)
    lower_strict = jnp.where(cb < rb, 1.0, 0.0).astype(BF16)
    upper_strict = jnp.where(rb < cb, 1.0, 0.0).astype(BF16)
    ones8 = jnp.ones((8, LANES), BF16)

    def count(mask):
        c = jnp.sum(jnp.where(mask, 1.0, 0.0), axis=0, keepdims=True)
        return jnp.sum(c, axis=1, keepdims=True)

    bits_all = pltpu.bitcast(aff_ref[...], I32)

    def bitstep(i, t):
        cand = t | jnp.left_shift(jnp.int32(1), 30 - i)
        above = jnp.sum(jnp.where(bits_all >= cand, 1.0, 0.0), axis=1, keepdims=True)
        return jnp.where(jnp.sum(above, axis=2, keepdims=True) >= cap, cand, t)

    thr = lax.fori_loop(0, 31, bitstep, jnp.zeros((N_EXPERTS, 1, 1), I32))
    thr_ref[...] = jnp.broadcast_to(thr, thr_ref.shape)

    def prefix(mask):
        mb = jnp.where(mask, 1.0, 0.0).astype(BF16)
        incl = _dot(mb, upper_incl)
        tot = _dot(mb, ones)
        offs = _dot(lower_strict, tot.astype(BF16))
        return offs + incl - 1.0, mb

    def per_expert(e, carry):
        bits = pltpu.bitcast(aff_ref[e], I32)
        t = thr_ref[e][0:1, 0:1]
        gt = bits > t
        eq = bits == t
        need = cap - count(gt)
        eq_rank, _ = prefix(eq)
        sel = jnp.logical_or(gt, jnp.logical_and(eq, eq_rank < need))
        rank, mb = prefix(sel)
        rank_ref[e] = jnp.where(sel, rank.astype(I32), -1)
        tot_row = _dot_nt(ones8, mb)
        off_row = _dot(tot_row.astype(BF16), upper_strict)
        off_ref[pl.ds(e, 1), :] = off_row[0:1, :].astype(I32)
        return carry

    lax.fori_loop(0, N_EXPERTS, per_expert, 0)


def _select(aff_t, cap):
    n = aff_t.shape[1]
    nb = n // LANES
    a3 = aff_t.reshape(N_EXPERTS, nb, LANES)
    rank, off = pl.pallas_call(
        functools.partial(_select_body, cap=cap, nb=nb),
        grid=(1,),
        in_specs=[pl.BlockSpec((N_EXPERTS, nb, LANES), lambda i: (0, 0, 0))],
        out_specs=[pl.BlockSpec((N_EXPERTS, nb, LANES), lambda i: (0, 0, 0)),
                   pl.BlockSpec((N_EXPERTS, nb), lambda i: (0, 0))],
        out_shape=[jax.ShapeDtypeStruct((N_EXPERTS, nb, LANES), I32),
                   jax.ShapeDtypeStruct((N_EXPERTS, nb), I32)],
        scratch_shapes=[pltpu.VMEM((N_EXPERTS, SUBLANES, LANES), I32)],
        compiler_params=_cparams(("arbitrary",)),
        name="select",
    )(a3)
    return rank.reshape(N_EXPERTS, n), off


def _ffn_body(x_ref, wg_ref, wu_ref, wd_ref, o_ref):
    w = x_ref[...]
    x = jnp.concatenate([pltpu.bitcast(lax.shift_left(w, 16), F32),
                         pltpu.bitcast(w & jnp.int32(-65536), F32)], axis=1).astype(BF16)
    tf = FFN_COLS
    acc = jnp.zeros(o_ref.shape, F32)
    for j in range(D_FF_EXPERT // tf):
        g = _dot(x, wg_ref[:, j * tf:(j + 1) * tf])
        u = _dot(x, wu_ref[:, j * tf:(j + 1) * tf])
        hdn = (g * jax.nn.sigmoid(g) * u).astype(BF16)
        acc = acc + _dot(hdn, wd_ref[j * tf:(j + 1) * tf, :])
    o_ref[...] = acc.astype(BF16)


def _ffn(xe, wg, wu, wd, layer):
    e, cap, _ = xe.shape
    tm = min(FFN_ROWS, cap)
    return pl.pallas_call(
        _ffn_body,
        grid=(e, cap // tm),
        in_specs=[pl.BlockSpec((None, tm, PACKED), lambda i, j: (i, j, 0)),
                  pl.BlockSpec((None, None, D_MODEL, D_FF_EXPERT), lambda i, j: (layer, i, 0, 0)),
                  pl.BlockSpec((None, None, D_MODEL, D_FF_EXPERT), lambda i, j: (layer, i, 0, 0)),
                  pl.BlockSpec((None, None, D_FF_EXPERT, D_MODEL), lambda i, j: (layer, i, 0, 0))],
        out_specs=pl.BlockSpec((None, tm, D_MODEL), lambda i, j: (i, j, 0)),
        out_shape=jax.ShapeDtypeStruct((e, cap, D_MODEL), BF16),
        compiler_params=_cparams(("arbitrary", "arbitrary")),
        name="ffn",
    )(xe, wg, wu, wd)


def _combine_body(off_ref, h_ref, rank_ref, aff_ref, p_ref, gp_ref, wg_ref, wp_ref, gfin_ref, ye_ref,
                  o_ref, stack, sems, *, cap, last):
    t = pl.program_id(0)
    nt = pl.num_programs(0)
    tile, win, nsub = COMBINE_TILE, COMBINE_WIN, COMBINE_SUB
    bpt = tile // LANES
    kdim = N_EXPERTS * win

    def base(tt, e):
        return (off_ref[e, tt * bpt] // BF16_ROWS) * BF16_ROWS

    def copy(e, start, slot, sub):
        return pltpu.make_async_copy(ye_ref.at[e, pl.ds(pl.multiple_of(start, BF16_ROWS), win), :],
                                     stack.at[slot, sub, pl.ds(e * win, win), :], sems.at[slot, sub, e])

    def starts(tt, r):
        want = [base(tt, e) + r * win for e in range(N_EXPERTS)]
        return want, [jnp.minimum(w, cap - win) for w in want]

    def issue(tt, r, slot, sub):
        _, got = starts(tt, r)
        for e in range(N_EXPERTS):
            copy(e, got[e], slot, sub).start(priority=e % 2)

    def wait(tt, r, slot, sub):
        _, got = starts(tt, r)
        for e in range(N_EXPERTS):
            copy(e, got[e], slot, sub).wait()

    slot = t % 2

    @pl.when(t == 0)
    def _():
        for sub in range(nsub):
            issue(sub, 0, 0, sub)

    @pl.when(t + 1 < nt)
    def _():
        for sub in range(nsub):
            issue((t + 1) * nsub + sub, 0, 1 - slot, sub)

    lane16 = lax.broadcasted_iota(I32, (1, N_EXPERTS), 1)
    er = lax.broadcasted_iota(I32, (N_EXPERTS, kdim), 0)
    ec = lax.broadcasted_iota(I32, (N_EXPERTS, kdim), 1)
    expand = jnp.where(ec // win == er, 1.0, 0.0).astype(BF16)
    lane_in = (lax.broadcasted_iota(I32, (1, kdim), 1) % win).astype(F32)

    def row_of(vals):
        r = jnp.zeros((1, N_EXPERTS), I32)
        for e in range(N_EXPERTS):
            r = jnp.where(lane16 == e, vals[e], r)
        return r

    def accumulate(sub, r):
        rows = pl.ds(sub * tile, tile)
        want, got = starts(t * nsub + sub, r)
        wrow, grow = row_of(want), row_of(got)
        rk = rank_ref[rows, :]
        ok = jnp.logical_and(rk >= wrow, rk < grow + win)
        rel = jnp.where(ok, rk - grow, -1).astype(F32).astype(BF16)
        hit = _dot(rel, expand) == lane_in
        gates = _dot(aff_ref[rows, :].astype(BF16), expand)
        w = jnp.where(hit, gates, 0.0).astype(BF16)
        return _dot(w, stack[slot, sub])

    for sub in range(nsub):
        wait(t * nsub + sub, 0, slot, sub)
    for sub in range(nsub):
        rows = pl.ds(sub * tile, tile)
        o_ref[rows, :] = h_ref[rows, :] + accumulate(sub, 0)

    rounds = []
    for sub in range(nsub):
        tt = t * nsub + sub
        nr = jnp.int32(1)
        for e in range(N_EXPERTS):
            span = off_ref[e, (tt + 1) * bpt] - base(tt, e)
            nr = jnp.maximum(nr, (span + win - 1) // win)
        rounds.append(nr)

    def extra(r, carry):
        for sub in range(nsub):
            @pl.when(r < rounds[sub])
            def _(sub=sub):
                issue(t * nsub + sub, r, slot, sub)
        for sub in range(nsub):
            @pl.when(r < rounds[sub])
            def _(sub=sub):
                wait(t * nsub + sub, r, slot, sub)
                o_ref[pl.ds(sub * tile, tile), :] += accumulate(sub, r)
        return carry

    lax.fori_loop(1, functools.reduce(jnp.maximum, rounds), extra, 0)

    for sub in range(nsub):
        rows = pl.ds(sub * tile, tile)
        h2 = o_ref[rows, :]
        gate = jax.nn.sigmoid(_dot(_rms(h2, gp_ref[...]).astype(BF16), wg_ref[...]))
        h3 = h2 + _dot(p_ref[rows, :].astype(BF16), wp_ref[...]) * gate
        o_ref[rows, :] = _rms(h3, gfin_ref[...]) if last else h3


def _combine(off, h1, rank_tok, aff_tok, p, layer, gp, wg, wp, gfin, ye, cap, last):
    n = h1.shape[0]
    tile = COMBINE_TILE * COMBINE_SUB
    row = lambda i, o: (i, 0)
    fix = lambda i, o: (0, 0)
    prow = lambda i, o: (layer * (n // tile) + i, 0)
    return pl.pallas_call(
        functools.partial(_combine_body, cap=cap, last=last),
        grid_spec=pltpu.PrefetchScalarGridSpec(
            num_scalar_prefetch=1,
            grid=(n // tile,),
            in_specs=[pl.BlockSpec((tile, D_MODEL), row),
                      pl.BlockSpec((tile, N_EXPERTS), row),
                      pl.BlockSpec((tile, N_EXPERTS), row),
                      pl.BlockSpec((tile, PLE_DIM), prow),
                      pl.BlockSpec((1, D_MODEL), fix),
                      pl.BlockSpec((D_MODEL, D_MODEL), fix),
                      pl.BlockSpec((PLE_DIM, D_MODEL), fix),
                      pl.BlockSpec((1, D_MODEL), fix),
                      pl.BlockSpec(memory_space=pl.ANY)],
            out_specs=pl.BlockSpec((tile, D_MODEL), row),
            scratch_shapes=[pltpu.VMEM((2, COMBINE_SUB, N_EXPERTS * COMBINE_WIN, D_MODEL), BF16),
                            pltpu.SemaphoreType.DMA((2, COMBINE_SUB, N_EXPERTS))],
        ),
        out_shape=jax.ShapeDtypeStruct((n, D_MODEL), F32),
        compiler_params=_cparams(("arbitrary",)),
        name="combine",
    )(off, h1, rank_tok, aff_tok, p, gp, wg, wp, gfin, ye)


def _gather_rows(table, rank, cap):
    n, w = table.shape
    workers_per_expert = SC_CORES * SC_SUBCORES // N_EXPERTS
    per = cap // workers_per_expert
    nch = per // GATHER_ROWS
    mesh = plsc.VectorSubcoreMesh(core_axis_name="c", subcore_axis_name="s",
                                  num_cores=SC_CORES, num_subcores=SC_SUBCORES)
    cp = dataclasses.replace(pltpu.CompilerParams(), needs_layout_passes=False)

    @functools.partial(
        pl.kernel, mesh=mesh, compiler_params=cp,
        out_type=jax.ShapeDtypeStruct((N_EXPERTS * cap, w), I32),
        scratch_types=[pltpu.VMEM((RANK_CHUNK,), I32),
                       pltpu.VMEM((nch, GATHER_ROWS), I32),
                       pltpu.VMEM((GATHER_ROWS, w), I32),
                       pltpu.SemaphoreType.DMA],
        name="sc_gather")
    def gather(table_hbm, rank_hbm, out_hbm, rbuf, idx, rows, sem):
        wid = lax.axis_index("s") * SC_CORES + lax.axis_index("c")
        e = wid // workers_per_expert
        lo = (wid % workers_per_expert) * per
        lane = lax.iota(I32, SC_LANES)

        @pl.loop(0, n // RANK_CHUNK)
        def _(c):
            pltpu.sync_copy(rank_hbm.at[e, pl.ds(c * RANK_CHUNK, RANK_CHUNK)], rbuf)

            @pl.loop(0, RANK_CHUNK // SC_LANES)
            def _(i):
                rel = rbuf[pl.ds(i * SC_LANES, SC_LANES)] - lo
                mask = jnp.logical_and(rel >= 0, rel < per)
                rel = jnp.where(mask, rel, 0)
                tok = c * RANK_CHUNK + i * SC_LANES + lane
                plsc.store_scatter(idx, [rel // GATHER_ROWS, rel % GATHER_ROWS], tok, mask=mask)

        @pl.loop(0, nch)
        def _(c):
            pltpu.async_copy(table_hbm.at[idx.at[c]], rows, sem).wait()
            pltpu.sync_copy(rows, out_hbm.at[pl.ds(e * cap + lo + c * GATHER_ROWS, GATHER_ROWS)])

    return gather(table, rank).reshape(N_EXPERTS, cap, w)


def _trunk(x, p, prm, b, s):
    n = b * s
    nc = s // SSM_CHUNK
    cap = EC_CAPACITY_FACTOR * n // N_EXPERTS
    tables = _fourier_tables(s)
    h = x.reshape(n, D_MODEL)
    depth = prm['w_in'].shape[0]
    p_rows = p.reshape(depth * n, PLE_DIM)
    for l in range(depth):
        lp = prm['layers'][l]
        zf, zs, zqkv = _inproj(h, lp['g_mix'], lp['w_in'])
        mf = _fourier(zf, tables, prm['ccb'], prm['scb'], lp['w_fnet'], lp['g_f'], b, s)
        xl = _s5_in(zs, lp['s5_w_in'])
        pr, pi = _scan_powers(lp['s5_a_chunk'], int(math.log2(nc)))
        st = _s5_scan(xl, pr, pi, b, nc)
        ms = _s5_out(zs, st, lp['s5_m_intra'], lp['s5_w_out'], lp['s5_d'], lp['s5_glu_w'], lp['s5_glu_b'],
                     lp['g_s'])
        oa = _attn(zqkv, prm['slopes'], b, s)
        h1, m, aff = _outproj(h, mf, ms, oa, lp['g_a'], lp['w_out'], lp['g_ffn'], lp['w_router'])
        rank, off = _select(aff.T, cap)
        off = jnp.concatenate([off, jnp.full((N_EXPERTS, 1), cap, I32)], axis=1)
        xe = _gather_rows(m, rank, cap)
        ye = _ffn(xe, prm['w_gate'], prm['w_up'], prm['w_down'], l)
        h = _combine(off, h1, rank.T, aff, p_rows, l, lp['g_ple'], lp['w_ple_gate'],
                     lp['w_ple_proj'], prm['g_final'], ye, cap, l == depth - 1)
    return h.reshape(b, s, D_MODEL)


def kernel(x_prompt, x_sample, p_prompt, p_sample, norm_mix, w_in, w_fnet, ssm_a_re, ssm_a_im, ssm_log_dt, ssm_b_re, ssm_b_im, ssm_c_re, ssm_c_im, ssm_d, ssm_glu_w, ssm_glu_b, norm_branch, w_out, norm_ffn, w_router, w_exp_gate, w_exp_up, w_exp_down, norm_ple, w_ple_gate, w_ple_proj, norm_final):
    depth = w_in.shape[0]
    o1, o2 = FN_WIDTH, FN_WIDTH + SSM_WIDTH
    row = lambda v: v.reshape(1, -1).astype(F32)
    cc, sc = _dft_tables(FN_HEAD_DIM)
    tile4 = lambda mtx: _block_diag(jnp.broadcast_to(mtx[None], (FN_HEADS,) + mtx.shape)).astype(BF16)
    layers = []
    for l in range(depth):
        m_intra, s5_w_in, s5_w_out, a_chunk = _s5_matrices(
            ssm_a_re[l], ssm_a_im[l], ssm_log_dt[l], ssm_b_re[l], ssm_b_im[l], ssm_c_re[l], ssm_c_im[l])
        layers.append(dict(
            g_mix=row(norm_mix[l]), w_in=w_in[l].astype(BF16),
            w_fnet=_block_diag(w_fnet[l]).astype(BF16),
            g_f=row(norm_branch[l][:o1]), g_s=row(norm_branch[l][o1:o2]), g_a=row(norm_branch[l][o2:]),
            s5_m_intra=m_intra, s5_w_in=s5_w_in, s5_w_out=s5_w_out, s5_a_chunk=a_chunk,
            s5_d=row(ssm_d[l]), s5_glu_w=ssm_glu_w[l].astype(BF16), s5_glu_b=row(ssm_glu_b[l]),
            w_out=w_out[l].astype(BF16), g_ffn=row(norm_ffn[l]),
            w_router=w_router[l].astype(BF16),
            g_ple=row(norm_ple[l]), w_ple_gate=w_ple_gate[l].astype(BF16),
            w_ple_proj=w_ple_proj[l].astype(BF16)))
    slopes = jnp.asarray([2.0 ** (-8.0 * (i + 1) / ATT_HEADS) for i in range(ATT_HEADS)], F32)
    prm = dict(w_in=w_in, layers=layers, ccb=tile4(cc), scb=tile4(sc), slopes=slopes,
               w_gate=w_exp_gate.astype(BF16), w_up=w_exp_up.astype(BF16), w_down=w_exp_down.astype(BF16),
               g_final=row(norm_final))
    bp, sp = x_prompt.shape[0], x_prompt.shape[1]
    bs, ssq = x_sample.shape[0], x_sample.shape[1]
    y_prompt = _trunk(x_prompt, p_prompt, prm, bp, sp)
    y_sample = _trunk(x_sample, p_sample, prm, bs, ssq)
    return (y_prompt, y_sample)
```
